```python
import math
import jax, jax.numpy as jnp
from jax import lax
import numpy as np

D_MODEL = 1024
BATCH = 2
SEQ = 16384
DEPTH = 2

GRID_W = 64
CTX_LEN = 256
N_BRANCH = 3
BRANCH_W = D_MODEL // 2
S5_P = 16
S5_N = 64
S5_G = BRANCH_W // S5_P
HEAD_DIM = 64
NA_HEADS = BRANCH_W // HEAD_DIM
NA_KR = 8
NA_KW = 16
AXIS_DIM = HEAD_DIM // 2
ROPE_HALF = AXIS_DIM // 2
ROPE_THETA = 10000.0
HY_ORDER = 2
HY_SHORT = 3
HY_EMB = 33
HY_BANDS = (HY_EMB - 1) // 2
HY_HID = 64
HY_DECAY_TARGET = 1e-2
HY_MIN_DECAY = math.log(HY_DECAY_TARGET) / 1.5
HY_MAX_DECAY = math.log(HY_DECAY_TARGET) / 0.3
EPS = 1e-6
COL_SPLITS = (BRANCH_W, 2 * BRANCH_W, 5 * BRANCH_W, 6 * BRANCH_W, 9 * BRANCH_W, 10 * BRANCH_W)
IN_COLS = 10 * BRANCH_W + N_BRANCH * D_MODEL

kernel_name = "hybrid_s5_natten_hyena_block"


def _rmsnorm(x, g):
    xf = x.astype(jnp.float32)
    y = xf * lax.rsqrt(jnp.mean(xf * xf, axis=-1, keepdims=True) + EPS)
    return (y * g.astype(jnp.float32)).astype(x.dtype)


def _rope_2d(x):
    L = x.shape[1]
    t = jnp.arange(L)
    rows = (t // GRID_W).astype(jnp.float32)
    cols = (t % GRID_W).astype(jnp.float32)
    inv = ROPE_THETA ** (-jnp.arange(ROPE_HALF, dtype=jnp.float32) / ROPE_HALF)
    xf = x.astype(jnp.float32)

    def rot(xa, pos):
        ang = pos[:, None] * inv
        cos = jnp.cos(ang)[None, :, None, :]
        sin = jnp.sin(ang)[None, :, None, :]
        x1, x2 = xa[..., :ROPE_HALF], xa[..., ROPE_HALF:]
        return jnp.concatenate([x1 * cos - x2 * sin, x2 * cos + x1 * sin], axis=-1)

    out = jnp.concatenate([rot(xf[..., :AXIS_DIM], rows), rot(xf[..., AXIS_DIM:], cols)], axis=-1)
    return out.astype(x.dtype)


def _linrec_combine(ei, ej):
    a_i, b_i = ei
    a_j, b_j = ej
    return a_j * a_i, a_j * b_i + b_j


def _flip(z, rev):
    return z[::-1] if rev else z


def _s5_scan(u, lam_bar, b_bar, h0):
    bu = jnp.einsum("lbgp,gnp->lbgn", u.astype(jnp.complex64), b_bar)
    if h0 is not None:
        bu = bu.at[0].add(lam_bar * h0)
    a = jnp.broadcast_to(lam_bar, (u.shape[0], 1) + lam_bar.shape)
    _, xs = lax.associative_scan(_linrec_combine, (a, bu), axis=0)
    return xs


def _s5_mixer(u_lat, u_ctx, lam_re, lam_im, log_dt, b_re, b_im, c_re, c_im, d_skip, glu_w, glu_b, ctx_out):
    f32 = jnp.float32
    dtype = u_lat.dtype

    def to_groups(u):
        return u.astype(f32).reshape(u.shape[0], u.shape[1], S5_G, S5_P).transpose(1, 0, 2, 3)

    ul, uc = to_groups(u_lat), to_groups(u_ctx)
    d = d_skip.astype(f32).reshape(S5_G, S5_P)
    y_l = ul * d
    y_c = uc * d if ctx_out else None
    for direction in range(2):
        rev = direction == 1
        lam = lax.complex(lam_re[direction].astype(f32), lam_im[direction].astype(f32))
        dt = jnp.exp(log_dt[direction].astype(f32))[:, None]
        lam_bar = jnp.exp(lam * dt)
        b_bar = ((lam_bar - 1.0) / lam)[..., None] * lax.complex(b_re[direction].astype(f32), b_im[direction].astype(f32))
        c_mat = lax.complex(c_re[direction].astype(f32), c_im[direction].astype(f32))
        xs_c = _s5_scan(_flip(uc, rev), lam_bar, b_bar, None)
        xs_l = _s5_scan(_flip(ul, rev), lam_bar, b_bar, xs_c[-1])
        y_l = y_l + _flip(jnp.einsum("lbgn,gpn->lbgp", xs_l, c_mat).real, rev)
        if ctx_out:
            y_c = y_c + _flip(jnp.einsum("lbgn,gpn->lbgp", xs_c, c_mat).real, rev)

    def readout(y):
        y = jax.nn.gelu(y.transpose(1, 0, 2, 3).reshape(y.shape[1], y.shape[0], BRANCH_W)).astype(dtype)
        val, gate = jnp.split(y @ glu_w + glu_b, 2, axis=-1)
        return val * jax.nn.sigmoid(gate)

    return readout(y_l), (readout(y_c) if ctx_out else None)


def _na_mixer(qkv_l, qkv_c, rpb, ctx_out):
    f32 = jnp.float32
    B_, L, _ = qkv_l.shape
    rows_n = L // GRID_W
    kr = min(NA_KR, rows_n)
    n_loc = kr * NA_KW

    def heads(t):
        return t.reshape(t.shape[0], t.shape[1], NA_HEADS, HEAD_DIM)

    q_l, k_l, v_l = [heads(t) for t in jnp.split(qkv_l, 3, axis=-1)]
    q_c, k_c, v_c = [heads(t) for t in jnp.split(qkv_c, 3, axis=-1)]
    scale = HEAD_DIM ** -0.5
    q_l = _rope_2d(q_l) * scale
    k_l = _rope_2d(k_l)
    grid = (B_, rows_n, GRID_W, NA_HEADS, HEAD_DIM)
    qg, kg, vg = q_l.reshape(grid), k_l.reshape(grid), v_l.reshape(grid)
    q_col = jnp.arange(GRID_W)
    col_idx = jnp.clip(q_col - NA_KW // 2, 0, GRID_W - NA_KW)[:, None] + jnp.arange(NA_KW)[None, :]
    rel_c = col_idx - q_col[:, None] + (NA_KW - 1)
    rpb32 = rpb.astype(f32)

    def row_block(r):
        rs = jnp.clip(r - kr // 2, 0, rows_n - kr)
        k_win = lax.dynamic_slice_in_dim(kg, rs, kr, axis=1)[:, :, col_idx]
        v_win = lax.dynamic_slice_in_dim(vg, rs, kr, axis=1)[:, :, col_idx]
        q_r = lax.dynamic_index_in_dim(qg, r, axis=1, keepdims=False)
        rel_r = rs + jnp.arange(kr) - r + (NA_KR - 1)
        bias = rpb32[:, rel_r[:, None, None], rel_c[None]].transpose(0, 2, 1, 3)
        s_loc = jnp.einsum("bchd,bicjhd->bhcij", q_r, k_win).astype(f32) + bias[None]
        s_ctx = jnp.einsum("bchd,bnhd->bhcn", q_r, k_c).astype(f32)
        s = jnp.concatenate([s_loc.reshape(B_, NA_HEADS, GRID_W, n_loc), s_ctx], axis=-1)
        p = jax.nn.softmax(s, axis=-1).astype(v_l.dtype)
        p_loc = p[..., :n_loc].reshape(B_, NA_HEADS, GRID_W, kr, NA_KW)
        return (jnp.einsum("bhcij,bicjhd->bchd", p_loc, v_win)
                + jnp.einsum("bhcn,bnhd->bchd", p[..., n_loc:], v_c))

    o_l = lax.map(row_block, jnp.arange(rows_n))
    o_l = o_l.transpose(1, 0, 2, 3, 4).reshape(B_, L, BRANCH_W)
    o_c = None
    if ctx_out:
        s = jnp.einsum("bqhd,bkhd->bhqk", q_c * scale, k_c).astype(f32)
        p = jax.nn.softmax(s, axis=-1).astype(v_c.dtype)
        o_c = jnp.einsum("bhqk,bkhd->bqhd", p, v_c).reshape(B_, qkv_c.shape[1], BRANCH_W)
    return o_l, o_c


def _short_conv(x, w, b):
    y = lax.conv_general_dilated(x, w.astype(x.dtype)[:, None, :], window_strides=(1,), padding=((1, 1),),
                                 dimension_numbers=("NWC", "WIO", "NWC"), feature_group_count=x.shape[-1])
    return y + b.astype(x.dtype)


def _hyena_filters(L, w1, b1, freq, w2, b2, w3):
    f32 = jnp.float32
    pos = jnp.arange(L, dtype=f32)
    t = pos / max(L - 1, 1)
    w = 2.0 * math.pi * pos / L
    bands = jnp.linspace(1e-4, HY_BANDS - 1, HY_BANDS, dtype=f32)
    feats = jnp.concatenate([t[:, None], jnp.cos(w[:, None] * bands), -jnp.sin(w[:, None] * bands)], axis=-1)
    fr = freq.astype(f32)
    h = jnp.sin(fr * (feats @ w1.astype(f32) + b1.astype(f32)))
    h = jnp.sin(fr * (h @ w2.astype(f32) + b2.astype(f32)))
    h = (h @ w3.astype(f32)).reshape(L, 2, HY_ORDER, BRANCH_W)
    deltas = jnp.linspace(HY_MIN_DECAY, HY_MAX_DECAY, BRANCH_W, dtype=f32)
    decay = jnp.exp(-t[:, None] * jnp.abs(deltas))
    h = h * decay[:, None, None, :]
    h = h * lax.rsqrt(jnp.sum(h * h, axis=(0, 1), keepdims=True) + EPS)
    k = jnp.concatenate([h[:, 0], jnp.zeros((1, HY_ORDER, BRANCH_W), f32), h[:0:-1, 1]], axis=0)
    return jnp.fft.rfft(k, axis=0)


def _fftconv(z, kf, d):
    L = z.shape[1]
    zf = jnp.fft.rfft(z, n=2 * L, axis=1)
    y = jnp.fft.irfft(zf * kf[None], n=2 * L, axis=1)[:, :L]
    return y + z * d


def _hyena_seq(p, conv_w, conv_b, w1, b1, freq, w2, b2, w3, hy_d):
    f32 = jnp.float32
    pc = _short_conv(p, conv_w, conv_b).astype(f32)
    v, x1, x2 = jnp.split(pc, 3, axis=-1)
    kf = _hyena_filters(p.shape[1], w1, b1, freq, w2, b2, w3)
    dd = hy_d.astype(f32)
    z = x1 * _fftconv(v, kf[:, 0], dd[0])
    y = x2 * _fftconv(z, kf[:, 1], dd[1])
    return y.astype(p.dtype)


def _merge(ys, zs, gates, w_branch, w_out):
    g = jnp.split(jax.nn.sigmoid(gates), N_BRANCH, axis=-1)
    acc = g[0] * ((jax.nn.silu(zs[0]) * ys[0]) @ w_branch[0])
    for i in range(1, N_BRANCH):
        acc = acc + g[i] * ((jax.nn.silu(zs[i]) * ys[i]) @ w_branch[i])
    return acc @ w_out


def setup_inputs(seed: int = 0) -> dict:
    key = jax.random.key(seed)
    ks = jax.random.split(key, 32)
    f32 = jnp.float32
    D = D_MODEL

    def nrm(k, shape, scale):
        return jax.random.normal(k, shape, f32) * scale

    n_idx = jnp.arange(S5_N, dtype=f32)
    s5_shape = (DEPTH, 2, S5_G, S5_N)
    return {
        "x": nrm(ks[0], (BATCH, SEQ, D), 1.0),
        "c": nrm(ks[1], (BATCH, D), 1.0),
        "ctx": nrm(ks[2], (BATCH, CTX_LEN, D), 1.0),
        "c_ctx": nrm(ks[3], (D,), 1.0),
        "ada_w": nrm(ks[4], (DEPTH, D, 3 * D), 0.5 * D ** -0.5),
        "ada_b": nrm(ks[5], (DEPTH, 3 * D), 0.01),
        "norm_g": 1.0 + nrm(ks[6], (DEPTH, D), 0.02),
        "w_in": nrm(ks[7], (DEPTH, D, IN_COLS), D ** -0.5),
        "s5_lam_re": -0.5 + nrm(ks[8], s5_shape, 0.01),
        "s5_lam_im": math.pi * n_idx + nrm(ks[9], s5_shape, 0.01),
        "s5_log_dt": jax.random.uniform(ks[10], (DEPTH, 2, S5_G), f32, math.log(1e-3), math.log(1e-1)),
        "s5_b_re": nrm(ks[11], (DEPTH, 2, S5_G, S5_N, S5_P), (2 * S5_P) ** -0.5),
        "s5_b_im": nrm(ks[12], (DEPTH, 2, S5_G, S5_N, S5_P), (2 * S5_P) ** -0.5),
        "s5_c_re": nrm(ks[13], (DEPTH, 2, S5_G, S5_P, S5_N), 0.5),
        "s5_c_im": nrm(ks[14], (DEPTH, 2, S5_G, S5_P, S5_N), 0.5),
        "s5_d": nrm(ks[15], (DEPTH, BRANCH_W), 1.0),
        "s5_glu_w": nrm(ks[16], (DEPTH, BRANCH_W, 2 * BRANCH_W), BRANCH_W ** -0.5),
        "s5_glu_b": nrm(ks[17], (DEPTH, 2 * BRANCH_W), 0.01),
        "na_rpb": nrm(ks[18], (DEPTH, NA_HEADS, 2 * NA_KR - 1, 2 * NA_KW - 1), 0.02),
        "hy_conv_w": nrm(ks[19], (DEPTH, HY_SHORT, 3 * BRANCH_W), HY_SHORT ** -0.5),
        "hy_conv_b": nrm(ks[20], (DEPTH, 3 * BRANCH_W), 0.01),
        "hf_w1": nrm(ks[21], (DEPTH, HY_EMB, HY_HID), HY_EMB ** -0.5),
        "hf_b1": nrm(ks[22], (DEPTH, HY_HID), 0.1),
        "hf_freq": 1.0 + nrm(ks[23], (DEPTH, HY_HID), 0.01),
        "hf_w2": nrm(ks[24], (DEPTH, HY_HID, HY_HID), HY_HID ** -0.5),
        "hf_b2": nrm(ks[25], (DEPTH, HY_HID), 0.1),
        "hf_w3": nrm(ks[26], (DEPTH, HY_HID, 2 * HY_ORDER * BRANCH_W), HY_HID ** -0.5),
        "hy_d": nrm(ks[27], (DEPTH, HY_ORDER, BRANCH_W), 0.5),
        "w_branch": nrm(ks[28], (DEPTH, N_BRANCH, BRANCH_W, D), BRANCH_W ** -0.5),
        "w_out": nrm(ks[29], (DEPTH, D, D), D ** -0.5),
        "final_g": 1.0 + nrm(ks[30], (D,), 0.02),
    }


def reference(x, c, ctx, c_ctx, ada_w, ada_b, norm_g, w_in, s5_lam_re, s5_lam_im, s5_log_dt, s5_b_re, s5_b_im,
              s5_c_re, s5_c_im, s5_d, s5_glu_w, s5_glu_b, na_rpb, hy_conv_w, hy_conv_b, hf_w1, hf_b1, hf_freq,
              hf_w2, hf_b2, hf_w3, hy_d, w_branch, w_out, final_g):
    x_lat, x_ctx = x, ctx
    for l in range(DEPTH):
        ctx_out = l < DEPTH - 1
        mod_l = jax.nn.silu(c) @ ada_w[l] + ada_b[l]
        mod_c = jax.nn.silu(c_ctx) @ ada_w[l] + ada_b[l]
        sh_l, sc_l, ga_l = jnp.split(mod_l[:, None, :], 3, axis=-1)
        sh_c, sc_c, ga_c = jnp.split(mod_c, 3, axis=-1)
        h_l = _rmsnorm(x_lat, norm_g[l]) * (1 + sc_l) + sh_l
        h_c = _rmsnorm(x_ctx, norm_g[l]) * (1 + sc_c) + sh_c
        ua_l, za_l, qkv_l, zb_l, hy_l, zc_l, gt_l = jnp.split(h_l @ w_in[l], COL_SPLITS, axis=-1)
        ua_c, za_c, qkv_c, zb_c, hy_c, zc_c, gt_c = jnp.split(h_c @ w_in[l], COL_SPLITS, axis=-1)
        ya_l, ya_c = _s5_mixer(ua_l, ua_c, s5_lam_re[l], s5_lam_im[l], s5_log_dt[l], s5_b_re[l], s5_b_im[l],
                               s5_c_re[l], s5_c_im[l], s5_d[l], s5_glu_w[l], s5_glu_b[l], ctx_out)
        yb_l, yb_c = _na_mixer(qkv_l, qkv_c, na_rpb[l], ctx_out)
        yc_l = _hyena_seq(hy_l, hy_conv_w[l], hy_conv_b[l], hf_w1[l], hf_b1[l], hf_freq[l], hf_w2[l], hf_b2[l],
                          hf_w3[l], hy_d[l])
        out_l = _merge((ya_l, yb_l, yc_l), (za_l, zb_l, zc_l), gt_l, w_branch[l], w_out[l])
        x_lat = x_lat + ga_l * out_l
        if ctx_out:
            yc_c = _hyena_seq(hy_c, hy_conv_w[l], hy_conv_b[l], hf_w1[l], hf_b1[l], hf_freq[l], hf_w2[l], hf_b2[l],
                              hf_w3[l], hy_d[l])
            out_c = _merge((ya_c, yb_c, yc_c), (za_c, zb_c, zc_c), gt_c, w_branch[l], w_out[l])
            x_ctx = x_ctx + ga_c * out_c
    return _rmsnorm(x_lat, final_g)
```

```python
import functools
import math

import numpy as np
import jax
import jax.numpy as jnp
from jax import lax
from jax.experimental import pallas as pl
from jax.experimental.pallas import tpu as pltpu

F32 = jnp.float32
BF16 = jnp.bfloat16
HIGHEST = lax.Precision.HIGHEST

GRID_W = 64
BRANCH_W = 512
S5_P = 16
S5_N = 64
S5_G = BRANCH_W // S5_P
S5_TC = 16
HEAD_DIM = 64
NA_HEADS = BRANCH_W // HEAD_DIM
NA_KR = 8
NA_KW = 16
ROPE_HALF = 16
ROPE_THETA = 10000.0
HY_EMB = 33
HY_BANDS = (HY_EMB - 1) // 2
HY_HID = 64
HY_MIN_DECAY = math.log(1e-2) / 1.5
HY_MAX_DECAY = math.log(1e-2) / 0.3
EPS = 1e-6
FFT_N2 = 256
NEG_BIG = -1e30
VMEM_LIMIT = 52 * 1024 * 1024

COL_UA, COL_ZA, COL_Q, COL_K, COL_V, COL_ZB, COL_HY, COL_ZC, COL_GT = (
    0, 512, 1024, 1536, 2048, 2560, 3072, 4608, 5120)
IN_COLS = 8192


def _cparams(sem):
    return pltpu.CompilerParams(dimension_semantics=sem, vmem_limit_bytes=VMEM_LIMIT)


def _silu(x):
    return x * jax.nn.sigmoid(x)


def _mod_kernel(c_ref, w_ref, b_ref, o_ref):
    s = _silu(c_ref[...])
    o_ref[...] = jnp.dot(s, w_ref[...], preferred_element_type=F32, precision=HIGHEST) + b_ref[...]


def _modulation(cvec, ada_w, ada_b):
    depth, d, d3 = ada_w.shape
    tn = 1024
    return pl.pallas_call(
        _mod_kernel,
        grid=(depth, d3 // tn),
        in_specs=[
            pl.BlockSpec((8, d), lambda l, j: (0, 0)),
            pl.BlockSpec((None, d, tn), lambda l, j: (l, 0, j)),
            pl.BlockSpec((None, 1, tn), lambda l, j: (l, 0, j)),
        ],
        out_specs=pl.BlockSpec((None, 8, tn), lambda l, j: (l, 0, j)),
        out_shape=jax.ShapeDtypeStruct((depth, 8, d3), F32),
        compiler_params=_cparams(("arbitrary", "arbitrary")),
        name="adaln_mod",
    )(cvec, ada_w, ada_b.reshape(depth, 1, d3))


def _inproj_kernel(x_ref, mod_ref, g_ref, w_ref, o_ref, h_ref, *, mod_row, d):
    @pl.when(pl.program_id(2) == 0)
    def _():
        x = x_ref[...]
        ms = jnp.mean(x * x, axis=-1, keepdims=True)
        y = x * lax.rsqrt(ms + EPS) * g_ref[...]
        if mod_row is None:
            m = mod_ref[pl.ds(pl.program_id(0), 1), :]
        else:
            m = mod_ref[mod_row:mod_row + 1, :]
        h_ref[...] = (y * (1.0 + m[:, d:2 * d]) + m[:, :d]).astype(BF16)

    o_ref[...] = jnp.dot(h_ref[...], w_ref[...], preferred_element_type=F32).astype(o_ref.dtype)


def _inproj(x, mod, g, w_bf, *, mod_row, tm):
    b, t, d = x.shape
    n = w_bf.shape[1]
    tn = 1024
    return pl.pallas_call(
        functools.partial(_inproj_kernel, mod_row=mod_row, d=d),
        grid=(b, t // tm, n // tn),
        in_specs=[
            pl.BlockSpec((None, tm, d), lambda bb, i, j: (bb, i, 0)),
            pl.BlockSpec((8, 3 * d), lambda bb, i, j: (0, 0)),
            pl.BlockSpec((1, d), lambda bb, i, j: (0, 0)),
            pl.BlockSpec((d, tn), lambda bb, i, j: (0, j)),
        ],
        out_specs=pl.BlockSpec((None, tm, tn), lambda bb, i, j: (bb, i, j)),
        out_shape=jax.ShapeDtypeStruct((b, t, n), BF16),
        scratch_shapes=[pltpu.VMEM((tm, d), BF16)],
        compiler_params=_cparams(("arbitrary", "arbitrary", "arbitrary")),
        name="norm_inproj",
    )(x, mod, g.reshape(1, d), w_bf)


def _s5_operators(lam_re, lam_im, log_dt, b_re, b_im, c_re, c_im, d_skip):
    tc, p, n, g = S5_TC, S5_P, S5_N, S5_G
    dt = jnp.exp(log_dt)[..., None]
    er = lam_re * dt
    ei = lam_im * dt
    k = jnp.arange(tc + 1, dtype=F32)[:, None, None, None]
    mag = jnp.exp(k * er[None])
    pw_r = mag * jnp.cos(k * ei[None])
    pw_i = mag * jnp.sin(k * ei[None])
    lb_r, lb_i = pw_r[1], pw_i[1]
    den = lam_re * lam_re + lam_im * lam_im
    q_r = ((lb_r - 1.0) * lam_re + lb_i * lam_im) / den
    q_i = (lb_i * lam_re - (lb_r - 1.0) * lam_im) / den
    bb_r = q_r[..., None] * b_re - q_i[..., None] * b_im
    bb_i = q_r[..., None] * b_im + q_i[..., None] * b_re

    def cmul(ar, ai, br, bi):
        return ar * br - ai * bi, ar * bi + ai * br

    cl_r, cl_i = cmul(c_re[None], c_im[None], pw_r[:, :, :, None, :], pw_i[:, :, :, None, :])
    kern = (jnp.einsum("kdgpn,dgnq->kdgpq", cl_r[:tc], bb_r, precision=HIGHEST)
            - jnp.einsum("kdgpn,dgnq->kdgpq", cl_i[:tc], bb_i, precision=HIGHEST))
    t_in = np.arange(tc)[:, None]
    t_out = np.arange(tc)[None, :]
    lag_f = t_out - t_in
    lag_b = t_in - t_out
    kf = jnp.where((lag_f >= 0)[:, :, None, None, None], kern[np.clip(lag_f, 0, tc - 1), 0], 0.0)
    kb = jnp.where((lag_b >= 0)[:, :, None, None, None], kern[np.clip(lag_b, 0, tc - 1), 1], 0.0)
    eye_t = jnp.asarray(np.eye(tc, dtype=np.float32))
    eye_p = jnp.asarray(np.eye(p, dtype=np.float32))
    kd = eye_t[:, :, None, None, None] * (d_skip.reshape(g, p)[None, None, :, :, None] * eye_p[None, None, None])
    m_sum = (kf + kb + kd).transpose(2, 0, 4, 1, 3).reshape(g, tc * p, tc * p)

    def state_in(pr, pi, dirn):
        sr, si = cmul(pr[..., None], pi[..., None], bb_r[dirn][None], bb_i[dirn][None])
        sr = sr.transpose(1, 0, 3, 2).reshape(g, tc * p, n)
        si = si.transpose(1, 0, 3, 2).reshape(g, tc * p, n)
        return jnp.concatenate([sr, si], -1), jnp.concatenate([si, sr], -1)

    bs_f, bs_f_sw = state_in(pw_r[:tc, 0][::-1], pw_i[:tc, 0][::-1], 0)
    bs_b, bs_b_sw = state_in(pw_r[:tc, 1], pw_i[:tc, 1], 1)
    wcat = jnp.concatenate([m_sum, bs_f, bs_f_sw, bs_b, bs_b_sw], -1)

    def state_out(dirn, powers):
        cr = cl_r[powers, dirn]
        ci = cl_i[powers, dirn]
        top = cr.transpose(1, 3, 0, 2).reshape(g, n, tc * p)
        bot = (-ci).transpose(1, 3, 0, 2).reshape(g, n, tc * p)
        return jnp.concatenate([top, bot], 1)

    cs_f = state_out(0, np.arange(1, tc + 1))
    cs_b = state_out(1, np.arange(tc, 0, -1))
    ccat = jnp.concatenate([cs_f, cs_b], 1)
    ar, ai = pw_r[tc], pw_i[tc]
    a1 = jnp.concatenate([ar, ar], -1)
    a2 = jnp.concatenate([-ai, ai], -1)
    a3 = jnp.concatenate([ai, -ai], -1)
    return wcat.astype(BF16), ccat.astype(BF16), a1, a2, a3


def _s5_kernel(x_ref, w_ref, c_ref, a1_ref, a2_ref, a3_ref, o_ref, spf, sqf, spb, sqb,
               *, gb, nch, nch_lat, pitch):
    for g in range(gb):
        r = jnp.dot(x_ref[g], w_ref[g], preferred_element_type=F32)
        o_ref[g] = r[:, :256]
        spf[g * pitch:g * pitch + nch, :] = r[:, 256:384]
        sqf[g * pitch:g * pitch + nch, :] = r[:, 384:512]
        spb[g * pitch:g * pitch + nch, :] = r[:, 512:640]
        sqb[g * pitch:g * pitch + nch, :] = r[:, 640:768]

    a1f, a2f, a3f = a1_ref[0], a2_ref[0], a3_ref[0]
    a1b, a2b, a3b = a1_ref[1], a2_ref[1], a3_ref[1]
    nch_ctx = nch - nch_lat

    def body(s, carry):
        pf, qf, pb, qb = carry
        cf = jnp.where(s < nch_ctx, s + nch_lat, s - nch_ctx)
        cb = nch - 1 - s
        idx_f = pl.ds(cf, gb, stride=pitch)
        idx_b = pl.ds(cb, gb, stride=pitch)
        sp = spf[idx_f, :]
        sq = sqf[idx_f, :]
        spf[idx_f, :] = pf
        pf, qf = pf * a1f + qf * a2f + sp, qf * a1f + pf * a3f + sq
        sp = spb[idx_b, :]
        sq = sqb[idx_b, :]
        spb[idx_b, :] = pb
        pb, qb = pb * a1b + qb * a2b + sp, qb * a1b + pb * a3b + sq
        return pf, qf, pb, qb

    z = jnp.zeros((gb, 128), F32)
    lax.fori_loop(0, nch, body, (z, z, z, z))

    for g in range(gb):
        hf = spf[g * pitch:g * pitch + nch, :].astype(BF16)
        hb = spb[g * pitch:g * pitch + nch, :].astype(BF16)
        cm = c_ref[g]
        y = (jnp.dot(hf, cm[:128, :], preferred_element_type=F32)
             + jnp.dot(hb, cm[128:, :], preferred_element_type=F32))
        o_ref[g] = o_ref[g] + y


def _s5_mixer(ua_lat, ua_ctx, ops):
    wcat, ccat, a1, a2, a3 = ops
    b, t_lat, _ = ua_lat.shape
    t_all = t_lat + ua_ctx.shape[1]
    nch, nch_lat = t_all // S5_TC, t_lat // S5_TC
    g, gb = S5_G, 8
    pitch = ((nch + 7) // 8) * 8 + 8
    ua = jnp.concatenate([ua_lat, ua_ctx], axis=1)
    xc = ua.reshape(b, nch, S5_TC, g, S5_P).transpose(0, 3, 1, 2, 4).reshape(b, g, nch, 256)
    kern = functools.partial(_s5_kernel, gb=gb, nch=nch, nch_lat=nch_lat, pitch=pitch)
    y = pl.pallas_call(
        kern,
        grid=(b, g // gb),
        in_specs=[
            pl.BlockSpec((None, gb, nch, 256), lambda bb, i: (bb, i, 0, 0)),
            pl.BlockSpec((gb, 256, 768), lambda bb, i: (i, 0, 0)),
            pl.BlockSpec((gb, 256, 256), lambda bb, i: (i, 0, 0)),
            pl.BlockSpec((2, gb, 128), lambda bb, i: (0, i, 0)),
            pl.BlockSpec((2, gb, 128), lambda bb, i: (0, i, 0)),
            pl.BlockSpec((2, gb, 128), lambda bb, i: (0, i, 0)),
        ],
        out_specs=pl.BlockSpec((None, gb, nch, 256), lambda bb, i: (bb, i, 0, 0)),
        out_shape=jax.ShapeDtypeStruct((b, g, nch, 256), F32),
        scratch_shapes=[pltpu.VMEM((gb * pitch, 128), F32) for _ in range(4)],
        compiler_params=_cparams(("arbitrary", "arbitrary")),
        name="s5_chunk_scan",
    )(xc, wcat, ccat, a1, a2, a3)
    y = y.reshape(b, g, nch, S5_TC, S5_P).transpose(0, 2, 3, 1, 4).reshape(b, t_all, BRANCH_W)
    return y.astype(BF16)


def _rope_tables(t_len):
    t = jnp.arange(t_len)
    rows = (t // GRID_W).astype(F32)
    cols = (t % GRID_W).astype(F32)
    inv = ROPE_THETA ** (-jnp.arange(ROPE_HALF, dtype=F32) / ROPE_HALF)
    lane = np.arange(128)
    dd = lane % HEAD_DIM
    fi = dd % ROPE_HALF
    use_row = jnp.asarray(dd < 32)
    sign = jnp.asarray(np.where((dd % 32) < ROPE_HALF, -1.0, 1.0).astype(np.float32))
    ang = jnp.where(use_row[None, :], rows[:, None], cols[:, None]) * inv[fi][None, :]
    return jnp.cos(ang), jnp.sin(ang) * sign[None, :]


def _rope_kernel(q_ref, k_ref, cos_ref, sin_ref, qo_ref, ko_ref, *, scale):
    cos = jnp.concatenate([cos_ref[...]] * 4, axis=1)
    sin = jnp.concatenate([sin_ref[...]] * 4, axis=1)
    lane = lax.broadcasted_iota(jnp.int32, cos.shape, 1)
    low = (lane % 32) < ROPE_HALF

    def rot(x):
        n = x.shape[1]
        partner = jnp.where(low, pltpu.roll(x, n - ROPE_HALF, 1), pltpu.roll(x, ROPE_HALF, 1))
        return x * cos + partner * sin

    qo_ref[...] = (rot(q_ref[...].astype(F32)) * scale).astype(BF16)
    ko_ref[...] = rot(k_ref[...].astype(F32)).astype(BF16)


def _rope_qk(p_lat, cos_t, sin_t):
    b, t, _ = p_lat.shape
    tm = 1024
    out = jax.ShapeDtypeStruct((b, t, BRANCH_W), BF16)
    return pl.pallas_call(
        functools.partial(_rope_kernel, scale=HEAD_DIM ** -0.5),
        grid=(b, t // tm),
        in_specs=[
            pl.BlockSpec((None, tm, 512), lambda bb, i: (bb, i, COL_Q // 512)),
            pl.BlockSpec((None, tm, 512), lambda bb, i: (bb, i, COL_K // 512)),
            pl.BlockSpec((tm, 128), lambda bb, i: (i, 0)),
            pl.BlockSpec((tm, 128), lambda bb, i: (i, 0)),
        ],
        out_specs=[pl.BlockSpec((None, tm, 512), lambda bb, i: (bb, i, 0))] * 2,
        out_shape=[out, out],
        compiler_params=_cparams(("arbitrary", "arbitrary")),
        name="rope_qk",
    )(p_lat, p_lat, cos_t, sin_t)


def _na_bias_tables(rpb, rows_n):
    h = rpb.shape[0]
    qr = np.arange(8)
    slot = np.arange(16)
    qc = np.arange(GRID_W)
    kc = np.arange(GRID_W)
    cs = np.clip(qc - NA_KW // 2, 0, GRID_W - NA_KW)
    valid_c = (kc[None, :] >= cs[:, None]) & (kc[None, :] < cs[:, None] + NA_KW)
    rel_c = np.clip(kc[None, :] - qc[:, None] + NA_KW - 1, 0, 2 * NA_KW - 2)
    oh_c = np.eye(2 * NA_KW - 1, dtype=np.float32)[rel_c]
    tables = []
    for r0, clipped in ((0, True), (8, False), (rows_n - 8, True)):
        r = r0 + qr
        rs = np.clip(r - NA_KR // 2, 0, rows_n - NA_KR) if clipped else r - NA_KR // 2
        kra = r0 - 4 + slot
        valid_r = (kra[None, :] >= rs[:, None]) & (kra[None, :] < rs[:, None] + NA_KR)
        rel_r = np.clip(kra[None, :] - r[:, None] + NA_KR - 1, 0, 2 * NA_KR - 2)
        oh_r = np.eye(2 * NA_KR - 1, dtype=np.float32)[rel_r]
        bias = jnp.einsum("rsa,hab,ckb->hrcsk", jnp.asarray(oh_r), rpb, jnp.asarray(oh_c), precision=HIGHEST)
        valid = valid_r[:, None, :, None] & valid_c[None, :, None, :]
        bias = jnp.where(jnp.asarray(valid)[None], bias, NEG_BIG)
        tables.append(bias.reshape(h, 8 * GRID_W, 16 * GRID_W))
    return jnp.stack(tables)


def _na_kernel(q_ref, kp_ref, kc_ref, kn_ref, vp_ref, vc_ref, vn_ref, kx_ref, vx_ref, bias_ref, o_ref):
    half = 4 * GRID_W
    kcat = jnp.concatenate([kp_ref[half:, :], kc_ref[...], kn_ref[:half, :]], axis=0)
    vcat = jnp.concatenate([vp_ref[half:, :], vc_ref[...], vn_ref[:half, :]], axis=0)
    q = q_ref[...]
    kx = kx_ref[...]
    vx = vx_ref[...]
    dn = (((1,), (1,)), ((), ()))
    outs = []
    for hh in range(2):
        sl = slice(hh * HEAD_DIM, (hh + 1) * HEAD_DIM)
        qh = q[:, sl]
        s_loc = lax.dot_general(qh, kcat[:, sl], dn, preferred_element_type=F32) + bias_ref[hh]
        s_ctx = lax.dot_general(qh, kx[:, sl], dn, preferred_element_type=F32)
        m = jnp.maximum(jnp.max(s_loc, axis=-1, keepdims=True), jnp.max(s_ctx, axis=-1, keepdims=True))
        p_loc = jnp.exp(s_loc - m)
        p_ctx = jnp.exp(s_ctx - m)
        den = jnp.sum(p_loc, axis=-1, keepdims=True) + jnp.sum(p_ctx, axis=-1, keepdims=True)
        o = (jnp.dot(p_loc.astype(BF16), vcat[:, sl], preferred_element_type=F32)
             + jnp.dot(p_ctx.astype(BF16), vx[:, sl], preferred_element_type=F32))
        outs.append(o / den)
    o_ref[...] = jnp.concatenate(outs, axis=1).astype(o_ref.dtype)


def _na_mixer(qr, kr, p_lat, p_ctx, bias):
    b, t, _ = qr.shape
    nc = p_ctx.shape[1]
    tq = 8 * GRID_W
    ni = t // tq

    def kind(i):
        return jnp.where(i == 0, 0, jnp.where(i == ni - 1, 2, 1))

    qk_spec = lambda off: pl.BlockSpec(
        (None, tq, 128), lambda hp, bb, i: (bb, jnp.clip(i + off, 0, ni - 1), hp))
    v_spec = lambda off: pl.BlockSpec(
        (None, tq, 128), lambda hp, bb, i: (bb, jnp.clip(i + off, 0, ni - 1), COL_V // 128 + hp))
    return pl.pallas_call(
        _na_kernel,
        grid=(NA_HEADS // 2, b, ni),
        in_specs=[
            qk_spec(0), qk_spec(-1), qk_spec(0), qk_spec(1),
            v_spec(-1), v_spec(0), v_spec(1),
            pl.BlockSpec((None, nc, 128), lambda hp, bb, i: (bb, 0, COL_K // 128 + hp)),
            pl.BlockSpec((None, nc, 128), lambda hp, bb, i: (bb, 0, COL_V // 128 + hp)),
            pl.BlockSpec((None, 2, tq, 2 * tq), lambda hp, bb, i: (kind(i), hp, 0, 0)),
        ],
        out_specs=pl.BlockSpec((None, tq, 128), lambda hp, bb, i: (bb, i, hp)),
        out_shape=jax.ShapeDtypeStruct((b, t, BRANCH_W), BF16),
        compiler_params=_cparams(("arbitrary", "arbitrary", "arbitrary")),
        name="na_attention",
    )(qr, kr, kr, kr, p_lat, p_lat, p_lat, p_ctx, p_ctx, bias)


def _ctx_attn_kernel(q_ref, k_ref, v_ref, o_ref, *, scale):
    dn = (((1,), (1,)), ((), ()))
    outs = []
    for hh in range(2):
        sl = slice(hh * HEAD_DIM, (hh + 1) * HEAD_DIM)
        qh = (q_ref[:, sl].astype(F32) * scale).astype(BF16)
        s = lax.dot_general(qh, k_ref[:, sl], dn, preferred_element_type=F32)
        m = jnp.max(s, axis=-1, keepdims=True)
        p = jnp.exp(s - m)
        den = jnp.sum(p, axis=-1, keepdims=True)
        outs.append(jnp.dot(p.astype(BF16), v_ref[:, sl], preferred_element_type=F32) / den)
    o_ref[...] = jnp.concatenate(outs, axis=1).astype(o_ref.dtype)


def _ctx_attention(p_ctx):
    b, nc, _ = p_ctx.shape
    spec = lambda col: pl.BlockSpec((None, nc, 128), lambda bb, hp: (bb, 0, col // 128 + hp))
    return pl.pallas_call(
        functools.partial(_ctx_attn_kernel, scale=HEAD_DIM ** -0.5),
        grid=(b, NA_HEADS // 2),
        in_specs=[spec(COL_Q), spec(COL_K), spec(COL_V)],
        out_specs=pl.BlockSpec((None, nc, 128), lambda bb, hp: (bb, 0, hp)),
        out_shape=jax.ShapeDtypeStruct((b, nc, BRANCH_W), BF16),
        compiler_params=_cparams(("arbitrary", "arbitrary")),
        name="ctx_attention",
    )(p_ctx, p_ctx, p_ctx)


def _filter_features(t_len):
    pos = jnp.arange(t_len, dtype=F32)
    t = pos / max(t_len - 1, 1)
    w = 2.0 * math.pi * pos / t_len
    bands = jnp.linspace(1e-4, HY_BANDS - 1, HY_BANDS, dtype=F32)
    feats = jnp.concatenate([t[:, None], jnp.cos(w[:, None] * bands), -jnp.sin(w[:, None] * bands)], axis=-1)
    feats2 = jnp.concatenate([feats, feats[:1], feats[:0:-1]], axis=0)
    return jnp.pad(feats2, ((0, 0), (0, 128 - HY_EMB)))


def _filter_kernel(f_ref, w1_ref, b1_ref, fr_ref, w2_ref, b2_ref, w3_ref, dl_ref, k_ref, ss_ref):
    f = f_ref[...]
    fr = fr_ref[...]
    h = jnp.sin(fr * (jnp.dot(f, w1_ref[...], preferred_element_type=F32, precision=HIGHEST) + b1_ref[...]))
    h = jnp.sin(fr * (jnp.dot(h, w2_ref[...], preferred_element_type=F32, precision=HIGHEST) + b2_ref[...]))
    k = jnp.dot(h, w3_ref[...], preferred_element_type=F32, precision=HIGHEST)
    k = k * jnp.exp(-f[:, 0:1] * dl_ref[...])
    k_ref[...] = k

    @pl.when(pl.program_id(0) == 0)
    def _():
        ss_ref[...] = jnp.zeros_like(ss_ref)

    ss_ref[...] += jnp.sum(k * k, axis=0, keepdims=True)


def _hyena_filter_time(t_len, w1, b1, freq, w2, b2, w3):
    n = 2 * t_len
    tm = min(1024, t_len)
    feats = _filter_features(t_len)
    w1p = jnp.pad(w1, ((0, 128 - HY_EMB), (0, 0)))
    deltas = jnp.abs(jnp.linspace(HY_MIN_DECAY, HY_MAX_DECAY, BRANCH_W, dtype=F32))
    dl = jnp.concatenate([deltas, deltas]).reshape(1, 2 * BRANCH_W)
    half = t_len // tm
    nco = 2 * BRANCH_W
    return pl.pallas_call(
        _filter_kernel,
        grid=(n // tm,),
        in_specs=[
            pl.BlockSpec((tm, 128), lambda i: (i, 0)),
            pl.BlockSpec((128, HY_HID), lambda i: (0, 0)),
            pl.BlockSpec((1, HY_HID), lambda i: (0, 0)),
            pl.BlockSpec((1, HY_HID), lambda i: (0, 0)),
            pl.BlockSpec((HY_HID, HY_HID), lambda i: (0, 0)),
            pl.BlockSpec((1, HY_HID), lambda i: (0, 0)),
            pl.BlockSpec((HY_HID, nco), lambda i: (0, i // half)),
            pl.BlockSpec((1, nco), lambda i: (0, 0)),
        ],
        out_specs=[pl.BlockSpec((tm, nco), lambda i: (i, 0)), pl.BlockSpec((1, nco), lambda i: (0, 0))],
        out_shape=[jax.ShapeDtypeStruct((n, nco), F32), jax.ShapeDtypeStruct((1, nco), F32)],
        compiler_params=_cparams(("arbitrary",)),
        name="hyena_filter_ffn",
    )(feats, w1p, b1.reshape(1, -1), freq.reshape(1, -1), w2, b2.reshape(1, -1), w3, dl)


def _dft_consts(n1):
    n2 = FFT_N2
    n = n1 * n2
    nh = n1 // 2
    a1 = -2.0 * np.pi * np.outer(np.arange(n1), np.arange(n1)) / n1
    f1r, f1i = np.cos(a1), np.sin(a1)
    a2 = -2.0 * np.pi * np.outer(np.arange(n2), np.arange(n2)) / n2
    f2r, f2i = np.cos(a2), np.sin(a2)
    at = -2.0 * np.pi * np.outer(np.arange(n1), np.arange(n2)) / n
    bf = lambda a: jnp.asarray(a.astype(np.float32)).astype(BF16)
    return dict(
        s1_data=bf(np.block([[f1r[:, :nh], -f1i[:, :nh]], [f1i[:, :nh], f1r[:, :nh]]])),
        s1_real=bf(np.concatenate([f1r, f1i], axis=0)),
        s6=bf(np.block([[f1r[:nh], f1i[:nh]], [-f1i[:nh], f1r[:nh]]]) / n),
        fa=bf(np.concatenate([f2r, f2i], axis=1)), fb=bf(np.concatenate([-f2i, f2r], axis=1)),
        ia=bf(np.concatenate([f2r, -f2i], axis=1)), ib=bf(np.concatenate([f2i, f2r], axis=1)),
        tw_r=jnp.asarray(np.cos(at).astype(np.float32)), tw_i=jnp.asarray(np.sin(at).astype(np.float32)),
    )


def _lane_dft(ar_ref, ai_ref, fa_ref, fb_ref):
    cb, n1, n2 = ar_ref.shape
    ar = ar_ref[...].reshape(cb * n1, n2).astype(BF16)
    ai = ai_ref[...].reshape(cb * n1, n2).astype(BF16)
    x = (jnp.dot(ar, fa_ref[...], preferred_element_type=F32)
         + jnp.dot(ai, fb_ref[...], preferred_element_type=F32))
    return x[:, :n2], x[:, n2:]


def _kf_kernel(k_ref, ss_ref, s1_ref, twr_ref, twi_ref, fa_ref, fb_ref, o_ref, ar_ref, ai_ref, *, cb, n1):
    i = pl.program_id(0)
    row = lax.broadcasted_iota(jnp.int32, (n1, FFT_N2), 0)
    lane = lax.broadcasted_iota(jnp.int32, (n1, FFT_N2), 1)
    keep = jnp.logical_not((row == n1 // 2) & (lane == 0))
    twr, twi = twr_ref[...], twi_ref[...]
    for c in range(cb):
        scale = lax.rsqrt(jnp.full((n1, FFT_N2), ss_ref[i * cb + c], F32) + EPS)
        k = jnp.where(keep, k_ref[c] * scale, 0.0)
        a = jnp.dot(s1_ref[...], k.astype(BF16), preferred_element_type=F32)
        a_r, a_i = a[:n1], a[n1:]
        ar_ref[c] = a_r * twr - a_i * twi
        ai_ref[c] = a_r * twi + a_i * twr
    xr, xi = _lane_dft(ar_ref, ai_ref, fa_ref, fb_ref)
    o_ref[:, :n1, :] = xr.reshape(cb, n1, FFT_N2)
    o_ref[:, n1:, :] = xi.reshape(cb, n1, FFT_N2)


def _hyena_filter_spectrum(k_time, ss, dft):
    n, nco = k_time.shape
    n1 = n // FFT_N2
    cb = 8
    kt = k_time.T.reshape(nco, n1, FFT_N2)
    const = lambda shape: pl.BlockSpec(shape, lambda i: (0,) * len(shape))
    return pl.pallas_call(
        functools.partial(_kf_kernel, cb=cb, n1=n1),
        grid=(nco // cb,),
        in_specs=[
            pl.BlockSpec((cb, n1, FFT_N2), lambda i: (i, 0, 0)),
            pl.BlockSpec(memory_space=pltpu.SMEM),
            const((2 * n1, n1)), const((n1, FFT_N2)), const((n1, FFT_N2)),
            const((FFT_N2, 2 * FFT_N2)), const((FFT_N2, 2 * FFT_N2)),
        ],
        out_specs=pl.BlockSpec((cb, 2 * n1, FFT_N2), lambda i: (i, 0, 0)),
        out_shape=jax.ShapeDtypeStruct((nco, 2 * n1, FFT_N2), F32),
        scratch_shapes=[pltpu.VMEM((cb, n1, FFT_N2), F32), pltpu.VMEM((cb, n1, FFT_N2), F32)],
        compiler_params=_cparams(("arbitrary",)),
        name="hyena_filter_fft",
    )(kt, ss.reshape(nco), dft["s1_real"], dft["tw_r"], dft["tw_i"], dft["fa"], dft["fb"])


def _hyena_kernel(v_ref, x1_ref, x2_ref, kf1_ref, kf2_ref, cw_ref, cbias_ref, dd_ref,
                  s1_ref, s6_ref, twr_ref, twi_ref, fa_ref, fb_ref, ia_ref, ib_ref,
                  o_ref, vs, x1s, x2s, ar_ref, ai_ref, *, cb, n1, nch):
    i = pl.program_id(0)
    nh = n1 // 2
    rows = 2 * nh
    row = lax.broadcasted_iota(jnp.int32, (rows, FFT_N2), 0)
    lane = lax.broadcasted_iota(jnp.int32, (rows, FFT_N2), 1)
    first = lane == 0
    last = lane == FFT_N2 - 1
    seq_start = first & ((row % nh) == 0)
    seq_end = last & ((row % nh) == nh - 1)
    twr, twi = twr_ref[...], twi_ref[...]

    def short_conv(x, ch):
        prev = pltpu.roll(x, 1, 1)
        prev = jnp.where(first, pltpu.roll(prev, 1, 0), prev)
        prev = jnp.where(seq_start, 0.0, prev)
        nxt = pltpu.roll(x, FFT_N2 - 1, 1)
        nxt = jnp.where(last, pltpu.roll(nxt, rows - 1, 0), nxt)
        nxt = jnp.where(seq_end, 0.0, nxt)
        return cw_ref[ch] * prev + cw_ref[nch + ch] * x + cw_ref[2 * nch + ch] * nxt + cbias_ref[ch]

    def stage1(z, c):
        a = jnp.dot(s1_ref[...], z.astype(BF16), preferred_element_type=F32)
        a_r, a_i = a[:n1], a[n1:]
        ar_ref[c] = a_r * twr - a_i * twi
        ai_ref[c] = a_r * twi + a_i * twr

    def spectral(kf_ref):
        xr, xi = _lane_dft(ar_ref, ai_ref, fa_ref, fb_ref)
        kr = kf_ref[:, :n1, :].reshape(cb * n1, FFT_N2)
        ki = kf_ref[:, n1:, :].reshape(cb * n1, FFT_N2)
        yr = (xr * kr - xi * ki).astype(BF16)
        yi = (xr * ki + xi * kr).astype(BF16)
        bm = (jnp.dot(yr, ia_ref[...], preferred_element_type=F32)
              + jnp.dot(yi, ib_ref[...], preferred_element_type=F32))
        br = bm[:, :FFT_N2].reshape(cb, n1, FFT_N2)
        bi = bm[:, FFT_N2:].reshape(cb, n1, FFT_N2)
        ar_ref[...] = br * twr[None] + bi * twi[None]
        ai_ref[...] = bi * twr[None] - br * twi[None]

    def stage6(c):
        bcat = jnp.concatenate([ar_ref[c], ai_ref[c]], axis=0).astype(BF16)
        return jnp.dot(s6_ref[...], bcat, preferred_element_type=F32)

    nbr = BRANCH_W
    for c in range(cb):
        ch = i * cb + c
        v = short_conv(v_ref[c].astype(F32), ch)
        vs[c] = v
        x1s[c] = short_conv(x1_ref[c].astype(F32), nbr + ch)
        x2s[c] = short_conv(x2_ref[c].astype(F32), 2 * nbr + ch)
        stage1(v, c)
    spectral(kf1_ref)
    for c in range(cb):
        ch = i * cb + c
        z = x1s[c] * (stage6(c) + vs[c] * dd_ref[ch])
        vs[c] = z
        stage1(z, c)
    spectral(kf2_ref)
    for c in range(cb):
        ch = i * cb + c
        o_ref[c] = (x2s[c] * (stage6(c) + vs[c] * dd_ref[nbr + ch])).astype(o_ref.dtype)


def _hyena_lat(hy_cm, kf, conv_w, conv_b, hy_d, dft):
    nch, rows, _ = hy_cm.shape
    n1 = rows
    cb = 8
    nblk = BRANCH_W // cb
    const = lambda shape: pl.BlockSpec(shape, lambda i: (0,) * len(shape))
    smem = pl.BlockSpec(memory_space=pltpu.SMEM)
    data = lambda sec: pl.BlockSpec((cb, rows, FFT_N2), lambda i: (sec * nblk + i, 0, 0))
    kfs = lambda o: pl.BlockSpec((cb, 2 * n1, FFT_N2), lambda i: (o * nblk + i, 0, 0))
    return pl.pallas_call(
        functools.partial(_hyena_kernel, cb=cb, n1=n1, nch=nch),
        grid=(nblk,),
        in_specs=[
            data(0), data(1), data(2), kfs(0), kfs(1), smem, smem, smem,
            const((2 * n1, n1)), const((n1, 2 * n1)), const((n1, FFT_N2)), const((n1, FFT_N2)),
            const((FFT_N2, 2 * FFT_N2)), const((FFT_N2, 2 * FFT_N2)),
            const((FFT_N2, 2 * FFT_N2)), const((FFT_N2, 2 * FFT_N2)),
        ],
        out_specs=pl.BlockSpec((cb, rows, FFT_N2), lambda i: (i, 0, 0)),
        out_shape=jax.ShapeDtypeStruct((BRANCH_W, rows, FFT_N2), BF16),
        scratch_shapes=[pltpu.VMEM((cb, rows, FFT_N2), F32) for _ in range(3)]
        + [pltpu.VMEM((cb, n1, FFT_N2), F32) for _ in range(2)],
        compiler_params=_cparams(("arbitrary",)),
        name="hyena_fftconv",
    )(hy_cm, hy_cm, hy_cm, kf, kf, conv_w.reshape(-1), conv_b, hy_d.reshape(-1),
      dft["s1_data"], dft["s6"], dft["tw_r"], dft["tw_i"], dft["fa"], dft["fb"], dft["ia"], dft["ib"])


def _hyena_ctx_kernel(hy_ref, k_ref, ss_ref, cw_ref, cbias_ref, dd_ref, fwd_a_ref, fwd_b_ref,
                      inv_a_ref, inv_b_ref, o_ref, *, t_len):
    nbr = BRANCH_W
    lane = lax.broadcasted_iota(jnp.int32, (nbr, t_len), 1)

    def short_conv(x, sec):
        sl = slice(sec * nbr, (sec + 1) * nbr)
        prev = jnp.where(lane == 0, 0.0, pltpu.roll(x, 1, 1))
        nxt = jnp.where(lane == t_len - 1, 0.0, pltpu.roll(x, t_len - 1, 1))
        return cw_ref[0, sl, :] * prev + cw_ref[1, sl, :] * x + cw_ref[2, sl, :] * nxt + cbias_ref[sl, :]

    def sec(b, s):
        return short_conv(hy_ref[b, s * nbr:(s + 1) * nbr, :].astype(F32), s)

    klane = lax.broadcasted_iota(jnp.int32, k_ref.shape, 1)
    kk = jnp.where(klane == t_len, 0.0, k_ref[...] * lax.rsqrt(ss_ref[...] + EPS))
    kf = jnp.dot(kk.astype(BF16), fwd_a_ref[...], preferred_element_type=F32)
    n = 2 * t_len

    def conv(z0, z1, order):
        x = (jnp.dot(z0.astype(BF16), fwd_a_ref[:t_len, :], preferred_element_type=F32)
             + jnp.dot(z1.astype(BF16), fwd_b_ref[:t_len, :], preferred_element_type=F32))
        xr, xi = x[:, :n], x[:, n:]
        kr = kf[order * nbr:(order + 1) * nbr, :n]
        ki = kf[order * nbr:(order + 1) * nbr, n:]
        yr = (xr * kr - xi * ki).astype(BF16)
        yi = (xr * ki + xi * kr).astype(BF16)
        y = (jnp.dot(yr, inv_a_ref[...], preferred_element_type=F32)
             + jnp.dot(yi, inv_b_ref[...], preferred_element_type=F32))
        return y[:, :t_len], y[:, t_len:]

    v0, v1 = sec(0, 0), sec(1, 0)
    y0, y1 = conv(v0, v1, 0)
    d1 = dd_ref[:nbr, :]
    d2 = dd_ref[nbr:, :]
    z0 = sec(0, 1) * (y0 + v0 * d1)
    z1 = sec(1, 1) * (y1 + v1 * d1)
    y0, y1 = conv(z0, z1, 1)
    o_ref[0] = (sec(0, 2) * (y0 + z0 * d2)).astype(o_ref.dtype)
    o_ref[1] = (sec(1, 2) * (y1 + z1 * d2)).astype(o_ref.dtype)


def _hyena_ctx(hy_ctx, k_time, ss, conv_w, conv_b, hy_d):
    b, t_len, nch = hy_ctx.shape
    n = 2 * t_len
    ang = -2.0 * np.pi * np.outer(np.arange(n), np.arange(n)) / n
    fr, fi = np.cos(ang), np.sin(ang)
    bf = lambda a: jnp.asarray(a.astype(np.float32)).astype(BF16)
    fwd_a = bf(np.concatenate([fr, fi], axis=1))
    fwd_b = bf(np.concatenate([-fi, fr], axis=1))
    inv_a = bf(np.concatenate([fr[:, :t_len], -fi[:, :t_len]], axis=1) / n)
    inv_b = bf(np.concatenate([fi[:, :t_len], fr[:, :t_len]], axis=1) / n)
    out = pl.pallas_call(
        functools.partial(_hyena_ctx_kernel, t_len=t_len),
        out_shape=jax.ShapeDtypeStruct((b, BRANCH_W, t_len), BF16),
        compiler_params=pltpu.CompilerParams(vmem_limit_bytes=VMEM_LIMIT),
        name="hyena_ctx",
    )(hy_ctx.transpose(0, 2, 1), k_time.T, ss.reshape(-1, 1), conv_w.reshape(3, nch, 1),
      conv_b.reshape(nch, 1), hy_d.reshape(-1, 1), fwd_a, fwd_b, inv_a, inv_b)
    return out.transpose(0, 2, 1)


def _merge_kernel(x_ref, ya_ref, yb_ref, yc_ref, za_ref, zb_ref, zc_ref, g0_ref, g1_ref, g2_ref, mod_ref,
                  gw_ref, gb_ref, wb_ref, wo_ref, fg_ref, o_ref, *, mod_row, d, final):
    ya = jax.nn.gelu(ya_ref[...].astype(F32)).astype(BF16)
    glu = jnp.dot(ya, gw_ref[...], preferred_element_type=F32) + gb_ref[...]
    y_a = glu[:, :BRANCH_W] * jax.nn.sigmoid(glu[:, BRANCH_W:])

    def branch(i, y, z_ref, g_ref):
        u = (_silu(z_ref[...].astype(F32)) * y).astype(BF16)
        return jax.nn.sigmoid(g_ref[...].astype(F32)) * jnp.dot(u, wb_ref[i], preferred_element_type=F32)

    acc = branch(0, y_a, za_ref, g0_ref)
    acc = acc + branch(1, yb_ref[...].astype(F32), zb_ref, g1_ref)
    acc = acc + branch(2, yc_ref[...].astype(F32), zc_ref, g2_ref)
    out = jnp.dot(acc.astype(BF16), wo_ref[...], preferred_element_type=F32)
    if mod_row is None:
        m = mod_ref[pl.ds(pl.program_id(0), 1), :]
    else:
        m = mod_ref[mod_row:mod_row + 1, :]
    xn = x_ref[...] + m[:, 2 * d:] * out
    if final:
        ms = jnp.mean(xn * xn, axis=-1, keepdims=True)
        xn = xn * lax.rsqrt(ms + EPS) * fg_ref[...]
    o_ref[...] = xn


def _merge(x, ya, yb, yc, p, mod, glu_w, glu_b, w_branch, w_out, final_g, *, mod_row, tm, final):
    b, t, d = x.shape
    tok = lambda w, col: pl.BlockSpec((None, tm, w), lambda bb, i: (bb, i, col // w))
    const = lambda shape: pl.BlockSpec(shape, lambda bb, i: (0,) * len(shape))
    return pl.pallas_call(
        functools.partial(_merge_kernel, mod_row=mod_row, d=d, final=final),
        grid=(b, t // tm),
        in_specs=[
            tok(d, 0), tok(512, 0), tok(512, 0), tok(512, 0),
            tok(512, COL_ZA), tok(512, COL_ZB), tok(512, COL_ZC),
            tok(1024, COL_GT), tok(1024, COL_GT + 1024), tok(1024, COL_GT + 2048),
            const((8, 3 * d)), const((BRANCH_W, 2 * BRANCH_W)), const((1, 2 * BRANCH_W)),
            const((3, BRANCH_W, d)), const((d, d)), const((1, d)),
        ],
        out_specs=tok(d, 0),
        out_shape=jax.ShapeDtypeStruct((b, t, d), F32),
        compiler_params=_cparams(("arbitrary", "arbitrary")),
        name="merge_out",
    )(x, ya, yb, yc, p, p, p, p, p, p, mod, glu_w, glu_b.reshape(1, -1), w_branch, w_out, final_g.reshape(1, d))


def kernel(x, c, ctx, c_ctx, ada_w, ada_b, norm_g, w_in, s5_lam_re, s5_lam_im, s5_log_dt, s5_b_re, s5_b_im,
           s5_c_re, s5_c_im, s5_d, s5_glu_w, s5_glu_b, na_rpb, hy_conv_w, hy_conv_b, hf_w1, hf_b1, hf_freq,
           hf_w2, hf_b2, hf_w3, hy_d, w_branch, w_out, final_g):
    bsz, t_lat, d = x.shape
    t_ctx = ctx.shape[1]
    depth = ada_w.shape[0]
    rows_n = t_lat // GRID_W
    assert bsz == 2 and d == 1024 and t_lat % (8 * GRID_W) == 0 and t_ctx % 128 == 0

    cvec = jnp.zeros((8, d), F32).at[:bsz].set(c).at[bsz].set(c_ctx)
    mods = _modulation(cvec, ada_w, ada_b)
    cos_t, sin_t = _rope_tables(t_lat)
    n1 = 2 * t_lat // FFT_N2
    dft = _dft_consts(n1)
    tm_lat = 1024 if t_lat % 1024 == 0 else 512

    x_lat, x_ctx = x, ctx
    for l in range(depth):
        ctx_out = l < depth - 1
        w_bf = w_in[l].astype(BF16)
        p_lat = _inproj(x_lat, mods[l], norm_g[l], w_bf, mod_row=None, tm=tm_lat)
        p_ctx = _inproj(x_ctx, mods[l], norm_g[l], w_bf, mod_row=bsz, tm=t_ctx)

        ops = _s5_operators(s5_lam_re[l], s5_lam_im[l], s5_log_dt[l], s5_b_re[l], s5_b_im[l],
                            s5_c_re[l], s5_c_im[l], s5_d[l])
        ya = _s5_mixer(p_lat[:, :, COL_UA:COL_UA + 512], p_ctx[:, :, COL_UA:COL_UA + 512], ops)

        qr, kr = _rope_qk(p_lat, cos_t, sin_t)
        bias = _na_bias_tables(na_rpb[l], rows_n)
        yb = _na_mixer(qr, kr, p_lat, p_ctx, bias)

        k_time, ss = _hyena_filter_time(t_lat, hf_w1[l], hf_b1[l], hf_freq[l], hf_w2[l], hf_b2[l], hf_w3[l])
        kf = _hyena_filter_spectrum(k_time, ss, dft)
        hy_cm = p_lat[:, :, COL_HY:COL_HY + 1536].transpose(2, 0, 1).reshape(1536, bsz * t_lat // FFT_N2, FFT_N2)
        yc_cm = _hyena_lat(hy_cm, kf, hy_conv_w[l], hy_conv_b[l], hy_d[l], dft)
        yc = yc_cm.reshape(BRANCH_W, bsz, t_lat).transpose(1, 2, 0)

        wb_bf = w_branch[l].astype(BF16)
        wo_bf = w_out[l].astype(BF16)
        gw_bf = s5_glu_w[l].astype(BF16)
        x_lat_new = _merge(x_lat, ya[:, :t_lat], yb, yc, p_lat, mods[l], gw_bf, s5_glu_b[l], wb_bf, wo_bf,
                           final_g, mod_row=None, tm=512, final=not ctx_out)
        if ctx_out:
            yb_c = _ctx_attention(p_ctx)
            kc_time, ss_c = _hyena_filter_time(t_ctx, hf_w1[l], hf_b1[l], hf_freq[l], hf_w2[l], hf_b2[l], hf_w3[l])
            yc_c = _hyena_ctx(p_ctx[:, :, COL_HY:COL_HY + 1536], kc_time, ss_c, hy_conv_w[l], hy_conv_b[l], hy_d[l])
            x_ctx = _merge(x_ctx, ya[:, t_lat:], yb_c, yc_c, p_ctx, mods[l], gw_bf, s5_glu_b[l], wb_bf, wo_bf,
                           final_g, mod_row=bsz, tm=t_ctx, final=False)
        x_lat = x_lat_new
    return x_lat
```

```python
import functools
import math

import numpy as np
import jax
import jax.numpy as jnp
from jax import lax
from jax.experimental import pallas as pl
from jax.experimental.pallas import tpu as pltpu

F32 = jnp.float32
BF16 = jnp.bfloat16
HIGHEST = lax.Precision.HIGHEST

GRID_W = 64
BRANCH_W = 512
S5_P = 16
S5_N = 64
S5_G = BRANCH_W // S5_P
S5_TC = 16
HEAD_DIM = 64
NA_HEADS = BRANCH_W // HEAD_DIM
NA_KR = 8
NA_KW = 16
ROPE_HALF = 16
ROPE_THETA = 10000.0
HY_EMB = 33
HY_BANDS = (HY_EMB - 1) // 2
HY_HID = 64
HY_MIN_DECAY = math.log(1e-2) / 1.5
HY_MAX_DECAY = math.log(1e-2) / 0.3
EPS = 1e-6
FFT_N2 = 256
NEG_BIG = -1e30
LOG2E = 1.4426950408889634
VMEM_LIMIT = 52 * 1024 * 1024

COL_UA, COL_ZA, COL_Q, COL_K, COL_V, COL_ZB, COL_HY, COL_ZC, COL_GT = (
    0, 512, 1024, 1536, 2048, 2560, 3072, 4608, 5120)
IN_COLS = 8192


def _cparams(sem):
    return pltpu.CompilerParams(dimension_semantics=sem, vmem_limit_bytes=VMEM_LIMIT)


def _silu(x):
    return x * jax.nn.sigmoid(x)


def _mod_kernel(c_ref, w_ref, b_ref, o_ref):
    s = _silu(c_ref[...])
    o_ref[...] = jnp.dot(s, w_ref[...], preferred_element_type=F32, precision=HIGHEST) + b_ref[...]


def _modulation(cvec, ada_w, ada_b):
    depth, d, d3 = ada_w.shape
    tn = 1024
    return pl.pallas_call(
        _mod_kernel,
        grid=(depth, d3 // tn),
        in_specs=[
            pl.BlockSpec((8, d), lambda l, j: (0, 0)),
            pl.BlockSpec((None, d, tn), lambda l, j: (l, 0, j)),
            pl.BlockSpec((None, 1, tn), lambda l, j: (l, 0, j)),
        ],
        out_specs=pl.BlockSpec((None, 8, tn), lambda l, j: (l, 0, j)),
        out_shape=jax.ShapeDtypeStruct((depth, 8, d3), F32),
        compiler_params=_cparams(("arbitrary", "arbitrary")),
        name="adaln_mod",
    )(cvec, ada_w, ada_b.reshape(depth, 1, d3))


def _inproj_kernel(x_ref, mod_ref, g_ref, w_ref, o_ref, h_ref, *, mod_row, d):
    @pl.when(pl.program_id(2) == 0)
    def _():
        x = x_ref[...]
        ms = jnp.mean(x * x, axis=-1, keepdims=True)
        y = x * lax.rsqrt(ms + EPS) * g_ref[...]
        if mod_row is None:
            m = mod_ref[pl.ds(pl.program_id(0), 1), :]
        else:
            m = mod_ref[mod_row:mod_row + 1, :]
        h_ref[...] = (y * (1.0 + m[:, d:2 * d]) + m[:, :d]).astype(BF16)

    o_ref[...] = jnp.dot(h_ref[...], w_ref[...], preferred_element_type=F32).astype(o_ref.dtype)


def _inproj(x, mod, g, w_bf, *, mod_row, tm):
    b, t, d = x.shape
    n = w_bf.shape[1]
    tn = 1024
    return pl.pallas_call(
        functools.partial(_inproj_kernel, mod_row=mod_row, d=d),
        grid=(b, t // tm, n // tn),
        in_specs=[
            pl.BlockSpec((None, tm, d), lambda bb, i, j: (bb, i, 0)),
            pl.BlockSpec((8, 3 * d), lambda bb, i, j: (0, 0)),
            pl.BlockSpec((1, d), lambda bb, i, j: (0, 0)),
            pl.BlockSpec((d, tn), lambda bb, i, j: (0, j)),
        ],
        out_specs=pl.BlockSpec((None, tm, tn), lambda bb, i, j: (bb, i, j)),
        out_shape=jax.ShapeDtypeStruct((b, t, n), BF16),
        scratch_shapes=[pltpu.VMEM((tm, d), BF16)],
        compiler_params=_cparams(("arbitrary", "arbitrary", "arbitrary")),
        name="norm_inproj",
    )(x, mod, g.reshape(1, d), w_bf)


def _s5_operators(lam_re, lam_im, log_dt, b_re, b_im, c_re, c_im, d_skip):
    tc, p, n, g = S5_TC, S5_P, S5_N, S5_G
    dt = jnp.exp(log_dt)[..., None]
    er = lam_re * dt
    ei = lam_im * dt
    k = jnp.arange(tc + 1, dtype=F32)[:, None, None, None]
    mag = jnp.exp(k * er[None])
    pw_r = mag * jnp.cos(k * ei[None])
    pw_i = mag * jnp.sin(k * ei[None])
    lb_r, lb_i = pw_r[1], pw_i[1]
    den = lam_re * lam_re + lam_im * lam_im
    q_r = ((lb_r - 1.0) * lam_re + lb_i * lam_im) / den
    q_i = (lb_i * lam_re - (lb_r - 1.0) * lam_im) / den
    bb_r = q_r[..., None] * b_re - q_i[..., None] * b_im
    bb_i = q_r[..., None] * b_im + q_i[..., None] * b_re

    def cmul(ar, ai, br, bi):
        return ar * br - ai * bi, ar * bi + ai * br

    cl_r, cl_i = cmul(c_re[None], c_im[None], pw_r[:, :, :, None, :], pw_i[:, :, :, None, :])
    kern = (jnp.einsum("kdgpn,dgnq->kdgpq", cl_r[:tc], bb_r, precision=HIGHEST)
            - jnp.einsum("kdgpn,dgnq->kdgpq", cl_i[:tc], bb_i, precision=HIGHEST))
    t_in = np.arange(tc)[:, None]
    t_out = np.arange(tc)[None, :]
    lag_f = t_out - t_in
    lag_b = t_in - t_out
    kf = jnp.where((lag_f >= 0)[:, :, None, None, None], kern[np.clip(lag_f, 0, tc - 1), 0], 0.0)
    kb = jnp.where((lag_b >= 0)[:, :, None, None, None], kern[np.clip(lag_b, 0, tc - 1), 1], 0.0)
    eye_t = jnp.asarray(np.eye(tc, dtype=np.float32))
    eye_p = jnp.asarray(np.eye(p, dtype=np.float32))
    kd = eye_t[:, :, None, None, None] * (d_skip.reshape(g, p)[None, None, :, :, None] * eye_p[None, None, None])
    m_sum = (kf + kb + kd).transpose(2, 0, 4, 1, 3).reshape(g, tc * p, tc * p)

    def state_in(pr, pi, dirn):
        sr, si = cmul(pr[..., None], pi[..., None], bb_r[dirn][None], bb_i[dirn][None])
        sr = sr.transpose(1, 0, 3, 2).reshape(g, tc * p, n)
        si = si.transpose(1, 0, 3, 2).reshape(g, tc * p, n)
        return jnp.concatenate([sr, si], -1), jnp.concatenate([si, sr], -1)

    bs_f, bs_f_sw = state_in(pw_r[:tc, 0][::-1], pw_i[:tc, 0][::-1], 0)
    bs_b, bs_b_sw = state_in(pw_r[:tc, 1], pw_i[:tc, 1], 1)
    wcat = jnp.concatenate([m_sum, bs_f, bs_f_sw, bs_b, bs_b_sw], -1)

    def state_out(dirn, powers):
        cr = cl_r[powers, dirn]
        ci = cl_i[powers, dirn]
        top = cr.transpose(1, 3, 0, 2).reshape(g, n, tc * p)
        bot = (-ci).transpose(1, 3, 0, 2).reshape(g, n, tc * p)
        return jnp.concatenate([top, bot], 1)

    cs_f = state_out(0, np.arange(1, tc + 1))
    cs_b = state_out(1, np.arange(tc, 0, -1))
    ccat = jnp.concatenate([cs_f, cs_b], 1)
    ar, ai = pw_r[tc], pw_i[tc]
    a1 = jnp.concatenate([ar, ar], -1)
    a2 = jnp.concatenate([-ai, ai], -1)
    a3 = jnp.concatenate([ai, -ai], -1)
    return wcat.astype(BF16), ccat.astype(BF16), a1, a2, a3


def _s5_kernel(ul_ref, uc_ref, w_ref, c_ref, a1_ref, a2_ref, a3_ref, yl_ref, yc_ref,
               tok, xg, spf, sqf, spb, sqb, *, gb, nch, nch_lat, pitch):
    t_lat = nch_lat * S5_TC
    lane = lax.broadcasted_iota(jnp.int32, (nch, 128), 1)
    tok[:t_lat, :] = ul_ref[...].astype(F32)
    tok[t_lat:, :] = uc_ref[...].astype(F32)

    for t in range(S5_TC):
        z = tok[pl.ds(t, nch, stride=S5_TC), :]
        dst = (t % 8) * S5_P
        for g in range(gb):
            shift = (dst - g * S5_P) % 128
            zr = pltpu.roll(z, shift, 1) if shift else z
            xg[g, :, t * S5_P:(t + 1) * S5_P] = zr[:, dst:dst + S5_P]

    for g in range(gb):
        r = jnp.dot(xg[g].astype(BF16), w_ref[g], preferred_element_type=F32)
        xg[g] = r[:, :256]
        spf[g * pitch:g * pitch + nch, :] = r[:, 256:384]
        sqf[g * pitch:g * pitch + nch, :] = r[:, 384:512]
        spb[g * pitch:g * pitch + nch, :] = r[:, 512:640]
        sqb[g * pitch:g * pitch + nch, :] = r[:, 640:768]

    a1f, a2f, a3f = a1_ref[0], a2_ref[0], a3_ref[0]
    a1b, a2b, a3b = a1_ref[1], a2_ref[1], a3_ref[1]
    nch_ctx = nch - nch_lat

    def body(s, carry):
        pf, qf, pb, qb = carry
        cf = jnp.where(s < nch_ctx, s + nch_lat, s - nch_ctx)
        cb = nch - 1 - s
        idx_f = pl.ds(cf, gb, stride=pitch)
        idx_b = pl.ds(cb, gb, stride=pitch)
        sp = spf[idx_f, :]
        sq = sqf[idx_f, :]
        spf[idx_f, :] = pf
        pf, qf = pf * a1f + qf * a2f + sp, qf * a1f + pf * a3f + sq
        sp = spb[idx_b, :]
        sq = sqb[idx_b, :]
        spb[idx_b, :] = pb
        pb, qb = pb * a1b + qb * a2b + sp, qb * a1b + pb * a3b + sq
        return pf, qf, pb, qb

    z0 = jnp.zeros((gb, 128), F32)
    lax.fori_loop(0, nch, body, (z0, z0, z0, z0))

    for g in range(gb):
        hf = spf[g * pitch:g * pitch + nch, :].astype(BF16)
        hb = spb[g * pitch:g * pitch + nch, :].astype(BF16)
        cm = c_ref[g]
        y = (jnp.dot(hf, cm[:128, :], preferred_element_type=F32)
             + jnp.dot(hb, cm[128:, :], preferred_element_type=F32))
        xg[g] = xg[g] + y

    for t in range(S5_TC):
        half = (t // 8) * 128
        src = (t % 8) * S5_P
        o = jnp.zeros((nch, 128), F32)
        for g in range(gb):
            shift = (g * S5_P - src) % 128
            y = xg[g, :, half:half + 128]
            yr = pltpu.roll(y, shift, 1) if shift else y
            o = jnp.where((lane >= g * S5_P) & (lane < (g + 1) * S5_P), yr, o)
        tok[pl.ds(t, nch, stride=S5_TC), :] = o
    yl_ref[...] = tok[:t_lat, :].astype(yl_ref.dtype)
    yc_ref[...] = tok[t_lat:, :].astype(yc_ref.dtype)


def _s5_mixer(p_lat, p_ctx, ops):
    wcat, ccat, a1, a2, a3 = ops
    b, t_lat, _ = p_lat.shape
    t_ctx = p_ctx.shape[1]
    nch, nch_lat = (t_lat + t_ctx) // S5_TC, t_lat // S5_TC
    gb = 128 // S5_P
    pitch = ((nch + 7) // 8) * 8 + 8
    kern = functools.partial(_s5_kernel, gb=gb, nch=nch, nch_lat=nch_lat, pitch=pitch)
    once = pl.Buffered(1)
    return pl.pallas_call(
        kern,
        grid=(b, S5_G // gb),
        in_specs=[
            pl.BlockSpec((None, t_lat, 128), lambda bb, i: (bb, 0, COL_UA // 128 + i), pipeline_mode=once),
            pl.BlockSpec((None, t_ctx, 128), lambda bb, i: (bb, 0, COL_UA // 128 + i)),
            pl.BlockSpec((gb, 256, 768), lambda bb, i: (i, 0, 0)),
            pl.BlockSpec((gb, 256, 256), lambda bb, i: (i, 0, 0)),
            pl.BlockSpec((2, gb, 128), lambda bb, i: (0, i, 0)),
            pl.BlockSpec((2, gb, 128), lambda bb, i: (0, i, 0)),
            pl.BlockSpec((2, gb, 128), lambda bb, i: (0, i, 0)),
        ],
        out_specs=[
            pl.BlockSpec((None, t_lat, 128), lambda bb, i: (bb, 0, i), pipeline_mode=once),
            pl.BlockSpec((None, t_ctx, 128), lambda bb, i: (bb, 0, i)),
        ],
        out_shape=[jax.ShapeDtypeStruct((b, t_lat, BRANCH_W), BF16),
                   jax.ShapeDtypeStruct((b, t_ctx, BRANCH_W), BF16)],
        scratch_shapes=[pltpu.VMEM((t_lat + t_ctx, 128), F32), pltpu.VMEM((gb, nch, 256), F32)]
        + [pltpu.VMEM((gb * pitch, 128), F32) for _ in range(4)],
        compiler_params=_cparams(("arbitrary", "arbitrary")),
        name="s5_chunk_scan",
    )(p_lat, p_ctx, wcat, ccat, a1, a2, a3)


def _rope_tables(t_len):
    t = jnp.arange(t_len)
    rows = (t // GRID_W).astype(F32)
    cols = (t % GRID_W).astype(F32)
    inv = ROPE_THETA ** (-jnp.arange(ROPE_HALF, dtype=F32) / ROPE_HALF)
    lane = np.arange(128)
    dd = lane % HEAD_DIM
    fi = dd % ROPE_HALF
    use_row = jnp.asarray(dd < 32)
    sign = jnp.asarray(np.where((dd % 32) < ROPE_HALF, -1.0, 1.0).astype(np.float32))
    ang = jnp.where(use_row[None, :], rows[:, None], cols[:, None]) * inv[fi][None, :]
    return jnp.cos(ang), jnp.sin(ang) * sign[None, :]


def _rope_kernel(q_ref, k_ref, cos_ref, sin_ref, qo_ref, kt_ref, *, scale):
    cos = jnp.concatenate([cos_ref[...]] * 4, axis=1)
    sin = jnp.concatenate([sin_ref[...]] * 4, axis=1)
    lane = lax.broadcasted_iota(jnp.int32, cos.shape, 1)
    low = (lane % 32) < ROPE_HALF

    def rot(x):
        n = x.shape[1]
        partner = jnp.where(low, pltpu.roll(x, n - ROPE_HALF, 1), pltpu.roll(x, ROPE_HALF, 1))
        return x * cos + partner * sin

    qo_ref[...] = (rot(q_ref[...].astype(F32)) * scale).astype(BF16)
    kr = rot(k_ref[...].astype(F32))
    for c in range(kt_ref.shape[0]):
        kt_ref[c] = kr[c * 128:(c + 1) * 128, :].T.astype(BF16)


def _rope_qk(p_lat, cos_t, sin_t):
    b, t, _ = p_lat.shape
    tm = 1024
    return pl.pallas_call(
        functools.partial(_rope_kernel, scale=HEAD_DIM ** -0.5 * LOG2E),
        grid=(b, t // tm),
        in_specs=[
            pl.BlockSpec((None, tm, 512), lambda bb, i: (bb, i, COL_Q // 512)),
            pl.BlockSpec((None, tm, 512), lambda bb, i: (bb, i, COL_K // 512)),
            pl.BlockSpec((tm, 128), lambda bb, i: (i, 0)),
            pl.BlockSpec((tm, 128), lambda bb, i: (i, 0)),
        ],
        out_specs=[pl.BlockSpec((None, tm, 512), lambda bb, i: (bb, i, 0)),
                   pl.BlockSpec((None, tm // 128, 512, 128), lambda bb, i: (bb, i, 0, 0))],
        out_shape=[jax.ShapeDtypeStruct((b, t, BRANCH_W), BF16),
                   jax.ShapeDtypeStruct((b, t // 128, BRANCH_W, 128), BF16)],
        compiler_params=_cparams(("arbitrary", "arbitrary")),
        name="rope_qk",
    )(p_lat, p_lat, cos_t, sin_t)


_NA_VARIANTS = (((4, 5), (0, 1)), ((0, 1), (0, 0)), ((2, 3), (0, 0)), ((4, 5), (0, 0)), ((6, 7), (0, 0)))
NA_WIN = NA_KR + 2


def _na_bias_tables(rpb):
    h = rpb.shape[0]
    qc = np.arange(GRID_W)
    kc = np.arange(GRID_W)
    cs = np.clip(qc - NA_KW // 2, 0, GRID_W - NA_KW)
    valid_c = (kc[None, :] >= cs[:, None]) & (kc[None, :] < cs[:, None] + NA_KW)
    rel_c = np.clip(kc[None, :] - qc[:, None] + NA_KW - 1, 0, 2 * NA_KW - 2)
    oh_c = np.eye(2 * NA_KW - 1, dtype=np.float32)[rel_c]
    j = np.arange(NA_WIN)
    oh_r = np.zeros((len(_NA_VARIANTS), 2, NA_WIN, 2 * NA_KR - 1), np.float32)
    valid_r = np.zeros((len(_NA_VARIANTS), 2, NA_WIN), bool)
    for n, (offs, los) in enumerate(_NA_VARIANTS):
        for rr in range(2):
            valid_r[n, rr] = (j >= los[rr]) & (j < los[rr] + NA_KR)
            rel_r = np.clip(j - offs[rr] + NA_KR - 1, 0, 2 * NA_KR - 2)
            oh_r[n, rr] = np.eye(2 * NA_KR - 1, dtype=np.float32)[rel_r]
    bias = jnp.einsum("nrja,hab,ckb->nhrcjk", jnp.asarray(oh_r), rpb, jnp.asarray(oh_c), precision=HIGHEST) * LOG2E
    valid = valid_r[:, None, :, None, :, None] & valid_c[None, None, None, :, None, :]
    bias = jnp.where(jnp.asarray(valid), bias, NEG_BIG)
    return bias.reshape(len(_NA_VARIANTS), h // 2, 4 * GRID_W, NA_WIN * GRID_W)


def _na_kernel(q_ref, kp_ref, kc_ref, kn_ref, vp_ref, vc_ref, vn_ref, kx_ref, vx_ref, bias_ref, o_ref,
               kt3, vcat, *, ni):
    half = 4 * GRID_W
    tq = 8 * GRID_W
    kt3[0:2] = kp_ref[2:4]
    kt3[2:6] = kc_ref[...]
    kt3[6:8] = kn_ref[0:2]
    vcat[:half, :] = vp_ref[half:, :]
    vcat[half:half + tq, :] = vc_ref[...]
    vcat[half + tq:, :] = vn_ref[:half, :]
    kxt = kx_ref[...].astype(F32).T.astype(BF16)
    i = pl.program_id(1)
    head0 = lax.broadcasted_iota(jnp.int32, (2 * GRID_W, 128), 1) < HEAD_DIM
    nchunk = NA_WIN // 2

    def pair_body(a, carry):
        first = i == 0
        last = i == ni - 1
        s0 = jnp.where(first, jnp.maximum(2 * a, 4), jnp.where(last, jnp.minimum(2 * a, 4), 2 * a))
        var = jnp.where(first & (a < 2), 1 + a, jnp.where(last & (a >= 2), 1 + a, 0))
        q0 = pl.multiple_of(a * 2 * GRID_W, 2 * GRID_W)
        k0 = pl.multiple_of(s0 * GRID_W, 2 * GRID_W)
        sp = s0 // 2
        outs = []
        for hp in range(NA_HEADS // 2):
            ls = slice(hp * 128, (hp + 1) * 128)
            qp = q_ref[pl.ds(q0, 2 * GRID_W), ls]
            zero = jnp.zeros_like(qp)
            q2 = jnp.concatenate([jnp.where(head0, qp, zero), jnp.where(head0, zero, qp)], axis=0)
            kw = kt3[pl.ds(sp, nchunk), ls, :]
            s_loc = jnp.concatenate(
                [jnp.dot(q2, kw[c], preferred_element_type=F32) for c in range(nchunk)], axis=1) + bias_ref[var, hp]
            s_ctx = jnp.dot(q2, kxt[ls, :], preferred_element_type=F32)
            m = jnp.maximum(jnp.max(s_loc, axis=-1, keepdims=True), jnp.max(s_ctx, axis=-1, keepdims=True))
            p_loc = jnp.exp2(s_loc - m)
            p_ctx = jnp.exp2(s_ctx - m)
            den = jnp.sum(p_loc, axis=-1, keepdims=True) + jnp.sum(p_ctx, axis=-1, keepdims=True)
            vw = vcat[pl.ds(k0, NA_WIN * GRID_W), ls]
            o = (jnp.dot(p_loc.astype(BF16), vw, preferred_element_type=F32)
                 + jnp.dot(p_ctx.astype(BF16), vx_ref[:, ls], preferred_element_type=F32))
            o = o / den
            outs.append(jnp.where(head0, o[:2 * GRID_W], o[2 * GRID_W:]))
        o_ref[pl.ds(q0, 2 * GRID_W), :] = jnp.concatenate(outs, axis=1).astype(o_ref.dtype)
        return carry

    lax.fori_loop(0, 4, pair_body, 0)


def _na_mixer(qr, kt, p_lat, p_ctx, bias):
    b, t, _ = qr.shape
    nc = p_ctx.shape[1]
    tq = 8 * GRID_W
    ni = t // tq
    assert ni >= 2
    blk = lambda off, col: pl.BlockSpec(
        (None, tq, BRANCH_W), lambda bb, i: (bb, jnp.clip(i + off, 0, ni - 1), col))
    ktb = lambda off: pl.BlockSpec(
        (None, tq // 128, BRANCH_W, 128), lambda bb, i: (bb, jnp.clip(i + off, 0, ni - 1), 0, 0))
    cv = COL_V // BRANCH_W
    return pl.pallas_call(
        functools.partial(_na_kernel, ni=ni),
        grid=(b, ni),
        in_specs=[
            blk(0, 0), ktb(-1), ktb(0), ktb(1),
            blk(-1, cv), blk(0, cv), blk(1, cv),
            pl.BlockSpec((None, nc, BRANCH_W), lambda bb, i: (bb, 0, COL_K // BRANCH_W)),
            pl.BlockSpec((None, nc, BRANCH_W), lambda bb, i: (bb, 0, cv)),
            pl.BlockSpec(bias.shape, lambda bb, i: (0, 0, 0, 0), pipeline_mode=pl.Buffered(1)),
        ],
        out_specs=pl.BlockSpec((None, tq, BRANCH_W), lambda bb, i: (bb, i, 0)),
        out_shape=jax.ShapeDtypeStruct((b, t, BRANCH_W), BF16),
        scratch_shapes=[pltpu.VMEM((2 * tq // 128, BRANCH_W, 128), BF16), pltpu.VMEM((2 * tq, BRANCH_W), BF16)],
        compiler_params=_cparams(("arbitrary", "arbitrary")),
        name="na_attention",
    )(qr, kt, kt, kt, p_lat, p_lat, p_lat, p_ctx, p_ctx, bias)


def _ctx_attn_kernel(q_ref, k_ref, v_ref, o_ref, *, scale):
    dn = (((1,), (1,)), ((), ()))
    outs = []
    for hh in range(2):
        sl = slice(hh * HEAD_DIM, (hh + 1) * HEAD_DIM)
        qh = (q_ref[:, sl].astype(F32) * scale).astype(BF16)
        s = lax.dot_general(qh, k_ref[:, sl], dn, preferred_element_type=F32)
        m = jnp.max(s, axis=-1, keepdims=True)
        p = jnp.exp(s - m)
        den = jnp.sum(p, axis=-1, keepdims=True)
        outs.append(jnp.dot(p.astype(BF16), v_ref[:, sl], preferred_element_type=F32) / den)
    o_ref[...] = jnp.concatenate(outs, axis=1).astype(o_ref.dtype)


def _ctx_attention(p_ctx):
    b, nc, _ = p_ctx.shape
    spec = lambda col: pl.BlockSpec((None, nc, 128), lambda bb, hp: (bb, 0, col // 128 + hp))
    return pl.pallas_call(
        functools.partial(_ctx_attn_kernel, scale=HEAD_DIM ** -0.5),
        grid=(b, NA_HEADS // 2),
        in_specs=[spec(COL_Q), spec(COL_K), spec(COL_V)],
        out_specs=pl.BlockSpec((None, nc, 128), lambda bb, hp: (bb, 0, hp)),
        out_shape=jax.ShapeDtypeStruct((b, nc, BRANCH_W), BF16),
        compiler_params=_cparams(("arbitrary", "arbitrary")),
        name="ctx_attention",
    )(p_ctx, p_ctx, p_ctx)


def _filter_features(t_len):
    pos = jnp.arange(t_len, dtype=F32)
    t = pos / max(t_len - 1, 1)
    w = 2.0 * math.pi * pos / t_len
    bands = jnp.linspace(1e-4, HY_BANDS - 1, HY_BANDS, dtype=F32)
    feats = jnp.concatenate([t[:, None], jnp.cos(w[:, None] * bands), -jnp.sin(w[:, None] * bands)], axis=-1)
    feats2 = jnp.concatenate([feats, feats[:1], feats[:0:-1]], axis=0)
    return jnp.pad(feats2, ((0, 0), (0, 128 - HY_EMB)))


def _filter_kernel(f_ref, w1_ref, b1_ref, fr_ref, w2_ref, b2_ref, w3_ref, dl_ref, k_ref, ss_ref):
    f = f_ref[...]
    fr = fr_ref[...]
    h = jnp.sin(fr * (jnp.dot(f, w1_ref[...], preferred_element_type=F32, precision=HIGHEST) + b1_ref[...]))
    h = jnp.sin(fr * (jnp.dot(h, w2_ref[...], preferred_element_type=F32, precision=HIGHEST) + b2_ref[...]))
    k = jnp.dot(h, w3_ref[...], preferred_element_type=F32, precision=HIGHEST)
    k = k * jnp.exp(-f[:, 0:1] * dl_ref[...])
    k_ref[...] = k

    @pl.when(pl.program_id(0) == 0)
    def _():
        ss_ref[...] = jnp.zeros_like(ss_ref)

    ss_ref[...] += jnp.sum(k * k, axis=0, keepdims=True)


def _hyena_filter_time(t_len, w1, b1, freq, w2, b2, w3):
    n = 2 * t_len
    tm = min(1024, t_len)
    feats = _filter_features(t_len)
    w1p = jnp.pad(w1, ((0, 128 - HY_EMB), (0, 0)))
    deltas = jnp.abs(jnp.linspace(HY_MIN_DECAY, HY_MAX_DECAY, BRANCH_W, dtype=F32))
    dl = jnp.concatenate([deltas, deltas]).reshape(1, 2 * BRANCH_W)
    half = t_len // tm
    nco = 2 * BRANCH_W
    return pl.pallas_call(
        _filter_kernel,
        grid=(n // tm,),
        in_specs=[
            pl.BlockSpec((tm, 128), lambda i: (i, 0)),
            pl.BlockSpec((128, HY_HID), lambda i: (0, 0)),
            pl.BlockSpec((1, HY_HID), lambda i: (0, 0)),
            pl.BlockSpec((1, HY_HID), lambda i: (0, 0)),
            pl.BlockSpec((HY_HID, HY_HID), lambda i: (0, 0)),
            pl.BlockSpec((1, HY_HID), lambda i: (0, 0)),
            pl.BlockSpec((HY_HID, nco), lambda i: (0, i // half)),
            pl.BlockSpec((1, nco), lambda i: (0, 0)),
        ],
        out_specs=[pl.BlockSpec((tm, nco), lambda i: (i, 0)), pl.BlockSpec((1, nco), lambda i: (0, 0))],
        out_shape=[jax.ShapeDtypeStruct((n, nco), F32), jax.ShapeDtypeStruct((1, nco), F32)],
        compiler_params=_cparams(("arbitrary",)),
        name="hyena_filter_ffn",
    )(feats, w1p, b1.reshape(1, -1), freq.reshape(1, -1), w2, b2.reshape(1, -1), w3, dl)


def _dft_consts(n1):
    n2 = FFT_N2
    n = n1 * n2
    nh = n1 // 2
    a1 = -2.0 * np.pi * np.outer(np.arange(n1), np.arange(n1)) / n1
    f1r, f1i = np.cos(a1), np.sin(a1)
    a2 = -2.0 * np.pi * np.outer(np.arange(n2), np.arange(n2)) / n2
    f2r, f2i = np.cos(a2), np.sin(a2)
    at = -2.0 * np.pi * np.outer(np.arange(n1), np.arange(n2)) / n
    bf = lambda a: jnp.asarray(a.astype(np.float32)).astype(BF16)
    return dict(
        s1_data=bf(np.block([[f1r[:, :nh], -f1i[:, :nh]], [f1i[:, :nh], f1r[:, :nh]]])),
        s1_real=bf(np.concatenate([f1r, f1i], axis=0)),
        s6=bf(np.block([[f1r[:nh], f1i[:nh]], [-f1i[:nh], f1r[:nh]]]) / n),
        fa=bf(np.concatenate([f2r, f2i], axis=1)), fb=bf(np.concatenate([-f2i, f2r], axis=1)),
        ia=bf(np.concatenate([f2r, -f2i], axis=1)), ib=bf(np.concatenate([f2i, f2r], axis=1)),
        tw_r=jnp.asarray(np.cos(at).astype(np.float32)), tw_i=jnp.asarray(np.sin(at).astype(np.float32)),
    )


def _lane_dft(ar_ref, ai_ref, fa_ref, fb_ref):
    cb, n1, n2 = ar_ref.shape
    ar = ar_ref[...].reshape(cb * n1, n2).astype(BF16)
    ai = ai_ref[...].reshape(cb * n1, n2).astype(BF16)
    x = (jnp.dot(ar, fa_ref[...], preferred_element_type=F32)
         + jnp.dot(ai, fb_ref[...], preferred_element_type=F32))
    return x[:, :n2], x[:, n2:]


def _kf_kernel(k_ref, ss_ref, s1_ref, twr_ref, twi_ref, fa_ref, fb_ref, o_ref, ar_ref, ai_ref, *, cb, n1):
    i = pl.program_id(0)
    row = lax.broadcasted_iota(jnp.int32, (n1, FFT_N2), 0)
    lane = lax.broadcasted_iota(jnp.int32, (n1, FFT_N2), 1)
    keep = jnp.logical_not((row == n1 // 2) & (lane == 0))
    twr, twi = twr_ref[...], twi_ref[...]
    for c in range(cb):
        scale = lax.rsqrt(jnp.full((n1, FFT_N2), ss_ref[i * cb + c], F32) + EPS)
        k = jnp.where(keep, k_ref[c] * scale, 0.0)
        a = jnp.dot(s1_ref[...], k.astype(BF16), preferred_element_type=F32)
        a_r, a_i = a[:n1], a[n1:]
        ar_ref[c] = a_r * twr - a_i * twi
        ai_ref[c] = a_r * twi + a_i * twr
    xr, xi = _lane_dft(ar_ref, ai_ref, fa_ref, fb_ref)
    o_ref[:, :n1, :] = xr.reshape(cb, n1, FFT_N2)
    o_ref[:, n1:, :] = xi.reshape(cb, n1, FFT_N2)


def _hyena_filter_spectrum(k_time, ss, dft):
    n, nco = k_time.shape
    n1 = n // FFT_N2
    cb = 8
    kt = k_time.T.reshape(nco, n1, FFT_N2)
    const = lambda shape: pl.BlockSpec(shape, lambda i: (0,) * len(shape))
    return pl.pallas_call(
        functools.partial(_kf_kernel, cb=cb, n1=n1),
        grid=(nco // cb,),
        in_specs=[
            pl.BlockSpec((cb, n1, FFT_N2), lambda i: (i, 0, 0)),
            pl.BlockSpec(memory_space=pltpu.SMEM),
            const((2 * n1, n1)), const((n1, FFT_N2)), const((n1, FFT_N2)),
            const((FFT_N2, 2 * FFT_N2)), const((FFT_N2, 2 * FFT_N2)),
        ],
        out_specs=pl.BlockSpec((cb, 2 * n1, FFT_N2), lambda i: (i, 0, 0)),
        out_shape=jax.ShapeDtypeStruct((nco, 2 * n1, FFT_N2), F32),
        scratch_shapes=[pltpu.VMEM((cb, n1, FFT_N2), F32), pltpu.VMEM((cb, n1, FFT_N2), F32)],
        compiler_params=_cparams(("arbitrary",)),
        name="hyena_filter_fft",
    )(kt, ss.reshape(nco), dft["s1_real"], dft["tw_r"], dft["tw_i"], dft["fa"], dft["fb"])


def _hyena_kernel(v_ref, x1_ref, x2_ref, kf1_ref, kf2_ref, cw_ref, cbias_ref, dd_ref,
                  s1_ref, s6_ref, twr_ref, twi_ref, fa_ref, fb_ref, ia_ref, ib_ref,
                  o_ref, vs, x1s, x2s, ar_ref, ai_ref, *, cb, n1, nch):
    i = pl.program_id(0)
    nh = n1 // 2
    rows = 2 * nh
    row = lax.broadcasted_iota(jnp.int32, (rows, FFT_N2), 0)
    lane = lax.broadcasted_iota(jnp.int32, (rows, FFT_N2), 1)
    first = lane == 0
    last = lane == FFT_N2 - 1
    seq_start = first & ((row % nh) == 0)
    seq_end = last & ((row % nh) == nh - 1)
    twr, twi = twr_ref[...], twi_ref[...]

    def short_conv(x, ch):
        prev = pltpu.roll(x, 1, 1)
        prev = jnp.where(first, pltpu.roll(prev, 1, 0), prev)
        prev = jnp.where(seq_start, 0.0, prev)
        nxt = pltpu.roll(x, FFT_N2 - 1, 1)
        nxt = jnp.where(last, pltpu.roll(nxt, rows - 1, 0), nxt)
        nxt = jnp.where(seq_end, 0.0, nxt)
        return cw_ref[ch] * prev + cw_ref[nch + ch] * x + cw_ref[2 * nch + ch] * nxt + cbias_ref[ch]

    def stage1(z, c):
        a = jnp.dot(s1_ref[...], z.astype(BF16), preferred_element_type=F32)
        a_r, a_i = a[:n1], a[n1:]
        ar_ref[c] = a_r * twr - a_i * twi
        ai_ref[c] = a_r * twi + a_i * twr

    def spectral(kf_ref):
        xr, xi = _lane_dft(ar_ref, ai_ref, fa_ref, fb_ref)
        kr = kf_ref[:, :n1, :].reshape(cb * n1, FFT_N2)
        ki = kf_ref[:, n1:, :].reshape(cb * n1, FFT_N2)
        yr = (xr * kr - xi * ki).astype(BF16)
        yi = (xr * ki + xi * kr).astype(BF16)
        bm = (jnp.dot(yr, ia_ref[...], preferred_element_type=F32)
              + jnp.dot(yi, ib_ref[...], preferred_element_type=F32))
        br = bm[:, :FFT_N2].reshape(cb, n1, FFT_N2)
        bi = bm[:, FFT_N2:].reshape(cb, n1, FFT_N2)
        ar_ref[...] = br * twr[None] + bi * twi[None]
        ai_ref[...] = bi * twr[None] - br * twi[None]

    def stage6(c):
        bcat = jnp.concatenate([ar_ref[c], ai_ref[c]], axis=0).astype(BF16)
        return jnp.dot(s6_ref[...], bcat, preferred_element_type=F32)

    nbr = BRANCH_W
    for c in range(cb):
        ch = i * cb + c
        v = short_conv(v_ref[c].astype(F32), ch)
        vs[c] = v
        x1s[c] = short_conv(x1_ref[c].astype(F32), nbr + ch)
        x2s[c] = short_conv(x2_ref[c].astype(F32), 2 * nbr + ch)
        stage1(v, c)
    spectral(kf1_ref)
    for c in range(cb):
        ch = i * cb + c
        z = x1s[c] * (stage6(c) + vs[c] * dd_ref[ch])
        vs[c] = z
        stage1(z, c)
    spectral(kf2_ref)
    for c in range(cb):
        ch = i * cb + c
        o_ref[c] = (x2s[c] * (stage6(c) + vs[c] * dd_ref[nbr + ch])).astype(o_ref.dtype)


def _hyena_lat(hy_cm, kf, conv_w, conv_b, hy_d, dft):
    nch, rows, _ = hy_cm.shape
    n1 = rows
    cb = 8
    nblk = BRANCH_W // cb
    const = lambda shape: pl.BlockSpec(shape, lambda i: (0,) * len(shape))
    smem = pl.BlockSpec(memory_space=pltpu.SMEM)
    data = lambda sec: pl.BlockSpec((cb, rows, FFT_N2), lambda i: (sec * nblk + i, 0, 0))
    kfs = lambda o: pl.BlockSpec((cb, 2 * n1, FFT_N2), lambda i: (o * nblk + i, 0, 0))
    return pl.pallas_call(
        functools.partial(_hyena_kernel, cb=cb, n1=n1, nch=nch),
        grid=(nblk,),
        in_specs=[
            data(0), data(1), data(2), kfs(0), kfs(1), smem, smem, smem,
            const((2 * n1, n1)), const((n1, 2 * n1)), const((n1, FFT_N2)), const((n1, FFT_N2)),
            const((FFT_N2, 2 * FFT_N2)), const((FFT_N2, 2 * FFT_N2)),
            const((FFT_N2, 2 * FFT_N2)), const((FFT_N2, 2 * FFT_N2)),
        ],
        out_specs=pl.BlockSpec((cb, rows, FFT_N2), lambda i: (i, 0, 0)),
        out_shape=jax.ShapeDtypeStruct((BRANCH_W, rows, FFT_N2), BF16),
        scratch_shapes=[pltpu.VMEM((cb, rows, FFT_N2), F32) for _ in range(3)]
        + [pltpu.VMEM((cb, n1, FFT_N2), F32) for _ in range(2)],
        compiler_params=_cparams(("arbitrary",)),
        name="hyena_fftconv",
    )(hy_cm, hy_cm, hy_cm, kf, kf, conv_w.reshape(-1), conv_b, hy_d.reshape(-1),
      dft["s1_data"], dft["s6"], dft["tw_r"], dft["tw_i"], dft["fa"], dft["fb"], dft["ia"], dft["ib"])


def _hyena_ctx_kernel(hy_ref, k_ref, ss_ref, cw_ref, cbias_ref, dd_ref, fwd_a_ref, fwd_b_ref,
                      inv_a_ref, inv_b_ref, o_ref, *, t_len):
    nbr = BRANCH_W
    lane = lax.broadcasted_iota(jnp.int32, (nbr, t_len), 1)

    def short_conv(x, sec):
        sl = slice(sec * nbr, (sec + 1) * nbr)
        prev = jnp.where(lane == 0, 0.0, pltpu.roll(x, 1, 1))
        nxt = jnp.where(lane == t_len - 1, 0.0, pltpu.roll(x, t_len - 1, 1))
        return cw_ref[0, sl, :] * prev + cw_ref[1, sl, :] * x + cw_ref[2, sl, :] * nxt + cbias_ref[sl, :]

    def sec(b, s):
        return short_conv(hy_ref[b, s * nbr:(s + 1) * nbr, :].astype(F32), s)

    klane = lax.broadcasted_iota(jnp.int32, k_ref.shape, 1)
    kk = jnp.where(klane == t_len, 0.0, k_ref[...] * lax.rsqrt(ss_ref[...] + EPS))
    kf = jnp.dot(kk.astype(BF16), fwd_a_ref[...], preferred_element_type=F32)
    n = 2 * t_len

    def conv(z0, z1, order):
        x = (jnp.dot(z0.astype(BF16), fwd_a_ref[:t_len, :], preferred_element_type=F32)
             + jnp.dot(z1.astype(BF16), fwd_b_ref[:t_len, :], preferred_element_type=F32))
        xr, xi = x[:, :n], x[:, n:]
        kr = kf[order * nbr:(order + 1) * nbr, :n]
        ki = kf[order * nbr:(order + 1) * nbr, n:]
        yr = (xr * kr - xi * ki).astype(BF16)
        yi = (xr * ki + xi * kr).astype(BF16)
        y = (jnp.dot(yr, inv_a_ref[...], preferred_element_type=F32)
             + jnp.dot(yi, inv_b_ref[...], preferred_element_type=F32))
        return y[:, :t_len], y[:, t_len:]

    v0, v1 = sec(0, 0), sec(1, 0)
    y0, y1 = conv(v0, v1, 0)
    d1 = dd_ref[:nbr, :]
    d2 = dd_ref[nbr:, :]
    z0 = sec(0, 1) * (y0 + v0 * d1)
    z1 = sec(1, 1) * (y1 + v1 * d1)
    y0, y1 = conv(z0, z1, 1)
    o_ref[0] = (sec(0, 2) * (y0 + z0 * d2)).astype(o_ref.dtype)
    o_ref[1] = (sec(1, 2) * (y1 + z1 * d2)).astype(o_ref.dtype)


def _hyena_ctx(hy_ctx, k_time, ss, conv_w, conv_b, hy_d):
    b, t_len, nch = hy_ctx.shape
    n = 2 * t_len
    ang = -2.0 * np.pi * np.outer(np.arange(n), np.arange(n)) / n
    fr, fi = np.cos(ang), np.sin(ang)
    bf = lambda a: jnp.asarray(a.astype(np.float32)).astype(BF16)
    fwd_a = bf(np.concatenate([fr, fi], axis=1))
    fwd_b = bf(np.concatenate([-fi, fr], axis=1))
    inv_a = bf(np.concatenate([fr[:, :t_len], -fi[:, :t_len]], axis=1) / n)
    inv_b = bf(np.concatenate([fi[:, :t_len], fr[:, :t_len]], axis=1) / n)
    out = pl.pallas_call(
        functools.partial(_hyena_ctx_kernel, t_len=t_len),
        out_shape=jax.ShapeDtypeStruct((b, BRANCH_W, t_len), BF16),
        compiler_params=pltpu.CompilerParams(vmem_limit_bytes=VMEM_LIMIT),
        name="hyena_ctx",
    )(hy_ctx.transpose(0, 2, 1), k_time.T, ss.reshape(-1, 1), conv_w.reshape(3, nch, 1),
      conv_b.reshape(nch, 1), hy_d.reshape(-1, 1), fwd_a, fwd_b, inv_a, inv_b)
    return out.transpose(0, 2, 1)


def _merge_kernel(x_ref, ya_ref, yb_ref, yc_ref, za_ref, zb_ref, zc_ref, g0_ref, g1_ref, g2_ref, mod_ref,
                  gw_ref, gb_ref, wb_ref, wo_ref, fg_ref, o_ref, *, mod_row, d, final):
    ya = jax.nn.gelu(ya_ref[...].astype(F32)).astype(BF16)
    glu = jnp.dot(ya, gw_ref[...], preferred_element_type=F32) + gb_ref[...]
    y_a = glu[:, :BRANCH_W] * jax.nn.sigmoid(glu[:, BRANCH_W:])

    def branch(i, y, z_ref, g_ref):
        u = (_silu(z_ref[...].astype(F32)) * y).astype(BF16)
        return jax.nn.sigmoid(g_ref[...].astype(F32)) * jnp.dot(u, wb_ref[i], preferred_element_type=F32)

    acc = branch(0, y_a, za_ref, g0_ref)
    acc = acc + branch(1, yb_ref[...].astype(F32), zb_ref, g1_ref)
    acc = acc + branch(2, yc_ref[...].astype(F32), zc_ref, g2_ref)
    out = jnp.dot(acc.astype(BF16), wo_ref[...], preferred_element_type=F32)
    if mod_row is None:
        m = mod_ref[pl.ds(pl.program_id(0), 1), :]
    else:
        m = mod_ref[mod_row:mod_row + 1, :]
    xn = x_ref[...] + m[:, 2 * d:] * out
    if final:
        ms = jnp.mean(xn * xn, axis=-1, keepdims=True)
        xn = xn * lax.rsqrt(ms + EPS) * fg_ref[...]
    o_ref[...] = xn


def _merge(x, ya, yb, yc, p, mod, glu_w, glu_b, w_branch, w_out, final_g, *, mod_row, tm, final):
    b, t, d = x.shape
    tok = lambda w, col: pl.BlockSpec((None, tm, w), lambda bb, i: (bb, i, col // w))
    const = lambda shape: pl.BlockSpec(shape, lambda bb, i: (0,) * len(shape))
    return pl.pallas_call(
        functools.partial(_merge_kernel, mod_row=mod_row, d=d, final=final),
        grid=(b, t // tm),
        in_specs=[
            tok(d, 0), tok(512, 0), tok(512, 0), tok(512, 0),
            tok(512, COL_ZA), tok(512, COL_ZB), tok(512, COL_ZC),
            tok(1024, COL_GT), tok(1024, COL_GT + 1024), tok(1024, COL_GT + 2048),
            const((8, 3 * d)), const((BRANCH_W, 2 * BRANCH_W)), const((1, 2 * BRANCH_W)),
            const((3, BRANCH_W, d)), const((d, d)), const((1, d)),
        ],
        out_specs=tok(d, 0),
        out_shape=jax.ShapeDtypeStruct((b, t, d), F32),
        compiler_params=_cparams(("arbitrary", "arbitrary")),
        name="merge_out",
    )(x, ya, yb, yc, p, p, p, p, p, p, mod, glu_w, glu_b.reshape(1, -1), w_branch, w_out, final_g.reshape(1, d))


def kernel(x, c, ctx, c_ctx, ada_w, ada_b, norm_g, w_in, s5_lam_re, s5_lam_im, s5_log_dt, s5_b_re, s5_b_im,
           s5_c_re, s5_c_im, s5_d, s5_glu_w, s5_glu_b, na_rpb, hy_conv_w, hy_conv_b, hf_w1, hf_b1, hf_freq,
           hf_w2, hf_b2, hf_w3, hy_d, w_branch, w_out, final_g):
    bsz, t_lat, d = x.shape
    t_ctx = ctx.shape[1]
    depth = ada_w.shape[0]
    rows_n = t_lat // GRID_W
    assert bsz == 2 and d == 1024 and t_lat % (8 * GRID_W) == 0 and t_ctx % 128 == 0

    cvec = jnp.zeros((8, d), F32).at[:bsz].set(c).at[bsz].set(c_ctx)
    mods = _modulation(cvec, ada_w, ada_b)
    cos_t, sin_t = _rope_tables(t_lat)
    n1 = 2 * t_lat // FFT_N2
    dft = _dft_consts(n1)
    tm_lat = 1024 if t_lat % 1024 == 0 else 512

    x_lat, x_ctx = x, ctx
    for l in range(depth):
        ctx_out = l < depth - 1
        w_bf = w_in[l].astype(BF16)
        p_lat = _inproj(x_lat, mods[l], norm_g[l], w_bf, mod_row=None, tm=tm_lat)
        p_ctx = _inproj(x_ctx, mods[l], norm_g[l], w_bf, mod_row=bsz, tm=t_ctx)

        ops = _s5_operators(s5_lam_re[l], s5_lam_im[l], s5_log_dt[l], s5_b_re[l], s5_b_im[l],
                            s5_c_re[l], s5_c_im[l], s5_d[l])
        ya, ya_c = _s5_mixer(p_lat, p_ctx, ops)

        qr, kt = _rope_qk(p_lat, cos_t, sin_t)
        bias = _na_bias_tables(na_rpb[l])
        yb = _na_mixer(qr, kt, p_lat, p_ctx, bias)

        k_time, ss = _hyena_filter_time(t_lat, hf_w1[l], hf_b1[l], hf_freq[l], hf_w2[l], hf_b2[l], hf_w3[l])
        kf = _hyena_filter_spectrum(k_time, ss, dft)
        hy_cm = p_lat[:, :, COL_HY:COL_HY + 1536].transpose(2, 0, 1).reshape(1536, bsz * t_lat // FFT_N2, FFT_N2)
        yc_cm = _hyena_lat(hy_cm, kf, hy_conv_w[l], hy_conv_b[l], hy_d[l], dft)
        yc = yc_cm.reshape(BRANCH_W, bsz, t_lat).transpose(1, 2, 0)

        wb_bf = w_branch[l].astype(BF16)
        wo_bf = w_out[l].astype(BF16)
        gw_bf = s5_glu_w[l].astype(BF16)
        x_lat_new = _merge(x_lat, ya, yb, yc, p_lat, mods[l], gw_bf, s5_glu_b[l], wb_bf, wo_bf,
                           final_g, mod_row=None, tm=512, final=not ctx_out)
        if ctx_out:
            yb_c = _ctx_attention(p_ctx)
            kc_time, ss_c = _hyena_filter_time(t_ctx, hf_w1[l], hf_b1[l], hf_freq[l], hf_w2[l], hf_b2[l], hf_w3[l])
            yc_c = _hyena_ctx(p_ctx[:, :, COL_HY:COL_HY + 1536], kc_time, ss_c, hy_conv_w[l], hy_conv_b[l], hy_d[l])
            x_ctx = _merge(x_ctx, ya_c, yb_c, yc_c, p_ctx, mods[l], gw_bf, s5_glu_b[l], wb_bf, wo_bf,
                           final_g, mod_row=bsz, tm=t_ctx, final=False)
        x_lat = x_lat_new
    return x_lat
```

```python
import functools
import math

import numpy as np
import jax
import jax.numpy as jnp
from jax import lax
from jax.experimental import pallas as pl
from jax.experimental.pallas import tpu as pltpu

F32 = jnp.float32
BF16 = jnp.bfloat16
HIGHEST = lax.Precision.HIGHEST

GRID_W = 64
BRANCH_W = 512
S5_P = 16
S5_N = 64
S5_G = BRANCH_W // S5_P
S5_TC = 16
HEAD_DIM = 64
NA_HEADS = BRANCH_W // HEAD_DIM
NA_KR = 8
NA_KW = 16
ROPE_HALF = 16
ROPE_THETA = 10000.0
HY_EMB = 33
HY_BANDS = (HY_EMB - 1) // 2
HY_HID = 64
HY_MIN_DECAY = math.log(1e-2) / 1.5
HY_MAX_DECAY = math.log(1e-2) / 0.3
EPS = 1e-6
FFT_N2 = 256
NEG_BIG = -1e30
LOG2E = 1.4426950408889634
VMEM_LIMIT = 52 * 1024 * 1024

COL_UA, COL_ZA, COL_Q, COL_K, COL_V, COL_ZB, COL_HY, COL_ZC, COL_GT = (
    0, 512, 1024, 1536, 2048, 2560, 3072, 4608, 5120)
IN_COLS = 8192


def _cparams(sem):
    return pltpu.CompilerParams(dimension_semantics=sem, vmem_limit_bytes=VMEM_LIMIT)


def _silu(x):
    return x * jax.nn.sigmoid(x)


def _mod_kernel(c_ref, w_ref, b_ref, o_ref):
    s = _silu(c_ref[...])
    o_ref[...] = jnp.dot(s, w_ref[...], preferred_element_type=F32, precision=HIGHEST) + b_ref[...]


def _modulation(cvec, ada_w, ada_b):
    depth, d, d3 = ada_w.shape
    tn = 1024
    return pl.pallas_call(
        _mod_kernel,
        grid=(depth, d3 // tn),
        in_specs=[
            pl.BlockSpec((8, d), lambda l, j: (0, 0)),
            pl.BlockSpec((None, d, tn), lambda l, j: (l, 0, j)),
            pl.BlockSpec((None, 1, tn), lambda l, j: (l, 0, j)),
        ],
        out_specs=pl.BlockSpec((None, 8, tn), lambda l, j: (l, 0, j)),
        out_shape=jax.ShapeDtypeStruct((depth, 8, d3), F32),
        compiler_params=_cparams(("arbitrary", "arbitrary")),
        name="adaln_mod",
    )(cvec, ada_w, ada_b.reshape(depth, 1, d3))


def _inproj_kernel(x_ref, mod_ref, g_ref, w_ref, o_ref, h_ref, *, mod_row, d):
    @pl.when(pl.program_id(2) == 0)
    def _():
        x = x_ref[...]
        ms = jnp.mean(x * x, axis=-1, keepdims=True)
        y = x * lax.rsqrt(ms + EPS) * g_ref[...]
        if mod_row is None:
            m = mod_ref[pl.ds(pl.program_id(0), 1), :]
        else:
            m = mod_ref[mod_row:mod_row + 1, :]
        h_ref[...] = (y * (1.0 + m[:, d:2 * d]) + m[:, :d]).astype(BF16)

    o_ref[...] = jnp.dot(h_ref[...], w_ref[...], preferred_element_type=F32).astype(o_ref.dtype)


def _inproj(x, mod, g, w_bf, *, mod_row, tm):
    b, t, d = x.shape
    n = w_bf.shape[1]
    tn = 2048
    return pl.pallas_call(
        functools.partial(_inproj_kernel, mod_row=mod_row, d=d),
        grid=(b, t // tm, n // tn),
        in_specs=[
            pl.BlockSpec((None, tm, d), lambda bb, i, j: (bb, i, 0)),
            pl.BlockSpec((8, 3 * d), lambda bb, i, j: (0, 0)),
            pl.BlockSpec((1, d), lambda bb, i, j: (0, 0)),
            pl.BlockSpec((d, tn), lambda bb, i, j: (0, j)),
        ],
        out_specs=pl.BlockSpec((None, tm, tn), lambda bb, i, j: (bb, i, j)),
        out_shape=jax.ShapeDtypeStruct((b, t, n), BF16),
        scratch_shapes=[pltpu.VMEM((tm, d), BF16)],
        compiler_params=_cparams(("arbitrary", "arbitrary", "arbitrary")),
        name="norm_inproj",
    )(x, mod, g.reshape(1, d), w_bf)


def _s5_operators(lam_re, lam_im, log_dt, b_re, b_im, c_re, c_im, d_skip):
    tc, p, n, g = S5_TC, S5_P, S5_N, S5_G
    dt = jnp.exp(log_dt)[..., None]
    er = lam_re * dt
    ei = lam_im * dt
    k = jnp.arange(tc + 1, dtype=F32)[:, None, None, None]
    mag = jnp.exp(k * er[None])
    pw_r = mag * jnp.cos(k * ei[None])
    pw_i = mag * jnp.sin(k * ei[None])
    lb_r, lb_i = pw_r[1], pw_i[1]
    den = lam_re * lam_re + lam_im * lam_im
    q_r = ((lb_r - 1.0) * lam_re + lb_i * lam_im) / den
    q_i = (lb_i * lam_re - (lb_r - 1.0) * lam_im) / den
    bb_r = q_r[..., None] * b_re - q_i[..., None] * b_im
    bb_i = q_r[..., None] * b_im + q_i[..., None] * b_re

    def cmul(ar, ai, br, bi):
        return ar * br - ai * bi, ar * bi + ai * br

    cl_r, cl_i = cmul(c_re[None], c_im[None], pw_r[:, :, :, None, :], pw_i[:, :, :, None, :])
    kern = (jnp.einsum("kdgpn,dgnq->kdgpq", cl_r[:tc], bb_r, precision=HIGHEST)
            - jnp.einsum("kdgpn,dgnq->kdgpq", cl_i[:tc], bb_i, precision=HIGHEST))
    t_in = np.arange(tc)[:, None]
    t_out = np.arange(tc)[None, :]
    lag_f = t_out - t_in
    lag_b = t_in - t_out
    kf = jnp.where((lag_f >= 0)[:, :, None, None, None], kern[np.clip(lag_f, 0, tc - 1), 0], 0.0)
    kb = jnp.where((lag_b >= 0)[:, :, None, None, None], kern[np.clip(lag_b, 0, tc - 1), 1], 0.0)
    eye_t = jnp.asarray(np.eye(tc, dtype=np.float32))
    eye_p = jnp.asarray(np.eye(p, dtype=np.float32))
    kd = eye_t[:, :, None, None, None] * (d_skip.reshape(g, p)[None, None, :, :, None] * eye_p[None, None, None])
    m_sum = (kf + kb + kd).transpose(2, 0, 4, 1, 3).reshape(g, tc * p, tc * p)

    def state_in(pr, pi, dirn):
        sr, si = cmul(pr[..., None], pi[..., None], bb_r[dirn][None], bb_i[dirn][None])
        sr = sr.transpose(1, 0, 3, 2).reshape(g, tc * p, n)
        si = si.transpose(1, 0, 3, 2).reshape(g, tc * p, n)
        return jnp.concatenate([sr, si], -1), jnp.concatenate([si, sr], -1)

    bs_f, bs_f_sw = state_in(pw_r[:tc, 0][::-1], pw_i[:tc, 0][::-1], 0)
    bs_b, bs_b_sw = state_in(pw_r[:tc, 1], pw_i[:tc, 1], 1)
    wcat = jnp.concatenate([m_sum, bs_f, bs_f_sw, bs_b, bs_b_sw], -1)

    def state_out(dirn, powers):
        cr = cl_r[powers, dirn]
        ci = cl_i[powers, dirn]
        top = cr.transpose(1, 3, 0, 2).reshape(g, n, tc * p)
        bot = (-ci).transpose(1, 3, 0, 2).reshape(g, n, tc * p)
        return jnp.concatenate([top, bot], 1)

    cs_f = state_out(0, np.arange(1, tc + 1))
    cs_b = state_out(1, np.arange(tc, 0, -1))
    ccat = jnp.concatenate([cs_f, cs_b], 1)
    ar, ai = pw_r[tc], pw_i[tc]
    a1 = jnp.concatenate([ar, ar], -1)
    a2 = jnp.concatenate([-ai, ai], -1)
    a3 = jnp.concatenate([ai, -ai], -1)
    return wcat.astype(BF16), ccat.astype(BF16), a1, a2, a3


def _s5_kernel(ul_ref, uc_ref, w_ref, c_ref, a1_ref, a2_ref, a3_ref, yl_ref, yc_ref,
               tok, xg, spf, sqf, spb, sqb, *, gb, nch, nch_lat, pitch):
    t_lat = nch_lat * S5_TC
    tok[:t_lat, :] = ul_ref[...].astype(F32)
    tok[t_lat:, :] = uc_ref[...].astype(F32)

    rt = next(r for r in (80, 40, 16, 8) if nch % r == 0)
    lane_rt = lax.broadcasted_iota(jnp.int32, (rt, 128), 1) // S5_P

    def block_transpose(a):
        for d in (4, 2, 1):
            keep = (lane_rt & d) == 0
            nxt = list(a)
            for i in range(8):
                if i & d == 0:
                    j = i + d
                    nxt[i] = jnp.where(keep, a[i], pltpu.roll(a[j], d * S5_P, 1))
                    nxt[j] = jnp.where(keep, pltpu.roll(a[i], 128 - d * S5_P, 1), a[j])
            a = nxt
        return a

    def to_compact(it, carry):
        r0 = pl.multiple_of(it * rt, rt)
        for h in range(2):
            z = [tok[pl.ds(r0 * S5_TC + 8 * h + t, rt, stride=S5_TC), :] for t in range(8)]
            b = block_transpose(z)
            for g in range(gb):
                xg[g, pl.ds(r0, rt), h * 128:(h + 1) * 128] = b[g]
        return carry

    lax.fori_loop(0, nch // rt, to_compact, 0)

    for g in range(gb):
        r = jnp.dot(xg[g].astype(BF16), w_ref[g], preferred_element_type=F32)
        xg[g] = r[:, :256]
        spf[g * pitch:g * pitch + nch, :] = r[:, 256:384]
        sqf[g * pitch:g * pitch + nch, :] = r[:, 384:512]
        spb[g * pitch:g * pitch + nch, :] = r[:, 512:640]
        sqb[g * pitch:g * pitch + nch, :] = r[:, 640:768]

    a1f, a2f, a3f = a1_ref[0], a2_ref[0], a3_ref[0]
    a1b, a2b, a3b = a1_ref[1], a2_ref[1], a3_ref[1]
    nch_ctx = nch - nch_lat

    def body(s, carry):
        pf, qf, pb, qb = carry
        cf = jnp.where(s < nch_ctx, s + nch_lat, s - nch_ctx)
        cb = nch - 1 - s
        idx_f = pl.ds(cf, gb, stride=pitch)
        idx_b = pl.ds(cb, gb, stride=pitch)
        sp = spf[idx_f, :]
        sq = sqf[idx_f, :]
        spf[idx_f, :] = pf
        pf, qf = pf * a1f + qf * a2f + sp, qf * a1f + pf * a3f + sq
        sp = spb[idx_b, :]
        sq = sqb[idx_b, :]
        spb[idx_b, :] = pb
        pb, qb = pb * a1b + qb * a2b + sp, qb * a1b + pb * a3b + sq
        return pf, qf, pb, qb

    z0 = jnp.zeros((gb, 128), F32)
    lax.fori_loop(0, nch, body, (z0, z0, z0, z0))

    for g in range(gb):
        hf = spf[g * pitch:g * pitch + nch, :].astype(BF16)
        hb = spb[g * pitch:g * pitch + nch, :].astype(BF16)
        cm = c_ref[g]
        y = (jnp.dot(hf, cm[:128, :], preferred_element_type=F32)
             + jnp.dot(hb, cm[128:, :], preferred_element_type=F32))
        xg[g] = xg[g] + y

    def to_tokens(it, carry):
        r0 = pl.multiple_of(it * rt, rt)
        for h in range(2):
            y = [xg[g, pl.ds(r0, rt), h * 128:(h + 1) * 128] for g in range(gb)]
            b = block_transpose(y)
            for t in range(8):
                tok[pl.ds(r0 * S5_TC + 8 * h + t, rt, stride=S5_TC), :] = b[t]
        return carry

    lax.fori_loop(0, nch // rt, to_tokens, 0)
    yl_ref[...] = tok[:t_lat, :].astype(yl_ref.dtype)
    yc_ref[...] = tok[t_lat:, :].astype(yc_ref.dtype)


def _s5_mixer(p_lat, p_ctx, ops):
    wcat, ccat, a1, a2, a3 = ops
    b, t_lat, _ = p_lat.shape
    t_ctx = p_ctx.shape[1]
    nch, nch_lat = (t_lat + t_ctx) // S5_TC, t_lat // S5_TC
    gb = 128 // S5_P
    pitch = ((nch + 7) // 8) * 8 + 8
    kern = functools.partial(_s5_kernel, gb=gb, nch=nch, nch_lat=nch_lat, pitch=pitch)
    once = pl.Buffered(1)
    return pl.pallas_call(
        kern,
        grid=(b, S5_G // gb),
        in_specs=[
            pl.BlockSpec((None, t_lat, 128), lambda bb, i: (bb, 0, COL_UA // 128 + i), pipeline_mode=once),
            pl.BlockSpec((None, t_ctx, 128), lambda bb, i: (bb, 0, COL_UA // 128 + i)),
            pl.BlockSpec((gb, 256, 768), lambda bb, i: (i, 0, 0)),
            pl.BlockSpec((gb, 256, 256), lambda bb, i: (i, 0, 0)),
            pl.BlockSpec((2, gb, 128), lambda bb, i: (0, i, 0)),
            pl.BlockSpec((2, gb, 128), lambda bb, i: (0, i, 0)),
            pl.BlockSpec((2, gb, 128), lambda bb, i: (0, i, 0)),
        ],
        out_specs=[
            pl.BlockSpec((None, t_lat, 128), lambda bb, i: (bb, 0, i), pipeline_mode=once),
            pl.BlockSpec((None, t_ctx, 128), lambda bb, i: (bb, 0, i)),
        ],
        out_shape=[jax.ShapeDtypeStruct((b, t_lat, BRANCH_W), BF16),
                   jax.ShapeDtypeStruct((b, t_ctx, BRANCH_W), BF16)],
        scratch_shapes=[pltpu.VMEM((t_lat + t_ctx, 128), F32), pltpu.VMEM((gb, nch, 256), F32)]
        + [pltpu.VMEM((gb * pitch, 128), F32) for _ in range(4)],
        compiler_params=_cparams(("arbitrary", "arbitrary")),
        name="s5_chunk_scan",
    )(p_lat, p_ctx, wcat, ccat, a1, a2, a3)


def _rope_tables(t_len):
    t = jnp.arange(t_len)
    rows = (t // GRID_W).astype(F32)
    cols = (t % GRID_W).astype(F32)
    inv = ROPE_THETA ** (-jnp.arange(ROPE_HALF, dtype=F32) / ROPE_HALF)
    lane = np.arange(128)
    dd = lane % HEAD_DIM
    fi = dd % ROPE_HALF
    use_row = jnp.asarray(dd < 32)
    sign = jnp.asarray(np.where((dd % 32) < ROPE_HALF, -1.0, 1.0).astype(np.float32))
    ang = jnp.where(use_row[None, :], rows[:, None], cols[:, None]) * inv[fi][None, :]
    return jnp.cos(ang), jnp.sin(ang) * sign[None, :]


def _rope_kernel(q_ref, k_ref, cos_ref, sin_ref, qo_ref, kt_ref, *, scale):
    cos = jnp.concatenate([cos_ref[...]] * 4, axis=1)
    sin = jnp.concatenate([sin_ref[...]] * 4, axis=1)
    lane = lax.broadcasted_iota(jnp.int32, cos.shape, 1)
    low = (lane % 32) < ROPE_HALF

    def rot(x):
        n = x.shape[1]
        partner = jnp.where(low, pltpu.roll(x, n - ROPE_HALF, 1), pltpu.roll(x, ROPE_HALF, 1))
        return x * cos + partner * sin

    qo_ref[...] = (rot(q_ref[...].astype(F32)) * scale).astype(BF16)
    kr = rot(k_ref[...].astype(F32))
    for c in range(kt_ref.shape[0]):
        kt_ref[c] = kr[c * 128:(c + 1) * 128, :].T.astype(BF16)


def _rope_qk(p_lat, cos_t, sin_t):
    b, t, _ = p_lat.shape
    tm = 1024
    return pl.pallas_call(
        functools.partial(_rope_kernel, scale=HEAD_DIM ** -0.5 * LOG2E),
        grid=(b, t // tm),
        in_specs=[
            pl.BlockSpec((None, tm, 512), lambda bb, i: (bb, i, COL_Q // 512)),
            pl.BlockSpec((None, tm, 512), lambda bb, i: (bb, i, COL_K // 512)),
            pl.BlockSpec((tm, 128), lambda bb, i: (i, 0)),
            pl.BlockSpec((tm, 128), lambda bb, i: (i, 0)),
        ],
        out_specs=[pl.BlockSpec((None, tm, 512), lambda bb, i: (bb, i, 0)),
                   pl.BlockSpec((None, tm // 128, 512, 128), lambda bb, i: (bb, i, 0, 0))],
        out_shape=[jax.ShapeDtypeStruct((b, t, BRANCH_W), BF16),
                   jax.ShapeDtypeStruct((b, t // 128, BRANCH_W, 128), BF16)],
        compiler_params=_cparams(("arbitrary", "arbitrary")),
        name="rope_qk",
    )(p_lat, p_lat, cos_t, sin_t)


_NA_VARIANTS = (((4, 5), (0, 1)), ((0, 1), (0, 0)), ((2, 3), (0, 0)), ((4, 5), (0, 0)), ((6, 7), (0, 0)))
NA_WIN = NA_KR + 2


def _na_bias_tables(rpb):
    h = rpb.shape[0]
    qc = np.arange(GRID_W)
    kc = np.arange(GRID_W)
    cs = np.clip(qc - NA_KW // 2, 0, GRID_W - NA_KW)
    valid_c = (kc[None, :] >= cs[:, None]) & (kc[None, :] < cs[:, None] + NA_KW)
    rel_c = np.clip(kc[None, :] - qc[:, None] + NA_KW - 1, 0, 2 * NA_KW - 2)
    oh_c = np.eye(2 * NA_KW - 1, dtype=np.float32)[rel_c]
    j = np.arange(NA_WIN)
    oh_r = np.zeros((len(_NA_VARIANTS), 2, NA_WIN, 2 * NA_KR - 1), np.float32)
    valid_r = np.zeros((len(_NA_VARIANTS), 2, NA_WIN), bool)
    for n, (offs, los) in enumerate(_NA_VARIANTS):
        for rr in range(2):
            valid_r[n, rr] = (j >= los[rr]) & (j < los[rr] + NA_KR)
            rel_r = np.clip(j - offs[rr] + NA_KR - 1, 0, 2 * NA_KR - 2)
            oh_r[n, rr] = np.eye(2 * NA_KR - 1, dtype=np.float32)[rel_r]
    bias = jnp.einsum("nrja,hab,ckb->nhrcjk", jnp.asarray(oh_r), rpb, jnp.asarray(oh_c), precision=HIGHEST) * LOG2E
    valid = valid_r[:, None, :, None, :, None] & valid_c[None, None, None, :, None, :]
    bias = jnp.where(jnp.asarray(valid), bias, NEG_BIG)
    return bias.reshape(len(_NA_VARIANTS), h // 2, 4 * GRID_W, NA_WIN * GRID_W)


def _na_kernel(q_ref, kp_ref, kc_ref, kn_ref, vp_ref, vc_ref, vn_ref, kx_ref, vx_ref, bias_ref, o_ref,
               kt3, vcat, *, ni):
    half = 4 * GRID_W
    tq = 8 * GRID_W
    kt3[0:2] = kp_ref[2:4]
    kt3[2:6] = kc_ref[...]
    kt3[6:8] = kn_ref[0:2]
    vcat[:half, :] = vp_ref[half:, :]
    vcat[half:half + tq, :] = vc_ref[...]
    vcat[half + tq:, :] = vn_ref[:half, :]
    kxt = kx_ref[...].astype(F32).T.astype(BF16)
    i = pl.program_id(1)
    head0 = lax.broadcasted_iota(jnp.int32, (2 * GRID_W, 128), 1) < HEAD_DIM
    nchunk = NA_WIN // 2

    def pair_body(a, carry):
        first = i == 0
        last = i == ni - 1
        s0 = jnp.where(first, jnp.maximum(2 * a, 4), jnp.where(last, jnp.minimum(2 * a, 4), 2 * a))
        var = jnp.where(first & (a < 2), 1 + a, jnp.where(last & (a >= 2), 1 + a, 0))
        q0 = pl.multiple_of(a * 2 * GRID_W, 2 * GRID_W)
        k0 = pl.multiple_of(s0 * GRID_W, 2 * GRID_W)
        sp = s0 // 2
        outs = []
        for hp in range(NA_HEADS // 2):
            ls = slice(hp * 128, (hp + 1) * 128)
            qp = q_ref[pl.ds(q0, 2 * GRID_W), ls]
            zero = jnp.zeros_like(qp)
            q2 = jnp.concatenate([jnp.where(head0, qp, zero), jnp.where(head0, zero, qp)], axis=0)
            kw = kt3[pl.ds(sp, nchunk), ls, :]
            s_loc = jnp.concatenate(
                [jnp.dot(q2, kw[c], preferred_element_type=F32) for c in range(nchunk)], axis=1) + bias_ref[var, hp]
            s_ctx = jnp.dot(q2, kxt[ls, :], preferred_element_type=F32)
            m = jnp.maximum(jnp.max(s_loc, axis=-1, keepdims=True), jnp.max(s_ctx, axis=-1, keepdims=True))
            p_loc = jnp.exp2(s_loc - m)
            p_ctx = jnp.exp2(s_ctx - m)
            den = jnp.sum(p_loc, axis=-1, keepdims=True) + jnp.sum(p_ctx, axis=-1, keepdims=True)
            vw = vcat[pl.ds(k0, NA_WIN * GRID_W), ls]
            o = (jnp.dot(p_loc.astype(BF16), vw, preferred_element_type=F32)
                 + jnp.dot(p_ctx.astype(BF16), vx_ref[:, ls], preferred_element_type=F32))
            o = o / den
            outs.append(jnp.where(head0, o[:2 * GRID_W], o[2 * GRID_W:]))
        o_ref[pl.ds(q0, 2 * GRID_W), :] = jnp.concatenate(outs, axis=1).astype(o_ref.dtype)
        return carry

    lax.fori_loop(0, 4, pair_body, 0)


def _na_mixer(qr, kt, p_lat, p_ctx, bias):
    b, t, _ = qr.shape
    nc = p_ctx.shape[1]
    tq = 8 * GRID_W
    ni = t // tq
    assert ni >= 2
    blk = lambda off, col: pl.BlockSpec(
        (None, tq, BRANCH_W), lambda bb, i: (bb, jnp.clip(i + off, 0, ni - 1), col))
    ktb = lambda off: pl.BlockSpec(
        (None, tq // 128, BRANCH_W, 128), lambda bb, i: (bb, jnp.clip(i + off, 0, ni - 1), 0, 0))
    cv = COL_V // BRANCH_W
    return pl.pallas_call(
        functools.partial(_na_kernel, ni=ni),
        grid=(b, ni),
        in_specs=[
            blk(0, 0), ktb(-1), ktb(0), ktb(1),
            blk(-1, cv), blk(0, cv), blk(1, cv),
            pl.BlockSpec((None, nc, BRANCH_W), lambda bb, i: (bb, 0, COL_K // BRANCH_W)),
            pl.BlockSpec((None, nc, BRANCH_W), lambda bb, i: (bb, 0, cv)),
            pl.BlockSpec(bias.shape, lambda bb, i: (0, 0, 0, 0), pipeline_mode=pl.Buffered(1)),
        ],
        out_specs=pl.BlockSpec((None, tq, BRANCH_W), lambda bb, i: (bb, i, 0)),
        out_shape=jax.ShapeDtypeStruct((b, t, BRANCH_W), BF16),
        scratch_shapes=[pltpu.VMEM((2 * tq // 128, BRANCH_W, 128), BF16), pltpu.VMEM((2 * tq, BRANCH_W), BF16)],
        compiler_params=_cparams(("arbitrary", "arbitrary")),
        name="na_attention",
    )(qr, kt, kt, kt, p_lat, p_lat, p_lat, p_ctx, p_ctx, bias)


def _ctx_attn_kernel(q_ref, k_ref, v_ref, o_ref, *, scale):
    dn = (((1,), (1,)), ((), ()))
    outs = []
    for hh in range(2):
        sl = slice(hh * HEAD_DIM, (hh + 1) * HEAD_DIM)
        qh = (q_ref[:, sl].astype(F32) * scale).astype(BF16)
        s = lax.dot_general(qh, k_ref[:, sl], dn, preferred_element_type=F32)
        m = jnp.max(s, axis=-1, keepdims=True)
        p = jnp.exp(s - m)
        den = jnp.sum(p, axis=-1, keepdims=True)
        outs.append(jnp.dot(p.astype(BF16), v_ref[:, sl], preferred_element_type=F32) / den)
    o_ref[...] = jnp.concatenate(outs, axis=1).astype(o_ref.dtype)


def _ctx_attention(p_ctx):
    b, nc, _ = p_ctx.shape
    spec = lambda col: pl.BlockSpec((None, nc, 128), lambda bb, hp: (bb, 0, col // 128 + hp))
    return pl.pallas_call(
        functools.partial(_ctx_attn_kernel, scale=HEAD_DIM ** -0.5),
        grid=(b, NA_HEADS // 2),
        in_specs=[spec(COL_Q), spec(COL_K), spec(COL_V)],
        out_specs=pl.BlockSpec((None, nc, 128), lambda bb, hp: (bb, 0, hp)),
        out_shape=jax.ShapeDtypeStruct((b, nc, BRANCH_W), BF16),
        compiler_params=_cparams(("arbitrary", "arbitrary")),
        name="ctx_attention",
    )(p_ctx, p_ctx, p_ctx)


def _filter_features(t_len):
    pos = jnp.arange(t_len, dtype=F32)
    t = pos / max(t_len - 1, 1)
    w = 2.0 * math.pi * pos / t_len
    bands = jnp.linspace(1e-4, HY_BANDS - 1, HY_BANDS, dtype=F32)
    feats = jnp.concatenate([t[:, None], jnp.cos(w[:, None] * bands), -jnp.sin(w[:, None] * bands)], axis=-1)
    feats2 = jnp.concatenate([feats, feats[:1], feats[:0:-1]], axis=0)
    return jnp.pad(feats2, ((0, 0), (0, 128 - HY_EMB)))


def _filter_kernel(f_ref, w1_ref, b1_ref, fr_ref, w2_ref, b2_ref, w3_ref, dl_ref, k_ref, ss_ref, *, channel_major):
    f = f_ref[...]
    fr = fr_ref[...]
    h = jnp.sin(fr * (jnp.dot(f, w1_ref[...], preferred_element_type=F32, precision=HIGHEST) + b1_ref[...]))
    h = jnp.sin(fr * (jnp.dot(h, w2_ref[...], preferred_element_type=F32, precision=HIGHEST) + b2_ref[...]))
    k = jnp.dot(h, w3_ref[...], preferred_element_type=F32, precision=HIGHEST)
    k = k * jnp.exp(-f[:, 0:1] * dl_ref[...])
    if channel_major:
        rows = k.shape[0] // FFT_N2
        for cb in range(k.shape[1] // 128):
            piece = k[:, cb * 128:(cb + 1) * 128].reshape(rows, FFT_N2, 128)
            k_ref[cb * 128:(cb + 1) * 128] = jnp.swapaxes(jnp.swapaxes(piece, 1, 2), 0, 1)
    else:
        k_ref[...] = k

    @pl.when(pl.program_id(0) == 0)
    def _():
        ss_ref[...] = jnp.zeros_like(ss_ref)

    ss_ref[...] += jnp.sum(k * k, axis=0, keepdims=True)


def _hyena_filter_time(t_len, w1, b1, freq, w2, b2, w3, *, channel_major):
    n = 2 * t_len
    tm = 2048 if channel_major else min(1024, t_len)
    assert t_len % tm == 0
    feats = _filter_features(t_len)
    w1p = jnp.pad(w1, ((0, 128 - HY_EMB), (0, 0)))
    deltas = jnp.abs(jnp.linspace(HY_MIN_DECAY, HY_MAX_DECAY, BRANCH_W, dtype=F32))
    dl = jnp.concatenate([deltas, deltas]).reshape(1, 2 * BRANCH_W)
    half = t_len // tm
    nco = 2 * BRANCH_W
    if channel_major:
        k_spec = pl.BlockSpec((nco, tm // FFT_N2, FFT_N2), lambda i: (0, i, 0))
        k_shape = jax.ShapeDtypeStruct((nco, n // FFT_N2, FFT_N2), F32)
    else:
        k_spec = pl.BlockSpec((tm, nco), lambda i: (i, 0))
        k_shape = jax.ShapeDtypeStruct((n, nco), F32)
    return pl.pallas_call(
        functools.partial(_filter_kernel, channel_major=channel_major),
        grid=(n // tm,),
        in_specs=[
            pl.BlockSpec((tm, 128), lambda i: (i, 0)),
            pl.BlockSpec((128, HY_HID), lambda i: (0, 0)),
            pl.BlockSpec((1, HY_HID), lambda i: (0, 0)),
            pl.BlockSpec((1, HY_HID), lambda i: (0, 0)),
            pl.BlockSpec((HY_HID, HY_HID), lambda i: (0, 0)),
            pl.BlockSpec((1, HY_HID), lambda i: (0, 0)),
            pl.BlockSpec((HY_HID, nco), lambda i: (0, i // half)),
            pl.BlockSpec((1, nco), lambda i: (0, 0)),
        ],
        out_specs=[k_spec, pl.BlockSpec((1, nco), lambda i: (0, 0))],
        out_shape=[k_shape, jax.ShapeDtypeStruct((1, nco), F32)],
        compiler_params=_cparams(("arbitrary",)),
        name="hyena_filter_ffn",
    )(feats, w1p, b1.reshape(1, -1), freq.reshape(1, -1), w2, b2.reshape(1, -1), w3, dl)


def _dft_consts(n1):
    n2 = FFT_N2
    n = n1 * n2
    nh = n1 // 2
    a1 = -2.0 * np.pi * np.outer(np.arange(n1), np.arange(n1)) / n1
    f1r, f1i = np.cos(a1), np.sin(a1)
    a2 = -2.0 * np.pi * np.outer(np.arange(n2), np.arange(n2)) / n2
    f2r, f2i = np.cos(a2), np.sin(a2)
    at = -2.0 * np.pi * np.outer(np.arange(n1), np.arange(n2)) / n
    bf = lambda a: jnp.asarray(a.astype(np.float32)).astype(BF16)
    return dict(
        s1_data=bf(np.block([[f1r[:, :nh], -f1i[:, :nh]], [f1i[:, :nh], f1r[:, :nh]]])),
        s1_real=bf(np.concatenate([f1r, f1i], axis=0)),
        s6=bf(np.block([[f1r[:nh], f1i[:nh]], [-f1i[:nh], f1r[:nh]]]) / n),
        fa=bf(np.concatenate([f2r, f2i], axis=1)), fb=bf(np.concatenate([-f2i, f2r], axis=1)),
        ia=bf(np.concatenate([f2r, -f2i], axis=1)), ib=bf(np.concatenate([f2i, f2r], axis=1)),
        tw_r=jnp.asarray(np.cos(at).astype(np.float32)), tw_i=jnp.asarray(np.sin(at).astype(np.float32)),
    )


def _lane_dft(ar_ref, ai_ref, fa_ref, fb_ref):
    cb, n1, n2 = ar_ref.shape
    ar = ar_ref[...].reshape(cb * n1, n2).astype(BF16)
    ai = ai_ref[...].reshape(cb * n1, n2).astype(BF16)
    x = (jnp.dot(ar, fa_ref[...], preferred_element_type=F32)
         + jnp.dot(ai, fb_ref[...], preferred_element_type=F32))
    return x[:, :n2], x[:, n2:]


def _kf_kernel(k_ref, ss_ref, s1_ref, twr_ref, twi_ref, fa_ref, fb_ref, o_ref, ar_ref, ai_ref, *, cb, n1):
    i = pl.program_id(0)
    row = lax.broadcasted_iota(jnp.int32, (n1, FFT_N2), 0)
    lane = lax.broadcasted_iota(jnp.int32, (n1, FFT_N2), 1)
    keep = jnp.logical_not((row == n1 // 2) & (lane == 0))
    twr, twi = twr_ref[...], twi_ref[...]
    for c in range(cb):
        scale = lax.rsqrt(jnp.full((n1, FFT_N2), ss_ref[i * cb + c], F32) + EPS)
        k = jnp.where(keep, k_ref[c] * scale, 0.0)
        a = jnp.dot(s1_ref[...], k.astype(BF16), preferred_element_type=F32)
        a_r, a_i = a[:n1], a[n1:]
        ar_ref[c] = a_r * twr - a_i * twi
        ai_ref[c] = a_r * twi + a_i * twr
    xr, xi = _lane_dft(ar_ref, ai_ref, fa_ref, fb_ref)
    o_ref[:, :n1, :] = xr.reshape(cb, n1, FFT_N2)
    o_ref[:, n1:, :] = xi.reshape(cb, n1, FFT_N2)


def _hyena_filter_spectrum(kt, ss, dft):
    nco, n1, _ = kt.shape
    cb = 8
    const = lambda shape: pl.BlockSpec(shape, lambda i: (0,) * len(shape))
    return pl.pallas_call(
        functools.partial(_kf_kernel, cb=cb, n1=n1),
        grid=(nco // cb,),
        in_specs=[
            pl.BlockSpec((cb, n1, FFT_N2), lambda i: (i, 0, 0)),
            pl.BlockSpec(memory_space=pltpu.SMEM),
            const((2 * n1, n1)), const((n1, FFT_N2)), const((n1, FFT_N2)),
            const((FFT_N2, 2 * FFT_N2)), const((FFT_N2, 2 * FFT_N2)),
        ],
        out_specs=pl.BlockSpec((cb, 2 * n1, FFT_N2), lambda i: (i, 0, 0)),
        out_shape=jax.ShapeDtypeStruct((nco, 2 * n1, FFT_N2), F32),
        scratch_shapes=[pltpu.VMEM((cb, n1, FFT_N2), F32), pltpu.VMEM((cb, n1, FFT_N2), F32)],
        compiler_params=_cparams(("arbitrary",)),
        name="hyena_filter_fft",
    )(kt, ss.reshape(nco), dft["s1_real"], dft["tw_r"], dft["tw_i"], dft["fa"], dft["fb"])


def _tok2cm_kernel(x_ref, o_ref):
    rows = x_ref.shape[0] // FFT_N2
    x3 = x_ref[...].reshape(rows, FFT_N2, 128)
    o_ref[...] = jnp.swapaxes(jnp.swapaxes(x3, 1, 2), 0, 1)


def _to_channel_major(p, col0, ncols):
    b, t, _ = p.shape
    tt = 16 * FFT_N2
    nt = t // tt
    return pl.pallas_call(
        _tok2cm_kernel,
        grid=(ncols // 128, b, nt),
        in_specs=[pl.BlockSpec((None, tt, 128), lambda c, bb, i: (bb, i, col0 // 128 + c))],
        out_specs=pl.BlockSpec((128, tt // FFT_N2, FFT_N2), lambda c, bb, i: (c, bb * nt + i, 0)),
        out_shape=jax.ShapeDtypeStruct((ncols, b * t // FFT_N2, FFT_N2), p.dtype),
        compiler_params=_cparams(("arbitrary", "arbitrary", "arbitrary")),
        name="to_channel_major",
    )(p)


def _cm2tok_kernel(x_ref, o_ref):
    x3 = jnp.swapaxes(jnp.swapaxes(x_ref[...], 0, 1), 1, 2)
    o_ref[...] = x3.reshape(o_ref.shape)


def _to_token_major(y_cm, b):
    c, rows, _ = y_cm.shape
    t = rows * FFT_N2 // b
    tt = 16 * FFT_N2
    nt = t // tt
    return pl.pallas_call(
        _cm2tok_kernel,
        grid=(c // 128, b, nt),
        in_specs=[pl.BlockSpec((128, tt // FFT_N2, FFT_N2), lambda cc, bb, i: (cc, bb * nt + i, 0))],
        out_specs=pl.BlockSpec((None, tt, 128), lambda cc, bb, i: (bb, i, cc)),
        out_shape=jax.ShapeDtypeStruct((b, t, c), y_cm.dtype),
        compiler_params=_cparams(("arbitrary", "arbitrary", "arbitrary")),
        name="to_token_major",
    )(y_cm)


def _hyena_kernel(v_ref, x1_ref, x2_ref, kf1_ref, kf2_ref, cw_ref, cbias_ref, dd_ref,
                  s1_ref, s6_ref, twr_ref, twi_ref, fa_ref, fb_ref, ia_ref, ib_ref,
                  o_ref, vs, x1s, x2s, ar_ref, ai_ref, *, cb, n1, nch):
    i = pl.program_id(0)
    nh = n1 // 2
    rows = 2 * nh
    row = lax.broadcasted_iota(jnp.int32, (rows, FFT_N2), 0)
    lane = lax.broadcasted_iota(jnp.int32, (rows, FFT_N2), 1)
    first = lane == 0
    last = lane == FFT_N2 - 1
    seq_start = first & ((row % nh) == 0)
    seq_end = last & ((row % nh) == nh - 1)
    twr, twi = twr_ref[...], twi_ref[...]

    def short_conv(x, ch):
        prev = pltpu.roll(x, 1, 1)
        prev = jnp.where(first, pltpu.roll(prev, 1, 0), prev)
        prev = jnp.where(seq_start, 0.0, prev)
        nxt = pltpu.roll(x, FFT_N2 - 1, 1)
        nxt = jnp.where(last, pltpu.roll(nxt, rows - 1, 0), nxt)
        nxt = jnp.where(seq_end, 0.0, nxt)
        return cw_ref[ch] * prev + cw_ref[nch + ch] * x + cw_ref[2 * nch + ch] * nxt + cbias_ref[ch]

    def stage1(z, c):
        a = jnp.dot(s1_ref[...], z.astype(BF16), preferred_element_type=F32)
        a_r, a_i = a[:n1], a[n1:]
        ar_ref[c] = a_r * twr - a_i * twi
        ai_ref[c] = a_r * twi + a_i * twr

    def spectral(kf_ref):
        xr, xi = _lane_dft(ar_ref, ai_ref, fa_ref, fb_ref)
        kr = kf_ref[:, :n1, :].reshape(cb * n1, FFT_N2)
        ki = kf_ref[:, n1:, :].reshape(cb * n1, FFT_N2)
        yr = (xr * kr - xi * ki).astype(BF16)
        yi = (xr * ki + xi * kr).astype(BF16)
        bm = (jnp.dot(yr, ia_ref[...], preferred_element_type=F32)
              + jnp.dot(yi, ib_ref[...], preferred_element_type=F32))
        br = bm[:, :FFT_N2].reshape(cb, n1, FFT_N2)
        bi = bm[:, FFT_N2:].reshape(cb, n1, FFT_N2)
        ar_ref[...] = br * twr[None] + bi * twi[None]
        ai_ref[...] = bi * twr[None] - br * twi[None]

    def stage6(c):
        bcat = jnp.concatenate([ar_ref[c], ai_ref[c]], axis=0).astype(BF16)
        return jnp.dot(s6_ref[...], bcat, preferred_element_type=F32)

    nbr = BRANCH_W
    for c in range(cb):
        ch = i * cb + c
        v = short_conv(v_ref[c].astype(F32), ch)
        vs[c] = v
        x1s[c] = short_conv(x1_ref[c].astype(F32), nbr + ch)
        x2s[c] = short_conv(x2_ref[c].astype(F32), 2 * nbr + ch)
        stage1(v, c)
    spectral(kf1_ref)
    for c in range(cb):
        ch = i * cb + c
        z = x1s[c] * (stage6(c) + vs[c] * dd_ref[ch])
        vs[c] = z
        stage1(z, c)
    spectral(kf2_ref)
    for c in range(cb):
        ch = i * cb + c
        o_ref[c] = (x2s[c] * (stage6(c) + vs[c] * dd_ref[nbr + ch])).astype(o_ref.dtype)


def _hyena_lat(hy_cm, kf, conv_w, conv_b, hy_d, dft):
    nch, rows, _ = hy_cm.shape
    n1 = rows
    cb = 8
    nblk = BRANCH_W // cb
    const = lambda shape: pl.BlockSpec(shape, lambda i: (0,) * len(shape))
    smem = pl.BlockSpec(memory_space=pltpu.SMEM)
    data = lambda sec: pl.BlockSpec((cb, rows, FFT_N2), lambda i: (sec * nblk + i, 0, 0))
    kfs = lambda o: pl.BlockSpec((cb, 2 * n1, FFT_N2), lambda i: (o * nblk + i, 0, 0))
    return pl.pallas_call(
        functools.partial(_hyena_kernel, cb=cb, n1=n1, nch=nch),
        grid=(nblk,),
        in_specs=[
            data(0), data(1), data(2), kfs(0), kfs(1), smem, smem, smem,
            const((2 * n1, n1)), const((n1, 2 * n1)), const((n1, FFT_N2)), const((n1, FFT_N2)),
            const((FFT_N2, 2 * FFT_N2)), const((FFT_N2, 2 * FFT_N2)),
            const((FFT_N2, 2 * FFT_N2)), const((FFT_N2, 2 * FFT_N2)),
        ],
        out_specs=pl.BlockSpec((cb, rows, FFT_N2), lambda i: (i, 0, 0)),
        out_shape=jax.ShapeDtypeStruct((BRANCH_W, rows, FFT_N2), BF16),
        scratch_shapes=[pltpu.VMEM((cb, rows, FFT_N2), F32) for _ in range(3)]
        + [pltpu.VMEM((cb, n1, FFT_N2), F32) for _ in range(2)],
        compiler_params=_cparams(("arbitrary",)),
        name="hyena_fftconv",
    )(hy_cm, hy_cm, hy_cm, kf, kf, conv_w.reshape(-1), conv_b, hy_d.reshape(-1),
      dft["s1_data"], dft["s6"], dft["tw_r"], dft["tw_i"], dft["fa"], dft["fb"], dft["ia"], dft["ib"])


def _hyena_ctx_kernel(hy_ref, k_ref, ss_ref, cw_ref, cbias_ref, dd_ref, fwd_a_ref, fwd_b_ref,
                      inv_a_ref, inv_b_ref, o_ref, *, t_len):
    nbr = BRANCH_W
    lane = lax.broadcasted_iota(jnp.int32, (nbr, t_len), 1)

    def short_conv(x, sec):
        sl = slice(sec * nbr, (sec + 1) * nbr)
        prev = jnp.where(lane == 0, 0.0, pltpu.roll(x, 1, 1))
        nxt = jnp.where(lane == t_len - 1, 0.0, pltpu.roll(x, t_len - 1, 1))
        return cw_ref[0, sl, :] * prev + cw_ref[1, sl, :] * x + cw_ref[2, sl, :] * nxt + cbias_ref[sl, :]

    def sec(b, s):
        return short_conv(hy_ref[b, s * nbr:(s + 1) * nbr, :].astype(F32), s)

    klane = lax.broadcasted_iota(jnp.int32, k_ref.shape, 1)
    kk = jnp.where(klane == t_len, 0.0, k_ref[...] * lax.rsqrt(ss_ref[...] + EPS))
    kf = jnp.dot(kk.astype(BF16), fwd_a_ref[...], preferred_element_type=F32)
    n = 2 * t_len

    def conv(z0, z1, order):
        x = (jnp.dot(z0.astype(BF16), fwd_a_ref[:t_len, :], preferred_element_type=F32)
             + jnp.dot(z1.astype(BF16), fwd_b_ref[:t_len, :], preferred_element_type=F32))
        xr, xi = x[:, :n], x[:, n:]
        kr = kf[order * nbr:(order + 1) * nbr, :n]
        ki = kf[order * nbr:(order + 1) * nbr, n:]
        yr = (xr * kr - xi * ki).astype(BF16)
        yi = (xr * ki + xi * kr).astype(BF16)
        y = (jnp.dot(yr, inv_a_ref[...], preferred_element_type=F32)
             + jnp.dot(yi, inv_b_ref[...], preferred_element_type=F32))
        return y[:, :t_len], y[:, t_len:]

    v0, v1 = sec(0, 0), sec(1, 0)
    y0, y1 = conv(v0, v1, 0)
    d1 = dd_ref[:nbr, :]
    d2 = dd_ref[nbr:, :]
    z0 = sec(0, 1) * (y0 + v0 * d1)
    z1 = sec(1, 1) * (y1 + v1 * d1)
    y0, y1 = conv(z0, z1, 1)
    o_ref[0] = (sec(0, 2) * (y0 + z0 * d2)).astype(o_ref.dtype)
    o_ref[1] = (sec(1, 2) * (y1 + z1 * d2)).astype(o_ref.dtype)


def _hyena_ctx(hy_ctx, k_time, ss, conv_w, conv_b, hy_d):
    b, t_len, nch = hy_ctx.shape
    n = 2 * t_len
    ang = -2.0 * np.pi * np.outer(np.arange(n), np.arange(n)) / n
    fr, fi = np.cos(ang), np.sin(ang)
    bf = lambda a: jnp.asarray(a.astype(np.float32)).astype(BF16)
    fwd_a = bf(np.concatenate([fr, fi], axis=1))
    fwd_b = bf(np.concatenate([-fi, fr], axis=1))
    inv_a = bf(np.concatenate([fr[:, :t_len], -fi[:, :t_len]], axis=1) / n)
    inv_b = bf(np.concatenate([fi[:, :t_len], fr[:, :t_len]], axis=1) / n)
    out = pl.pallas_call(
        functools.partial(_hyena_ctx_kernel, t_len=t_len),
        out_shape=jax.ShapeDtypeStruct((b, BRANCH_W, t_len), BF16),
        compiler_params=pltpu.CompilerParams(vmem_limit_bytes=VMEM_LIMIT),
        name="hyena_ctx",
    )(hy_ctx.transpose(0, 2, 1), k_time.T, ss.reshape(-1, 1), conv_w.reshape(3, nch, 1),
      conv_b.reshape(nch, 1), hy_d.reshape(-1, 1), fwd_a, fwd_b, inv_a, inv_b)
    return out.transpose(0, 2, 1)


def _merge_kernel(x_ref, ya_ref, yb_ref, yc_ref, za_ref, zb_ref, zc_ref, g0_ref, g1_ref, g2_ref, mod_ref,
                  gw_ref, gb_ref, wb_ref, wo_ref, fg_ref, o_ref, *, mod_row, d, final):
    ya = jax.nn.gelu(ya_ref[...].astype(F32)).astype(BF16)
    glu = jnp.dot(ya, gw_ref[...], preferred_element_type=F32) + gb_ref[...]
    y_a = glu[:, :BRANCH_W] * jax.nn.sigmoid(glu[:, BRANCH_W:])

    def branch(i, y, z_ref, g_ref):
        u = (_silu(z_ref[...].astype(F32)) * y).astype(BF16)
        return jax.nn.sigmoid(g_ref[...].astype(F32)) * jnp.dot(u, wb_ref[i], preferred_element_type=F32)

    acc = branch(0, y_a, za_ref, g0_ref)
    acc = acc + branch(1, yb_ref[...].astype(F32), zb_ref, g1_ref)
    acc = acc + branch(2, yc_ref[...].astype(F32), zc_ref, g2_ref)
    out = jnp.dot(acc.astype(BF16), wo_ref[...], preferred_element_type=F32)
    if mod_row is None:
        m = mod_ref[pl.ds(pl.program_id(0), 1), :]
    else:
        m = mod_ref[mod_row:mod_row + 1, :]
    xn = x_ref[...] + m[:, 2 * d:] * out
    if final:
        ms = jnp.mean(xn * xn, axis=-1, keepdims=True)
        xn = xn * lax.rsqrt(ms + EPS) * fg_ref[...]
    o_ref[...] = xn


def _merge(x, ya, yb, yc, p, mod, glu_w, glu_b, w_branch, w_out, final_g, *, mod_row, tm, final):
    b, t, d = x.shape
    tok = lambda w, col: pl.BlockSpec((None, tm, w), lambda bb, i: (bb, i, col // w))
    const = lambda shape: pl.BlockSpec(shape, lambda bb, i: (0,) * len(shape))
    return pl.pallas_call(
        functools.partial(_merge_kernel, mod_row=mod_row, d=d, final=final),
        grid=(b, t // tm),
        in_specs=[
            tok(d, 0), tok(512, 0), tok(512, 0), tok(512, 0),
            tok(512, COL_ZA), tok(512, COL_ZB), tok(512, COL_ZC),
            tok(1024, COL_GT), tok(1024, COL_GT + 1024), tok(1024, COL_GT + 2048),
            const((8, 3 * d)), const((BRANCH_W, 2 * BRANCH_W)), const((1, 2 * BRANCH_W)),
            const((3, BRANCH_W, d)), const((d, d)), const((1, d)),
        ],
        out_specs=tok(d, 0),
        out_shape=jax.ShapeDtypeStruct((b, t, d), F32),
        compiler_params=_cparams(("arbitrary", "arbitrary")),
        name="merge_out",
    )(x, ya, yb, yc, p, p, p, p, p, p, mod, glu_w, glu_b.reshape(1, -1), w_branch, w_out, final_g.reshape(1, d))


def kernel(x, c, ctx, c_ctx, ada_w, ada_b, norm_g, w_in, s5_lam_re, s5_lam_im, s5_log_dt, s5_b_re, s5_b_im,
           s5_c_re, s5_c_im, s5_d, s5_glu_w, s5_glu_b, na_rpb, hy_conv_w, hy_conv_b, hf_w1, hf_b1, hf_freq,
           hf_w2, hf_b2, hf_w3, hy_d, w_branch, w_out, final_g):
    bsz, t_lat, d = x.shape
    t_ctx = ctx.shape[1]
    depth = ada_w.shape[0]
    rows_n = t_lat // GRID_W
    assert bsz == 2 and d == 1024 and t_lat % (16 * FFT_N2) == 0 and t_ctx % 128 == 0

    cvec = jnp.zeros((8, d), F32).at[:bsz].set(c).at[bsz].set(c_ctx)
    mods = _modulation(cvec, ada_w, ada_b)
    cos_t, sin_t = _rope_tables(t_lat)
    n1 = 2 * t_lat // FFT_N2
    dft = _dft_consts(n1)
    tm_lat = 1024 if t_lat % 1024 == 0 else 512

    x_lat, x_ctx = x, ctx
    for l in range(depth):
        ctx_out = l < depth - 1
        w_bf = w_in[l].astype(BF16)
        p_lat = _inproj(x_lat, mods[l], norm_g[l], w_bf, mod_row=None, tm=tm_lat)
        p_ctx = _inproj(x_ctx, mods[l], norm_g[l], w_bf, mod_row=bsz, tm=t_ctx)

        ops = _s5_operators(s5_lam_re[l], s5_lam_im[l], s5_log_dt[l], s5_b_re[l], s5_b_im[l],
                            s5_c_re[l], s5_c_im[l], s5_d[l])
        ya, ya_c = _s5_mixer(p_lat, p_ctx, ops)

        qr, kt = _rope_qk(p_lat, cos_t, sin_t)
        bias = _na_bias_tables(na_rpb[l])
        yb = _na_mixer(qr, kt, p_lat, p_ctx, bias)

        k_cm, ss = _hyena_filter_time(t_lat, hf_w1[l], hf_b1[l], hf_freq[l], hf_w2[l], hf_b2[l], hf_w3[l],
                                      channel_major=True)
        kf = _hyena_filter_spectrum(k_cm, ss, dft)
        hy_cm = _to_channel_major(p_lat, COL_HY, 3 * BRANCH_W)
        yc_cm = _hyena_lat(hy_cm, kf, hy_conv_w[l], hy_conv_b[l], hy_d[l], dft)
        yc = _to_token_major(yc_cm, bsz)

        wb_bf = w_branch[l].astype(BF16)
        wo_bf = w_out[l].astype(BF16)
        gw_bf = s5_glu_w[l].astype(BF16)
        x_lat_new = _merge(x_lat, ya, yb, yc, p_lat, mods[l], gw_bf, s5_glu_b[l], wb_bf, wo_bf,
                           final_g, mod_row=None, tm=512, final=not ctx_out)
        if ctx_out:
            yb_c = _ctx_attention(p_ctx)
            kc_time, ss_c = _hyena_filter_time(t_ctx, hf_w1[l], hf_b1[l], hf_freq[l], hf_w2[l], hf_b2[l], hf_w3[l],
                                               channel_major=False)
            yc_c = _hyena_ctx(p_ctx[:, :, COL_HY:COL_HY + 1536], kc_time, ss_c, hy_conv_w[l], hy_conv_b[l], hy_d[l])
            x_ctx = _merge(x_ctx, ya_c, yb_c, yc_c, p_ctx, mods[l], gw_bf, s5_glu_b[l], wb_bf, wo_bf,
                           final_g, mod_row=bsz, tm=t_ctx, final=False)
        x_lat = x_lat_new
    return x_lat
```

```python
import functools
import math

import numpy as np
import jax
import jax.numpy as jnp
from jax import lax
from jax.experimental import pallas as pl
from jax.experimental.pallas import tpu as pltpu

F32 = jnp.float32
BF16 = jnp.bfloat16
HIGHEST = lax.Precision.HIGHEST
HIGH = lax.Precision.HIGH

GRID_W = 64
BRANCH_W = 512
S5_P = 16
S5_N = 64
S5_G = BRANCH_W // S5_P
S5_TC = 16
HEAD_DIM = 64
NA_HEADS = BRANCH_W // HEAD_DIM
NA_KR = 8
NA_KW = 16
ROPE_HALF = 16
ROPE_THETA = 10000.0
HY_EMB = 33
HY_BANDS = (HY_EMB - 1) // 2
HY_HID = 64
HY_MIN_DECAY = math.log(1e-2) / 1.5
HY_MAX_DECAY = math.log(1e-2) / 0.3
EPS = 1e-6
FFT_N2 = 256
NEG_BIG = -1e30
LOG2E = 1.4426950408889634
VMEM_LIMIT = 52 * 1024 * 1024

COL_UA, COL_ZA, COL_Q, COL_K, COL_V, COL_ZB, COL_HY, COL_ZC, COL_GT = (
    0, 512, 1024, 1536, 2048, 2560, 3072, 4608, 5120)
IN_COLS = 8192


def _cparams(sem):
    return pltpu.CompilerParams(dimension_semantics=sem, vmem_limit_bytes=VMEM_LIMIT)


def _sigmoid(x):
    return 0.5 * jnp.tanh(0.5 * x) + 0.5


def _silu(x):
    return x * _sigmoid(x)


def _mod_kernel(c_ref, w_ref, b_ref, o_ref):
    s = _silu(c_ref[...])
    o_ref[...] = jnp.dot(s, w_ref[...], preferred_element_type=F32, precision=HIGHEST) + b_ref[...]


def _modulation(cvec, ada_w, ada_b):
    depth, d, d3 = ada_w.shape
    tn = 1024
    return pl.pallas_call(
        _mod_kernel,
        grid=(depth, d3 // tn),
        in_specs=[
            pl.BlockSpec((8, d), lambda l, j: (0, 0)),
            pl.BlockSpec((None, d, tn), lambda l, j: (l, 0, j)),
            pl.BlockSpec((None, 1, tn), lambda l, j: (l, 0, j)),
        ],
        out_specs=pl.BlockSpec((None, 8, tn), lambda l, j: (l, 0, j)),
        out_shape=jax.ShapeDtypeStruct((depth, 8, d3), F32),
        compiler_params=_cparams(("arbitrary", "arbitrary")),
        name="adaln_mod",
    )(cvec, ada_w, ada_b.reshape(depth, 1, d3))


def _inproj_kernel(x_ref, mod_ref, g_ref, w_ref, o_ref, h_ref, *, mod_row, d):
    @pl.when(pl.program_id(2) == 0)
    def _():
        x = x_ref[...]
        ms = jnp.mean(x * x, axis=-1, keepdims=True)
        y = x * lax.rsqrt(ms + EPS) * g_ref[...]
        if mod_row is None:
            m = mod_ref[pl.ds(pl.program_id(0), 1), :]
        else:
            m = mod_ref[mod_row:mod_row + 1, :]
        h_ref[...] = (y * (1.0 + m[:, d:2 * d]) + m[:, :d]).astype(BF16)

    o_ref[...] = jnp.dot(h_ref[...], w_ref[...], preferred_element_type=F32).astype(o_ref.dtype)


def _inproj(x, mod, g, w_bf, *, mod_row, tm):
    b, t, d = x.shape
    n = w_bf.shape[1]
    tn = 2048
    return pl.pallas_call(
        functools.partial(_inproj_kernel, mod_row=mod_row, d=d),
        grid=(b, t // tm, n // tn),
        in_specs=[
            pl.BlockSpec((None, tm, d), lambda bb, i, j: (bb, i, 0)),
            pl.BlockSpec((8, 3 * d), lambda bb, i, j: (0, 0)),
            pl.BlockSpec((1, d), lambda bb, i, j: (0, 0)),
            pl.BlockSpec((d, tn), lambda bb, i, j: (0, j)),
        ],
        out_specs=pl.BlockSpec((None, tm, tn), lambda bb, i, j: (bb, i, j)),
        out_shape=jax.ShapeDtypeStruct((b, t, n), BF16),
        scratch_shapes=[pltpu.VMEM((tm, d), BF16)],
        compiler_params=_cparams(("arbitrary", "arbitrary", "arbitrary")),
        name="norm_inproj",
    )(x, mod, g.reshape(1, d), w_bf)


def _s5_operators(lam_re, lam_im, log_dt, b_re, b_im, c_re, c_im, d_skip):
    tc, p, n, g = S5_TC, S5_P, S5_N, S5_G
    dt = jnp.exp(log_dt)[..., None]
    er = lam_re * dt
    ei = lam_im * dt
    k = jnp.arange(tc + 1, dtype=F32)[:, None, None, None]
    mag = jnp.exp(k * er[None])
    pw_r = mag * jnp.cos(k * ei[None])
    pw_i = mag * jnp.sin(k * ei[None])
    lb_r, lb_i = pw_r[1], pw_i[1]
    den = lam_re * lam_re + lam_im * lam_im
    q_r = ((lb_r - 1.0) * lam_re + lb_i * lam_im) / den
    q_i = (lb_i * lam_re - (lb_r - 1.0) * lam_im) / den
    bb_r = q_r[..., None] * b_re - q_i[..., None] * b_im
    bb_i = q_r[..., None] * b_im + q_i[..., None] * b_re

    def cmul(ar, ai, br, bi):
        return ar * br - ai * bi, ar * bi + ai * br

    cl_r, cl_i = cmul(c_re[None], c_im[None], pw_r[:, :, :, None, :], pw_i[:, :, :, None, :])
    kern = (jnp.einsum("kdgpn,dgnq->kdgpq", cl_r[:tc], bb_r, precision=HIGH)
            - jnp.einsum("kdgpn,dgnq->kdgpq", cl_i[:tc], bb_i, precision=HIGH))
    t_in = np.arange(tc)[:, None]
    t_out = np.arange(tc)[None, :]
    lag_f = t_out - t_in
    lag_b = t_in - t_out
    kf = jnp.where((lag_f >= 0)[:, :, None, None, None], kern[np.clip(lag_f, 0, tc - 1), 0], 0.0)
    kb = jnp.where((lag_b >= 0)[:, :, None, None, None], kern[np.clip(lag_b, 0, tc - 1), 1], 0.0)
    eye_t = jnp.asarray(np.eye(tc, dtype=np.float32))
    eye_p = jnp.asarray(np.eye(p, dtype=np.float32))
    kd = eye_t[:, :, None, None, None] * (d_skip.reshape(g, p)[None, None, :, :, None] * eye_p[None, None, None])
    m_sum = (kf + kb + kd).transpose(2, 0, 4, 1, 3).reshape(g, tc * p, tc * p)

    def state_in(pr, pi, dirn):
        sr, si = cmul(pr[..., None], pi[..., None], bb_r[dirn][None], bb_i[dirn][None])
        sr = sr.transpose(1, 0, 3, 2).reshape(g, tc * p, n)
        si = si.transpose(1, 0, 3, 2).reshape(g, tc * p, n)
        return jnp.concatenate([sr, si], -1), jnp.concatenate([si, sr], -1)

    bs_f, bs_f_sw = state_in(pw_r[:tc, 0][::-1], pw_i[:tc, 0][::-1], 0)
    bs_b, bs_b_sw = state_in(pw_r[:tc, 1], pw_i[:tc, 1], 1)
    wcat = jnp.concatenate([m_sum, bs_f, bs_f_sw, bs_b, bs_b_sw], -1)

    def state_out(dirn, powers):
        cr = cl_r[powers, dirn]
        ci = cl_i[powers, dirn]
        top = cr.transpose(1, 3, 0, 2).reshape(g, n, tc * p)
        bot = (-ci).transpose(1, 3, 0, 2).reshape(g, n, tc * p)
        return jnp.concatenate([top, bot], 1)

    cs_f = state_out(0, np.arange(1, tc + 1))
    cs_b = state_out(1, np.arange(tc, 0, -1))
    ccat = jnp.concatenate([cs_f, cs_b], 1)
    ar, ai = pw_r[tc], pw_i[tc]
    a1 = jnp.concatenate([ar, ar], -1)
    a2 = jnp.concatenate([-ai, ai], -1)
    a3 = jnp.concatenate([ai, -ai], -1)
    return wcat.astype(BF16), ccat.astype(BF16), a1, a2, a3


def _s5_kernel(ul_ref, uc_ref, w_ref, c_ref, a1_ref, a2_ref, a3_ref, yl_ref, yc_ref,
               tok, xg, spf, sqf, spb, sqb, *, gb, nch, nch_lat, pitch):
    t_lat = nch_lat * S5_TC
    tok[:t_lat, :] = ul_ref[...].astype(F32)
    tok[t_lat:, :] = uc_ref[...].astype(F32)

    rt = next(r for r in (80, 40, 16, 8) if nch % r == 0)
    lane_rt = lax.broadcasted_iota(jnp.int32, (rt, 128), 1) // S5_P

    def block_transpose(a):
        for d in (4, 2, 1):
            keep = (lane_rt & d) == 0
            nxt = list(a)
            for i in range(8):
                if i & d == 0:
                    j = i + d
                    nxt[i] = jnp.where(keep, a[i], pltpu.roll(a[j], d * S5_P, 1))
                    nxt[j] = jnp.where(keep, pltpu.roll(a[i], 128 - d * S5_P, 1), a[j])
            a = nxt
        return a

    def to_compact(it, carry):
        r0 = pl.multiple_of(it * rt, rt)
        for h in range(2):
            z = [tok[pl.ds(r0 * S5_TC + 8 * h + t, rt, stride=S5_TC), :] for t in range(8)]
            b = block_transpose(z)
            for g in range(gb):
                xg[g, pl.ds(r0, rt), h * 128:(h + 1) * 128] = b[g]
        return carry

    lax.fori_loop(0, nch // rt, to_compact, 0)

    for g in range(gb):
        r = jnp.dot(xg[g].astype(BF16), w_ref[g], preferred_element_type=F32)
        xg[g] = r[:, :256]
        spf[g * pitch:g * pitch + nch, :] = r[:, 256:384]
        sqf[g * pitch:g * pitch + nch, :] = r[:, 384:512]
        spb[g * pitch:g * pitch + nch, :] = r[:, 512:640]
        sqb[g * pitch:g * pitch + nch, :] = r[:, 640:768]

    a1f, a2f, a3f = a1_ref[0], a2_ref[0], a3_ref[0]
    a1b, a2b, a3b = a1_ref[1], a2_ref[1], a3_ref[1]
    nch_ctx = nch - nch_lat

    def body(s, carry):
        pf, qf, pb, qb = carry
        cf = jnp.where(s < nch_ctx, s + nch_lat, s - nch_ctx)
        cb = nch - 1 - s
        idx_f = pl.ds(cf, gb, stride=pitch)
        idx_b = pl.ds(cb, gb, stride=pitch)
        sp = spf[idx_f, :]
        sq = sqf[idx_f, :]
        spf[idx_f, :] = pf
        pf, qf = pf * a1f + qf * a2f + sp, qf * a1f + pf * a3f + sq
        sp = spb[idx_b, :]
        sq = sqb[idx_b, :]
        spb[idx_b, :] = pb
        pb, qb = pb * a1b + qb * a2b + sp, qb * a1b + pb * a3b + sq
        return pf, qf, pb, qb

    z0 = jnp.zeros((gb, 128), F32)
    lax.fori_loop(0, nch, body, (z0, z0, z0, z0))

    for g in range(gb):
        hf = spf[g * pitch:g * pitch + nch, :].astype(BF16)
        hb = spb[g * pitch:g * pitch + nch, :].astype(BF16)
        cm = c_ref[g]
        y = (jnp.dot(hf, cm[:128, :], preferred_element_type=F32)
             + jnp.dot(hb, cm[128:, :], preferred_element_type=F32))
        xg[g] = xg[g] + y

    def to_tokens(it, carry):
        r0 = pl.multiple_of(it * rt, rt)
        for h in range(2):
            y = [xg[g, pl.ds(r0, rt), h * 128:(h + 1) * 128] for g in range(gb)]
            b = block_transpose(y)
            for t in range(8):
                tok[pl.ds(r0 * S5_TC + 8 * h + t, rt, stride=S5_TC), :] = b[t]
        return carry

    lax.fori_loop(0, nch // rt, to_tokens, 0)
    yl_ref[...] = tok[:t_lat, :].astype(yl_ref.dtype)
    yc_ref[...] = tok[t_lat:, :].astype(yc_ref.dtype)


def _s5_mixer(p_lat, p_ctx, ops):
    wcat, ccat, a1, a2, a3 = ops
    b, t_lat, _ = p_lat.shape
    t_ctx = p_ctx.shape[1]
    nch, nch_lat = (t_lat + t_ctx) // S5_TC, t_lat // S5_TC
    gb = 128 // S5_P
    pitch = ((nch + 7) // 8) * 8 + 8
    kern = functools.partial(_s5_kernel, gb=gb, nch=nch, nch_lat=nch_lat, pitch=pitch)
    once = pl.Buffered(1)
    return pl.pallas_call(
        kern,
        grid=(b, S5_G // gb),
        in_specs=[
            pl.BlockSpec((None, t_lat, 128), lambda bb, i: (bb, 0, COL_UA // 128 + i), pipeline_mode=once),
            pl.BlockSpec((None, t_ctx, 128), lambda bb, i: (bb, 0, COL_UA // 128 + i)),
            pl.BlockSpec((gb, 256, 768), lambda bb, i: (i, 0, 0)),
            pl.BlockSpec((gb, 256, 256), lambda bb, i: (i, 0, 0)),
            pl.BlockSpec((2, gb, 128), lambda bb, i: (0, i, 0)),
            pl.BlockSpec((2, gb, 128), lambda bb, i: (0, i, 0)),
            pl.BlockSpec((2, gb, 128), lambda bb, i: (0, i, 0)),
        ],
        out_specs=[
            pl.BlockSpec((None, t_lat, 128), lambda bb, i: (bb, 0, i), pipeline_mode=once),
            pl.BlockSpec((None, t_ctx, 128), lambda bb, i: (bb, 0, i)),
        ],
        out_shape=[jax.ShapeDtypeStruct((b, t_lat, BRANCH_W), BF16),
                   jax.ShapeDtypeStruct((b, t_ctx, BRANCH_W), BF16)],
        scratch_shapes=[pltpu.VMEM((t_lat + t_ctx, 128), F32), pltpu.VMEM((gb, nch, 256), F32)]
        + [pltpu.VMEM((gb * pitch, 128), F32) for _ in range(4)],
        compiler_params=_cparams(("arbitrary", "arbitrary")),
        name="s5_chunk_scan",
    )(p_lat, p_ctx, wcat, ccat, a1, a2, a3)


def _rope_tables(t_len):
    t = jnp.arange(t_len)
    rows = (t // GRID_W).astype(F32)
    cols = (t % GRID_W).astype(F32)
    inv = ROPE_THETA ** (-jnp.arange(ROPE_HALF, dtype=F32) / ROPE_HALF)
    lane = np.arange(128)
    dd = lane % HEAD_DIM
    fi = dd % ROPE_HALF
    use_row = jnp.asarray(dd < 32)
    sign = jnp.asarray(np.where((dd % 32) < ROPE_HALF, -1.0, 1.0).astype(np.float32))
    ang = jnp.where(use_row[None, :], rows[:, None], cols[:, None]) * inv[fi][None, :]
    return jnp.cos(ang), jnp.sin(ang) * sign[None, :]


def _rope_kernel(q_ref, k_ref, cos_ref, sin_ref, qo_ref, kt_ref, *, scale):
    cos = jnp.concatenate([cos_ref[...]] * 4, axis=1)
    sin = jnp.concatenate([sin_ref[...]] * 4, axis=1)
    lane = lax.broadcasted_iota(jnp.int32, cos.shape, 1)
    low = (lane % 32) < ROPE_HALF

    def rot(x):
        n = x.shape[1]
        partner = jnp.where(low, pltpu.roll(x, n - ROPE_HALF, 1), pltpu.roll(x, ROPE_HALF, 1))
        return x * cos + partner * sin

    qo_ref[...] = (rot(q_ref[...].astype(F32)) * scale).astype(BF16)
    kr = rot(k_ref[...].astype(F32))
    for c in range(kt_ref.shape[0]):
        kt_ref[c] = kr[c * 128:(c + 1) * 128, :].T.astype(BF16)


def _rope_qk(p_lat, cos_t, sin_t):
    b, t, _ = p_lat.shape
    tm = 1024
    return pl.pallas_call(
        functools.partial(_rope_kernel, scale=HEAD_DIM ** -0.5 * LOG2E),
        grid=(b, t // tm),
        in_specs=[
            pl.BlockSpec((None, tm, 512), lambda bb, i: (bb, i, COL_Q // 512)),
            pl.BlockSpec((None, tm, 512), lambda bb, i: (bb, i, COL_K // 512)),
            pl.BlockSpec((tm, 128), lambda bb, i: (i, 0)),
            pl.BlockSpec((tm, 128), lambda bb, i: (i, 0)),
        ],
        out_specs=[pl.BlockSpec((None, tm, 512), lambda bb, i: (bb, i, 0)),
                   pl.BlockSpec((None, tm // 128, 512, 128), lambda bb, i: (bb, i, 0, 0))],
        out_shape=[jax.ShapeDtypeStruct((b, t, BRANCH_W), BF16),
                   jax.ShapeDtypeStruct((b, t // 128, BRANCH_W, 128), BF16)],
        compiler_params=_cparams(("arbitrary", "arbitrary")),
        name="rope_qk",
    )(p_lat, p_lat, cos_t, sin_t)


_NA_VARIANTS = (((4, 5), (0, 1)), ((0, 1), (0, 0)), ((2, 3), (0, 0)), ((4, 5), (0, 0)), ((6, 7), (0, 0)))
NA_WIN = NA_KR + 2


def _na_bias_tables(rpb):
    h = rpb.shape[0]
    qc = np.arange(GRID_W)
    kc = np.arange(GRID_W)
    cs = np.clip(qc - NA_KW // 2, 0, GRID_W - NA_KW)
    valid_c = (kc[None, :] >= cs[:, None]) & (kc[None, :] < cs[:, None] + NA_KW)
    rel_c = np.clip(kc[None, :] - qc[:, None] + NA_KW - 1, 0, 2 * NA_KW - 2)
    oh_c = np.eye(2 * NA_KW - 1, dtype=np.float32)[rel_c]
    j = np.arange(NA_WIN)
    oh_r = np.zeros((len(_NA_VARIANTS), 2, NA_WIN, 2 * NA_KR - 1), np.float32)
    valid_r = np.zeros((len(_NA_VARIANTS), 2, NA_WIN), bool)
    for n, (offs, los) in enumerate(_NA_VARIANTS):
        for rr in range(2):
            valid_r[n, rr] = (j >= los[rr]) & (j < los[rr] + NA_KR)
            rel_r = np.clip(j - offs[rr] + NA_KR - 1, 0, 2 * NA_KR - 2)
            oh_r[n, rr] = np.eye(2 * NA_KR - 1, dtype=np.float32)[rel_r]
    bias = jnp.einsum("nrja,hab,ckb->nhrcjk", jnp.asarray(oh_r), rpb, jnp.asarray(oh_c), precision=HIGH) * LOG2E
    valid = valid_r[:, None, :, None, :, None] & valid_c[None, None, None, :, None, :]
    bias = jnp.where(jnp.asarray(valid), bias, NEG_BIG)
    return bias.reshape(len(_NA_VARIANTS), h // 2, 4 * GRID_W, NA_WIN * GRID_W)


def _na_kernel(q_ref, kp_ref, kc_ref, kn_ref, vp_ref, vc_ref, vn_ref, kx_ref, vx_ref, bias_ref, o_ref,
               kt3, vcat, *, ni):
    half = 4 * GRID_W
    tq = 8 * GRID_W
    kt3[0:2] = kp_ref[2:4]
    kt3[2:6] = kc_ref[...]
    kt3[6:8] = kn_ref[0:2]
    vcat[:half, :] = vp_ref[half:, :]
    vcat[half:half + tq, :] = vc_ref[...]
    vcat[half + tq:, :] = vn_ref[:half, :]
    kxt = kx_ref[...].astype(F32).T.astype(BF16)
    i = pl.program_id(1)
    head0 = lax.broadcasted_iota(jnp.int32, (2 * GRID_W, 128), 1) < HEAD_DIM
    nchunk = NA_WIN // 2

    def pair_body(a, carry):
        first = i == 0
        last = i == ni - 1
        s0 = jnp.where(first, jnp.maximum(2 * a, 4), jnp.where(last, jnp.minimum(2 * a, 4), 2 * a))
        var = jnp.where(first & (a < 2), 1 + a, jnp.where(last & (a >= 2), 1 + a, 0))
        q0 = pl.multiple_of(a * 2 * GRID_W, 2 * GRID_W)
        k0 = pl.multiple_of(s0 * GRID_W, 2 * GRID_W)
        sp = s0 // 2
        outs = []
        for hp in range(NA_HEADS // 2):
            ls = slice(hp * 128, (hp + 1) * 128)
            qp = q_ref[pl.ds(q0, 2 * GRID_W), ls]
            zero = jnp.zeros_like(qp)
            q2 = jnp.concatenate([jnp.where(head0, qp, zero), jnp.where(head0, zero, qp)], axis=0)
            kw = kt3[pl.ds(sp, nchunk), ls, :]
            s_loc = jnp.concatenate(
                [jnp.dot(q2, kw[c], preferred_element_type=F32) for c in range(nchunk)], axis=1) + bias_ref[var, hp]
            s_ctx = jnp.dot(q2, kxt[ls, :], preferred_element_type=F32)
            m = jnp.maximum(jnp.max(s_loc, axis=-1, keepdims=True), jnp.max(s_ctx, axis=-1, keepdims=True))
            p_loc = jnp.exp2(s_loc - m)
            p_ctx = jnp.exp2(s_ctx - m)
            den = jnp.sum(p_loc, axis=-1, keepdims=True) + jnp.sum(p_ctx, axis=-1, keepdims=True)
            vw = vcat[pl.ds(k0, NA_WIN * GRID_W), ls]
            o = (jnp.dot(p_loc.astype(BF16), vw, preferred_element_type=F32)
                 + jnp.dot(p_ctx.astype(BF16), vx_ref[:, ls], preferred_element_type=F32))
            o = o / den
            outs.append(jnp.where(head0, o[:2 * GRID_W], o[2 * GRID_W:]))
        o_ref[pl.ds(q0, 2 * GRID_W), :] = jnp.concatenate(outs, axis=1).astype(o_ref.dtype)
        return carry

    lax.fori_loop(0, 4, pair_body, 0)


def _na_mixer(qr, kt, p_lat, p_ctx, bias):
    b, t, _ = qr.shape
    nc = p_ctx.shape[1]
    tq = 8 * GRID_W
    ni = t // tq
    assert ni >= 2
    blk = lambda off, col: pl.BlockSpec(
        (None, tq, BRANCH_W), lambda bb, i: (bb, jnp.clip(i + off, 0, ni - 1), col))
    ktb = lambda off: pl.BlockSpec(
        (None, tq // 128, BRANCH_W, 128), lambda bb, i: (bb, jnp.clip(i + off, 0, ni - 1), 0, 0))
    cv = COL_V // BRANCH_W
    return pl.pallas_call(
        functools.partial(_na_kernel, ni=ni),
        grid=(b, ni),
        in_specs=[
            blk(0, 0), ktb(-1), ktb(0), ktb(1),
            blk(-1, cv), blk(0, cv), blk(1, cv),
            pl.BlockSpec((None, nc, BRANCH_W), lambda bb, i: (bb, 0, COL_K // BRANCH_W)),
            pl.BlockSpec((None, nc, BRANCH_W), lambda bb, i: (bb, 0, cv)),
            pl.BlockSpec(bias.shape, lambda bb, i: (0, 0, 0, 0), pipeline_mode=pl.Buffered(1)),
        ],
        out_specs=pl.BlockSpec((None, tq, BRANCH_W), lambda bb, i: (bb, i, 0)),
        out_shape=jax.ShapeDtypeStruct((b, t, BRANCH_W), BF16),
        scratch_shapes=[pltpu.VMEM((2 * tq // 128, BRANCH_W, 128), BF16), pltpu.VMEM((2 * tq, BRANCH_W), BF16)],
        compiler_params=_cparams(("arbitrary", "arbitrary")),
        name="na_attention",
    )(qr, kt, kt, kt, p_lat, p_lat, p_lat, p_ctx, p_ctx, bias)


def _ctx_attn_kernel(q_ref, k_ref, v_ref, o_ref, *, scale):
    dn = (((1,), (1,)), ((), ()))
    outs = []
    for hh in range(2):
        sl = slice(hh * HEAD_DIM, (hh + 1) * HEAD_DIM)
        qh = (q_ref[:, sl].astype(F32) * scale).astype(BF16)
        s = lax.dot_general(qh, k_ref[:, sl], dn, preferred_element_type=F32)
        m = jnp.max(s, axis=-1, keepdims=True)
        p = jnp.exp(s - m)
        den = jnp.sum(p, axis=-1, keepdims=True)
        outs.append(jnp.dot(p.astype(BF16), v_ref[:, sl], preferred_element_type=F32) / den)
    o_ref[...] = jnp.concatenate(outs, axis=1).astype(o_ref.dtype)


def _ctx_attention(p_ctx):
    b, nc, _ = p_ctx.shape
    spec = lambda col: pl.BlockSpec((None, nc, 128), lambda bb, hp: (bb, 0, col // 128 + hp))
    return pl.pallas_call(
        functools.partial(_ctx_attn_kernel, scale=HEAD_DIM ** -0.5),
        grid=(b, NA_HEADS // 2),
        in_specs=[spec(COL_Q), spec(COL_K), spec(COL_V)],
        out_specs=pl.BlockSpec((None, nc, 128), lambda bb, hp: (bb, 0, hp)),
        out_shape=jax.ShapeDtypeStruct((b, nc, BRANCH_W), BF16),
        compiler_params=_cparams(("arbitrary", "arbitrary")),
        name="ctx_attention",
    )(p_ctx, p_ctx, p_ctx)


def _filter_features(t_len):
    pos = jnp.arange(t_len, dtype=F32)
    t = pos / max(t_len - 1, 1)
    w = 2.0 * math.pi * pos / t_len
    bands = jnp.linspace(1e-4, HY_BANDS - 1, HY_BANDS, dtype=F32)
    feats = jnp.concatenate([t[:, None], jnp.cos(w[:, None] * bands), -jnp.sin(w[:, None] * bands)], axis=-1)
    feats2 = jnp.concatenate([feats, feats[:1], feats[:0:-1]], axis=0)
    return jnp.pad(feats2, ((0, 0), (0, 128 - HY_EMB)))


def _filter_kernel(f_ref, w1_ref, b1_ref, fr_ref, w2_ref, b2_ref, w3_ref, dl_ref, k_ref, ss_ref):
    f = f_ref[...]
    fr = fr_ref[...]
    h = jnp.sin(fr * (jnp.dot(f, w1_ref[...], preferred_element_type=F32, precision=HIGHEST) + b1_ref[...]))
    h = jnp.sin(fr * (jnp.dot(h, w2_ref[...], preferred_element_type=F32, precision=HIGHEST) + b2_ref[...]))
    k = jnp.dot(h, w3_ref[...], preferred_element_type=F32, precision=HIGHEST)
    k = k * jnp.exp(-f[:, 0:1] * dl_ref[...])
    k_ref[...] = k

    @pl.when(pl.program_id(0) == 0)
    def _():
        ss_ref[...] = jnp.zeros_like(ss_ref)

    ss_ref[...] += jnp.sum(k * k, axis=0, keepdims=True)


def _hyena_filter_time(t_len, w1, b1, freq, w2, b2, w3):
    n = 2 * t_len
    tm = min(1024, t_len)
    feats = _filter_features(t_len)
    w1p = jnp.pad(w1, ((0, 128 - HY_EMB), (0, 0)))
    deltas = jnp.abs(jnp.linspace(HY_MIN_DECAY, HY_MAX_DECAY, BRANCH_W, dtype=F32))
    dl = jnp.concatenate([deltas, deltas]).reshape(1, 2 * BRANCH_W)
    half = t_len // tm
    nco = 2 * BRANCH_W
    return pl.pallas_call(
        _filter_kernel,
        grid=(n // tm,),
        in_specs=[
            pl.BlockSpec((tm, 128), lambda i: (i, 0)),
            pl.BlockSpec((128, HY_HID), lambda i: (0, 0)),
            pl.BlockSpec((1, HY_HID), lambda i: (0, 0)),
            pl.BlockSpec((1, HY_HID), lambda i: (0, 0)),
            pl.BlockSpec((HY_HID, HY_HID), lambda i: (0, 0)),
            pl.BlockSpec((1, HY_HID), lambda i: (0, 0)),
            pl.BlockSpec((HY_HID, nco), lambda i: (0, i // half)),
            pl.BlockSpec((1, nco), lambda i: (0, 0)),
        ],
        out_specs=[pl.BlockSpec((tm, nco), lambda i: (i, 0)), pl.BlockSpec((1, nco), lambda i: (0, 0))],
        out_shape=[jax.ShapeDtypeStruct((n, nco), F32), jax.ShapeDtypeStruct((1, nco), F32)],
        compiler_params=_cparams(("arbitrary",)),
        name="hyena_filter_ffn_ctx",
    )(feats, w1p, b1.reshape(1, -1), freq.reshape(1, -1), w2, b2.reshape(1, -1), w3, dl)


def _filter_cm_kernel(f_ref, t_ref, w1_ref, b1_ref, fr_ref, w2_ref, b2_ref, w3_ref, dl_ref, k_ref, ss_ref):
    fr = fr_ref[...]
    h = jnp.sin(fr * (jnp.dot(w1_ref[...], f_ref[...], preferred_element_type=F32, precision=HIGHEST) + b1_ref[...]))
    h = jnp.sin(fr * (jnp.dot(w2_ref[...], h, preferred_element_type=F32, precision=HIGHEST) + b2_ref[...]))
    k = lax.dot_general(h.astype(BF16), w3_ref[...].astype(BF16), (((0,), (0,)), ((), ())), preferred_element_type=F32)
    k = k * jnp.exp(-t_ref[...] * dl_ref[...])
    rows = k.shape[0] // FFT_N2
    for cb in range(k.shape[1] // 128):
        piece = k[:, cb * 128:(cb + 1) * 128].reshape(rows, FFT_N2, 128)
        k_ref[cb * 128:(cb + 1) * 128] = jnp.swapaxes(jnp.swapaxes(piece, 1, 2), 0, 1)

    @pl.when(pl.program_id(0) == 0)
    def _():
        ss_ref[...] = jnp.zeros_like(ss_ref)

    ss_ref[...] += jnp.sum(k * k, axis=0, keepdims=True)


def _hyena_filter_cm(t_len, w1, b1, freq, w2, b2, w3):
    n = 2 * t_len
    tm = 2048
    assert t_len % tm == 0
    slot = jnp.arange(n)
    pos = jnp.where(slot < t_len, slot, jnp.where(slot == t_len, 0, n - slot)).astype(F32)
    t = pos / max(t_len - 1, 1)
    w = 2.0 * math.pi * pos / t_len
    bands = jnp.linspace(1e-4, HY_BANDS - 1, HY_BANDS, dtype=F32)
    feats = jnp.concatenate([t[None, :], jnp.cos(w[None, :] * bands[:, None]), -jnp.sin(w[None, :] * bands[:, None])],
                            axis=0)
    feats = jnp.pad(feats, ((0, 128 - HY_EMB), (0, 0)))
    w1t = jnp.pad(w1, ((0, 128 - HY_EMB), (0, 0))).T
    deltas = jnp.abs(jnp.linspace(HY_MIN_DECAY, HY_MAX_DECAY, BRANCH_W, dtype=F32))
    nco = 2 * BRANCH_W
    dl = jnp.concatenate([deltas, deltas]).reshape(1, nco)
    half = t_len // tm
    col = lambda a: a.reshape(-1, 1)
    const = lambda shape: pl.BlockSpec(shape, lambda i: (0, 0))
    return pl.pallas_call(
        _filter_cm_kernel,
        grid=(n // tm,),
        in_specs=[
            pl.BlockSpec((128, tm), lambda i: (0, i)),
            pl.BlockSpec((tm, 1), lambda i: (i, 0)),
            const((HY_HID, 128)), const((HY_HID, 1)), const((HY_HID, 1)),
            const((HY_HID, HY_HID)), const((HY_HID, 1)),
            pl.BlockSpec((HY_HID, nco), lambda i: (0, i // half)),
            const((1, nco)),
        ],
        out_specs=[pl.BlockSpec((nco, tm // FFT_N2, FFT_N2), lambda i: (0, i, 0)), const((1, nco))],
        out_shape=[jax.ShapeDtypeStruct((nco, n // FFT_N2, FFT_N2), F32), jax.ShapeDtypeStruct((1, nco), F32)],
        compiler_params=_cparams(("arbitrary",)),
        name="hyena_filter_ffn",
    )(feats, col(t), w1t, col(b1), col(freq), w2.T, col(b2), w3, dl)


def _dft_consts(n1):
    n2 = FFT_N2
    n = n1 * n2
    nh = n1 // 2
    a1 = -2.0 * np.pi * np.outer(np.arange(n1), np.arange(n1)) / n1
    f1r, f1i = np.cos(a1), np.sin(a1)
    a2 = -2.0 * np.pi * np.outer(np.arange(n2), np.arange(n2)) / n2
    f2r, f2i = np.cos(a2), np.sin(a2)
    at = -2.0 * np.pi * np.outer(np.arange(n1), np.arange(n2)) / n
    bf = lambda a: jnp.asarray(a.astype(np.float32)).astype(BF16)
    return dict(
        s1_data=bf(np.block([[f1r[:, :nh], -f1i[:, :nh]], [f1i[:, :nh], f1r[:, :nh]]])),
        s1_real=bf(np.concatenate([f1r, f1i], axis=0)),
        s6=bf(np.block([[f1r[:nh], f1i[:nh]], [-f1i[:nh], f1r[:nh]]]) / n),
        fa=bf(np.concatenate([f2r, f2i], axis=1)), fb=bf(np.concatenate([-f2i, f2r], axis=1)),
        ia=bf(np.concatenate([f2r, -f2i], axis=1)), ib=bf(np.concatenate([f2i, f2r], axis=1)),
        tw_r=jnp.asarray(np.cos(at).astype(np.float32)), tw_i=jnp.asarray(np.sin(at).astype(np.float32)),
    )


def _lane_dft(ar_ref, ai_ref, fa_ref, fb_ref):
    cb, n1, n2 = ar_ref.shape
    ar = ar_ref[...].reshape(cb * n1, n2).astype(BF16)
    ai = ai_ref[...].reshape(cb * n1, n2).astype(BF16)
    x = (jnp.dot(ar, fa_ref[...], preferred_element_type=F32)
         + jnp.dot(ai, fb_ref[...], preferred_element_type=F32))
    return x[:, :n2], x[:, n2:]


def _kf_kernel(k_ref, ss_ref, s1_ref, twr_ref, twi_ref, fa_ref, fb_ref, o_ref, ar_ref, ai_ref, *, cb, n1):
    i = pl.program_id(0)
    row = lax.broadcasted_iota(jnp.int32, (n1, FFT_N2), 0)
    lane = lax.broadcasted_iota(jnp.int32, (n1, FFT_N2), 1)
    keep = jnp.logical_not((row == n1 // 2) & (lane == 0))
    twr, twi = twr_ref[...], twi_ref[...]
    for c in range(cb):
        scale = lax.rsqrt(jnp.full((n1, FFT_N2), ss_ref[i * cb + c], F32) + EPS)
        k = jnp.where(keep, k_ref[c] * scale, 0.0)
        a = jnp.dot(s1_ref[...], k.astype(BF16), preferred_element_type=F32)
        a_r, a_i = a[:n1], a[n1:]
        ar_ref[c] = a_r * twr - a_i * twi
        ai_ref[c] = a_r * twi + a_i * twr
    xr, xi = _lane_dft(ar_ref, ai_ref, fa_ref, fb_ref)
    o_ref[:, :n1, :] = xr.reshape(cb, n1, FFT_N2)
    o_ref[:, n1:, :] = xi.reshape(cb, n1, FFT_N2)


def _hyena_filter_spectrum(kt, ss, dft):
    nco, n1, _ = kt.shape
    cb = 8
    const = lambda shape: pl.BlockSpec(shape, lambda i: (0,) * len(shape))
    return pl.pallas_call(
        functools.partial(_kf_kernel, cb=cb, n1=n1),
        grid=(nco // cb,),
        in_specs=[
            pl.BlockSpec((cb, n1, FFT_N2), lambda i: (i, 0, 0)),
            pl.BlockSpec(memory_space=pltpu.SMEM),
            const((2 * n1, n1)), const((n1, FFT_N2)), const((n1, FFT_N2)),
            const((FFT_N2, 2 * FFT_N2)), const((FFT_N2, 2 * FFT_N2)),
        ],
        out_specs=pl.BlockSpec((cb, 2 * n1, FFT_N2), lambda i: (i, 0, 0)),
        out_shape=jax.ShapeDtypeStruct((nco, 2 * n1, FFT_N2), F32),
        scratch_shapes=[pltpu.VMEM((cb, n1, FFT_N2), F32), pltpu.VMEM((cb, n1, FFT_N2), F32)],
        compiler_params=_cparams(("arbitrary",)),
        name="hyena_filter_fft",
    )(kt, ss.reshape(nco), dft["s1_real"], dft["tw_r"], dft["tw_i"], dft["fa"], dft["fb"])


def _tok2cm_kernel(x_ref, o_ref):
    rows = x_ref.shape[0] // FFT_N2
    x3 = x_ref[...].reshape(rows, FFT_N2, 128)
    o_ref[...] = jnp.swapaxes(jnp.swapaxes(x3, 1, 2), 0, 1)


def _to_channel_major(p, col0, ncols):
    b, t, _ = p.shape
    tt = 16 * FFT_N2
    nt = t // tt
    return pl.pallas_call(
        _tok2cm_kernel,
        grid=(ncols // 128, b, nt),
        in_specs=[pl.BlockSpec((None, tt, 128), lambda c, bb, i: (bb, i, col0 // 128 + c))],
        out_specs=pl.BlockSpec((128, tt // FFT_N2, FFT_N2), lambda c, bb, i: (c, bb * nt + i, 0)),
        out_shape=jax.ShapeDtypeStruct((ncols, b * t // FFT_N2, FFT_N2), p.dtype),
        compiler_params=_cparams(("arbitrary", "arbitrary", "arbitrary")),
        name="to_channel_major",
    )(p)


def _cm2tok_kernel(x_ref, o_ref):
    x3 = jnp.swapaxes(jnp.swapaxes(x_ref[...], 0, 1), 1, 2)
    o_ref[...] = x3.reshape(o_ref.shape)


def _to_token_major(y_cm, b):
    c, rows, _ = y_cm.shape
    t = rows * FFT_N2 // b
    tt = 16 * FFT_N2
    nt = t // tt
    return pl.pallas_call(
        _cm2tok_kernel,
        grid=(c // 128, b, nt),
        in_specs=[pl.BlockSpec((128, tt // FFT_N2, FFT_N2), lambda cc, bb, i: (cc, bb * nt + i, 0))],
        out_specs=pl.BlockSpec((None, tt, 128), lambda cc, bb, i: (bb, i, cc)),
        out_shape=jax.ShapeDtypeStruct((b, t, c), y_cm.dtype),
        compiler_params=_cparams(("arbitrary", "arbitrary", "arbitrary")),
        name="to_token_major",
    )(y_cm)


def _hyena_kernel(v_ref, x1_ref, x2_ref, kf1_ref, kf2_ref, cw_ref, cbias_ref, dd_ref,
                  s1_ref, s6_ref, twr_ref, twi_ref, fa_ref, fb_ref, ia_ref, ib_ref,
                  o_ref, vs, x1s, x2s, ar_ref, ai_ref, *, cb, n1, nch):
    i = pl.program_id(0)
    nh = n1 // 2
    rows = 2 * nh
    row = lax.broadcasted_iota(jnp.int32, (rows, FFT_N2), 0)
    lane = lax.broadcasted_iota(jnp.int32, (rows, FFT_N2), 1)
    first = lane == 0
    last = lane == FFT_N2 - 1
    seq_start = first & ((row % nh) == 0)
    seq_end = last & ((row % nh) == nh - 1)
    twr, twi = twr_ref[...], twi_ref[...]

    def short_conv(x, ch):
        prev = pltpu.roll(x, 1, 1)
        prev = jnp.where(first, pltpu.roll(prev, 1, 0), prev)
        prev = jnp.where(seq_start, 0.0, prev)
        nxt = pltpu.roll(x, FFT_N2 - 1, 1)
        nxt = jnp.where(last, pltpu.roll(nxt, rows - 1, 0), nxt)
        nxt = jnp.where(seq_end, 0.0, nxt)
        return cw_ref[ch] * prev + cw_ref[nch + ch] * x + cw_ref[2 * nch + ch] * nxt + cbias_ref[ch]

    def stage1(z, c):
        a = jnp.dot(s1_ref[...], z.astype(BF16), preferred_element_type=F32)
        a_r, a_i = a[:n1], a[n1:]
        ar_ref[c] = a_r * twr - a_i * twi
        ai_ref[c] = a_r * twi + a_i * twr

    def spectral(kf_ref):
        xr, xi = _lane_dft(ar_ref, ai_ref, fa_ref, fb_ref)
        kr = kf_ref[:, :n1, :].reshape(cb * n1, FFT_N2)
        ki = kf_ref[:, n1:, :].reshape(cb * n1, FFT_N2)
        yr = (xr * kr - xi * ki).astype(BF16)
        yi = (xr * ki + xi * kr).astype(BF16)
        bm = (jnp.dot(yr, ia_ref[...], preferred_element_type=F32)
              + jnp.dot(yi, ib_ref[...], preferred_element_type=F32))
        br = bm[:, :FFT_N2].reshape(cb, n1, FFT_N2)
        bi = bm[:, FFT_N2:].reshape(cb, n1, FFT_N2)
        ar_ref[...] = br * twr[None] + bi * twi[None]
        ai_ref[...] = bi * twr[None] - br * twi[None]

    def stage6(c):
        bcat = jnp.concatenate([ar_ref[c], ai_ref[c]], axis=0).astype(BF16)
        return jnp.dot(s6_ref[...], bcat, preferred_element_type=F32)

    nbr = BRANCH_W
    for c in range(cb):
        ch = i * cb + c
        v = short_conv(v_ref[c].astype(F32), ch)
        vs[c] = v
        x1s[c] = short_conv(x1_ref[c].astype(F32), nbr + ch)
        x2s[c] = short_conv(x2_ref[c].astype(F32), 2 * nbr + ch)
        stage1(v, c)
    spectral(kf1_ref)
    for c in range(cb):
        ch = i * cb + c
        z = x1s[c] * (stage6(c) + vs[c] * dd_ref[ch])
        vs[c] = z
        stage1(z, c)
    spectral(kf2_ref)
    for c in range(cb):
        ch = i * cb + c
        o_ref[c] = (x2s[c] * (stage6(c) + vs[c] * dd_ref[nbr + ch])).astype(o_ref.dtype)


def _hyena_lat(hy_cm, kf, conv_w, conv_b, hy_d, dft):
    nch, rows, _ = hy_cm.shape
    n1 = rows
    cb = 8
    nblk = BRANCH_W // cb
    const = lambda shape: pl.BlockSpec(shape, lambda i: (0,) * len(shape))
    smem = pl.BlockSpec(memory_space=pltpu.SMEM)
    data = lambda sec: pl.BlockSpec((cb, rows, FFT_N2), lambda i: (sec * nblk + i, 0, 0))
    kfs = lambda o: pl.BlockSpec((cb, 2 * n1, FFT_N2), lambda i: (o * nblk + i, 0, 0))
    return pl.pallas_call(
        functools.partial(_hyena_kernel, cb=cb, n1=n1, nch=nch),
        grid=(nblk,),
        in_specs=[
            data(0), data(1), data(2), kfs(0), kfs(1), smem, smem, smem,
            const((2 * n1, n1)), const((n1, 2 * n1)), const((n1, FFT_N2)), const((n1, FFT_N2)),
            const((FFT_N2, 2 * FFT_N2)), const((FFT_N2, 2 * FFT_N2)),
            const((FFT_N2, 2 * FFT_N2)), const((FFT_N2, 2 * FFT_N2)),
        ],
        out_specs=pl.BlockSpec((cb, rows, FFT_N2), lambda i: (i, 0, 0)),
        out_shape=jax.ShapeDtypeStruct((BRANCH_W, rows, FFT_N2), BF16),
        scratch_shapes=[pltpu.VMEM((cb, rows, FFT_N2), F32) for _ in range(3)]
        + [pltpu.VMEM((cb, n1, FFT_N2), F32) for _ in range(2)],
        compiler_params=_cparams(("arbitrary",)),
        name="hyena_fftconv",
    )(hy_cm, hy_cm, hy_cm, kf, kf, conv_w.reshape(-1), conv_b, hy_d.reshape(-1),
      dft["s1_data"], dft["s6"], dft["tw_r"], dft["tw_i"], dft["fa"], dft["fb"], dft["ia"], dft["ib"])


def _hyena_ctx_kernel(hy_ref, k_ref, ss_ref, cw_ref, cbias_ref, dd_ref, fwd_a_ref, fwd_b_ref,
                      inv_a_ref, inv_b_ref, o_ref, *, t_len):
    nbr = BRANCH_W
    lane = lax.broadcasted_iota(jnp.int32, (nbr, t_len), 1)

    def short_conv(x, sec):
        sl = slice(sec * nbr, (sec + 1) * nbr)
        prev = jnp.where(lane == 0, 0.0, pltpu.roll(x, 1, 1))
        nxt = jnp.where(lane == t_len - 1, 0.0, pltpu.roll(x, t_len - 1, 1))
        return cw_ref[0, sl, :] * prev + cw_ref[1, sl, :] * x + cw_ref[2, sl, :] * nxt + cbias_ref[sl, :]

    def sec(b, s):
        return short_conv(hy_ref[b, s * nbr:(s + 1) * nbr, :].astype(F32), s)

    klane = lax.broadcasted_iota(jnp.int32, k_ref.shape, 1)
    kk = jnp.where(klane == t_len, 0.0, k_ref[...] * lax.rsqrt(ss_ref[...] + EPS))
    kf = jnp.dot(kk.astype(BF16), fwd_a_ref[...], preferred_element_type=F32)
    n = 2 * t_len

    def conv(z0, z1, order):
        x = (jnp.dot(z0.astype(BF16), fwd_a_ref[:t_len, :], preferred_element_type=F32)
             + jnp.dot(z1.astype(BF16), fwd_b_ref[:t_len, :], preferred_element_type=F32))
        xr, xi = x[:, :n], x[:, n:]
        kr = kf[order * nbr:(order + 1) * nbr, :n]
        ki = kf[order * nbr:(order + 1) * nbr, n:]
        yr = (xr * kr - xi * ki).astype(BF16)
        yi = (xr * ki + xi * kr).astype(BF16)
        y = (jnp.dot(yr, inv_a_ref[...], preferred_element_type=F32)
             + jnp.dot(yi, inv_b_ref[...], preferred_element_type=F32))
        return y[:, :t_len], y[:, t_len:]

    v0, v1 = sec(0, 0), sec(1, 0)
    y0, y1 = conv(v0, v1, 0)
    d1 = dd_ref[:nbr, :]
    d2 = dd_ref[nbr:, :]
    z0 = sec(0, 1) * (y0 + v0 * d1)
    z1 = sec(1, 1) * (y1 + v1 * d1)
    y0, y1 = conv(z0, z1, 1)
    o_ref[0] = (sec(0, 2) * (y0 + z0 * d2)).astype(o_ref.dtype)
    o_ref[1] = (sec(1, 2) * (y1 + z1 * d2)).astype(o_ref.dtype)


def _hyena_ctx(hy_ctx, k_time, ss, conv_w, conv_b, hy_d):
    b, t_len, nch = hy_ctx.shape
    n = 2 * t_len
    ang = -2.0 * np.pi * np.outer(np.arange(n), np.arange(n)) / n
    fr, fi = np.cos(ang), np.sin(ang)
    bf = lambda a: jnp.asarray(a.astype(np.float32)).astype(BF16)
    fwd_a = bf(np.concatenate([fr, fi], axis=1))
    fwd_b = bf(np.concatenate([-fi, fr], axis=1))
    inv_a = bf(np.concatenate([fr[:, :t_len], -fi[:, :t_len]], axis=1) / n)
    inv_b = bf(np.concatenate([fi[:, :t_len], fr[:, :t_len]], axis=1) / n)
    out = pl.pallas_call(
        functools.partial(_hyena_ctx_kernel, t_len=t_len),
        out_shape=jax.ShapeDtypeStruct((b, BRANCH_W, t_len), BF16),
        compiler_params=pltpu.CompilerParams(vmem_limit_bytes=VMEM_LIMIT),
        name="hyena_ctx",
    )(hy_ctx.transpose(0, 2, 1), k_time.T, ss.reshape(-1, 1), conv_w.reshape(3, nch, 1),
      conv_b.reshape(nch, 1), hy_d.reshape(-1, 1), fwd_a, fwd_b, inv_a, inv_b)
    return out.transpose(0, 2, 1)


def _merge_kernel(x_ref, ya_ref, yb_ref, yc_ref, za_ref, zb_ref, zc_ref, g0_ref, g1_ref, g2_ref, mod_ref,
                  gw_ref, gb_ref, wb_ref, wo_ref, fg_ref, o_ref, *, mod_row, d, final):
    ya = jax.nn.gelu(ya_ref[...].astype(F32)).astype(BF16)
    glu = jnp.dot(ya, gw_ref[...], preferred_element_type=F32) + gb_ref[...]
    y_a = glu[:, :BRANCH_W] * (1.0 + jnp.tanh(glu[:, BRANCH_W:]))

    def branch(i, y, z_ref, g_ref):
        zh = z_ref[...]
        u = (zh + zh * jnp.tanh(zh)) * y
        return (1.0 + jnp.tanh(g_ref[...].astype(F32))) * jnp.dot(u, wb_ref[i], preferred_element_type=F32)

    acc = branch(0, y_a.astype(BF16), za_ref, g0_ref)
    acc = acc + branch(1, yb_ref[...], zb_ref, g1_ref)
    acc = acc + branch(2, yc_ref[...], zc_ref, g2_ref)
    out = jnp.dot(acc.astype(BF16), wo_ref[...], preferred_element_type=F32)
    if mod_row is None:
        m = mod_ref[pl.ds(pl.program_id(0), 1), :]
    else:
        m = mod_ref[mod_row:mod_row + 1, :]
    xn = x_ref[...] + m[:, 2 * d:] * out
    if final:
        ms = jnp.mean(xn * xn, axis=-1, keepdims=True)
        xn = xn * lax.rsqrt(ms + EPS) * fg_ref[...]
    o_ref[...] = xn


def _merge(x, ya, yb, yc, p, mod, glu_w, glu_b, w_branch, w_out, final_g, *, mod_row, tm, final):
    b, t, d = x.shape
    tok = lambda w, col: pl.BlockSpec((None, tm, w), lambda bb, i: (bb, i, col // w))
    const = lambda shape: pl.BlockSpec(shape, lambda bb, i: (0,) * len(shape))
    return pl.pallas_call(
        functools.partial(_merge_kernel, mod_row=mod_row, d=d, final=final),
        grid=(b, t // tm),
        in_specs=[
            tok(d, 0), tok(512, 0), tok(512, 0), tok(512, 0),
            tok(512, COL_ZA), tok(512, COL_ZB), tok(512, COL_ZC),
            tok(1024, COL_GT), tok(1024, COL_GT + 1024), tok(1024, COL_GT + 2048),
            const((8, 3 * d)), const((BRANCH_W, 2 * BRANCH_W)), const((1, 2 * BRANCH_W)),
            const((3, BRANCH_W, d)), const((d, d)), const((1, d)),
        ],
        out_specs=tok(d, 0),
        out_shape=jax.ShapeDtypeStruct((b, t, d), F32),
        compiler_params=_cparams(("arbitrary", "arbitrary")),
        name="merge_out",
    )(x, ya, yb, yc, p, p, p, p, p, p, mod, glu_w, glu_b.reshape(1, -1), w_branch, w_out, final_g.reshape(1, d))


def kernel(x, c, ctx, c_ctx, ada_w, ada_b, norm_g, w_in, s5_lam_re, s5_lam_im, s5_log_dt, s5_b_re, s5_b_im,
           s5_c_re, s5_c_im, s5_d, s5_glu_w, s5_glu_b, na_rpb, hy_conv_w, hy_conv_b, hf_w1, hf_b1, hf_freq,
           hf_w2, hf_b2, hf_w3, hy_d, w_branch, w_out, final_g):
    bsz, t_lat, d = x.shape
    t_ctx = ctx.shape[1]
    depth = ada_w.shape[0]
    rows_n = t_lat // GRID_W
    assert bsz == 2 and d == 1024 and t_lat % (16 * FFT_N2) == 0 and t_ctx % 128 == 0

    cvec = jnp.zeros((8, d), F32).at[:bsz].set(c).at[bsz].set(c_ctx)
    mods = _modulation(cvec, ada_w, ada_b)
    cos_t, sin_t = _rope_tables(t_lat)
    n1 = 2 * t_lat // FFT_N2
    dft = _dft_consts(n1)
    tm_lat = 1024 if t_lat % 1024 == 0 else 512

    x_lat, x_ctx = x, ctx
    for l in range(depth):
        ctx_out = l < depth - 1
        half_cols = jnp.asarray(np.where(np.isin(np.arange(IN_COLS) // BRANCH_W, (1, 5, 9)) | (np.arange(IN_COLS) >= COL_GT),
                                         0.5, 1.0).astype(np.float32))
        w_bf = (w_in[l] * half_cols).astype(BF16)
        p_lat = _inproj(x_lat, mods[l], norm_g[l], w_bf, mod_row=None, tm=tm_lat)
        p_ctx = _inproj(x_ctx, mods[l], norm_g[l], w_bf, mod_row=bsz, tm=t_ctx)

        ops = _s5_operators(s5_lam_re[l], s5_lam_im[l], s5_log_dt[l], s5_b_re[l], s5_b_im[l],
                            s5_c_re[l], s5_c_im[l], s5_d[l])
        ya, ya_c = _s5_mixer(p_lat, p_ctx, ops)

        qr, kt = _rope_qk(p_lat, cos_t, sin_t)
        bias = _na_bias_tables(na_rpb[l])
        yb = _na_mixer(qr, kt, p_lat, p_ctx, bias)

        k_cm, ss = _hyena_filter_cm(t_lat, hf_w1[l], hf_b1[l], hf_freq[l], hf_w2[l], hf_b2[l], hf_w3[l])
        kf = _hyena_filter_spectrum(k_cm, ss, dft)
        hy_cm = _to_channel_major(p_lat, COL_HY, 3 * BRANCH_W)
        yc_cm = _hyena_lat(hy_cm, kf, hy_conv_w[l], hy_conv_b[l], hy_d[l], dft)
        yc = _to_token_major(yc_cm, bsz)

        wb_bf = (0.5 * w_branch[l]).astype(BF16)
        wo_bf = w_out[l].astype(BF16)
        gw_bf = (0.5 * s5_glu_w[l]).astype(BF16)
        glu_b_half = 0.5 * s5_glu_b[l]
        x_lat_new = _merge(x_lat, ya, yb, yc, p_lat, mods[l], gw_bf, glu_b_half, wb_bf, wo_bf,
                           final_g, mod_row=None, tm=512, final=not ctx_out)
        if ctx_out:
            yb_c = _ctx_attention(p_ctx)
            kc_time, ss_c = _hyena_filter_time(t_ctx, hf_w1[l], hf_b1[l], hf_freq[l], hf_w2[l], hf_b2[l], hf_w3[l])
            yc_c = _hyena_ctx(p_ctx[:, :, COL_HY:COL_HY + 1536], kc_time, ss_c, hy_conv_w[l], hy_conv_b[l], hy_d[l])
            x_ctx = _merge(x_ctx, ya_c, yb_c, yc_c, p_ctx, mods[l], gw_bf, glu_b_half, wb_bf, wo_bf,
                           final_g, mod_row=bsz, tm=t_ctx, final=False)
        x_lat = x_lat_new
    return x_lat
```

```python
import functools
import math

import numpy as np
import jax
import jax.numpy as jnp
from jax import lax
from jax.experimental import pallas as pl
from jax.experimental.pallas import tpu as pltpu

F32 = jnp.float32
BF16 = jnp.bfloat16
HIGHEST = lax.Precision.HIGHEST
HIGH = lax.Precision.HIGH

GRID_W = 64
BRANCH_W = 512
S5_P = 16
S5_N = 64
S5_G = BRANCH_W // S5_P
S5_TC = 16
HEAD_DIM = 64
NA_HEADS = BRANCH_W // HEAD_DIM
NA_KR = 8
NA_KW = 16
ROPE_HALF = 16
ROPE_THETA = 10000.0
HY_EMB = 33
HY_BANDS = (HY_EMB - 1) // 2
HY_HID = 64
HY_MIN_DECAY = math.log(1e-2) / 1.5
HY_MAX_DECAY = math.log(1e-2) / 0.3
EPS = 1e-6
FFT_N2 = 256
NEG_BIG = -1e30
LOG2E = 1.4426950408889634
VMEM_LIMIT = 52 * 1024 * 1024

COL_UA, COL_ZA, COL_Q, COL_K, COL_V, COL_ZB, COL_HY, COL_ZC, COL_GT = (
    0, 512, 1024, 1536, 2048, 2560, 3072, 4608, 5120)
IN_COLS = 8192


def _cparams(sem):
    return pltpu.CompilerParams(dimension_semantics=sem, vmem_limit_bytes=VMEM_LIMIT)


def _sigmoid(x):
    return 0.5 * jnp.tanh(0.5 * x) + 0.5


def _silu(x):
    return x * _sigmoid(x)


def _mod_kernel(c_ref, w_ref, b_ref, o_ref):
    s = _silu(c_ref[...])
    o_ref[...] = jnp.dot(s, w_ref[...], preferred_element_type=F32, precision=HIGHEST) + b_ref[...]


def _modulation(cvec, ada_w, ada_b):
    depth, d, d3 = ada_w.shape
    tn = 1024
    return pl.pallas_call(
        _mod_kernel,
        grid=(depth, d3 // tn),
        in_specs=[
            pl.BlockSpec((8, d), lambda l, j: (0, 0)),
            pl.BlockSpec((None, d, tn), lambda l, j: (l, 0, j)),
            pl.BlockSpec((None, 1, tn), lambda l, j: (l, 0, j)),
        ],
        out_specs=pl.BlockSpec((None, 8, tn), lambda l, j: (l, 0, j)),
        out_shape=jax.ShapeDtypeStruct((depth, 8, d3), F32),
        compiler_params=_cparams(("arbitrary", "arbitrary")),
        name="adaln_mod",
    )(cvec, ada_w, ada_b.reshape(depth, 1, d3))


def _inproj_kernel(x_ref, mod_ref, g_ref, w_ref, o_ref, h_ref, *, mod_row, d):
    @pl.when(pl.program_id(2) == 0)
    def _():
        x = x_ref[...]
        ms = jnp.mean(x * x, axis=-1, keepdims=True)
        y = x * lax.rsqrt(ms + EPS) * g_ref[...]
        if mod_row is None:
            m = mod_ref[pl.ds(pl.program_id(0), 1), :]
        else:
            m = mod_ref[mod_row:mod_row + 1, :]
        h_ref[...] = (y * (1.0 + m[:, d:2 * d]) + m[:, :d]).astype(BF16)

    o_ref[...] = jnp.dot(h_ref[...], w_ref[...], preferred_element_type=F32).astype(o_ref.dtype)


def _inproj(x, mod, g, w_bf, *, mod_row, tm):
    b, t, d = x.shape
    n = w_bf.shape[1]
    tn = 2048
    return pl.pallas_call(
        functools.partial(_inproj_kernel, mod_row=mod_row, d=d),
        grid=(b, t // tm, n // tn),
        in_specs=[
            pl.BlockSpec((None, tm, d), lambda bb, i, j: (bb, i, 0)),
            pl.BlockSpec((8, 3 * d), lambda bb, i, j: (0, 0)),
            pl.BlockSpec((1, d), lambda bb, i, j: (0, 0)),
            pl.BlockSpec((d, tn), lambda bb, i, j: (0, j)),
        ],
        out_specs=pl.BlockSpec((None, tm, tn), lambda bb, i, j: (bb, i, j)),
        out_shape=jax.ShapeDtypeStruct((b, t, n), BF16),
        scratch_shapes=[pltpu.VMEM((tm, d), BF16)],
        compiler_params=_cparams(("arbitrary", "arbitrary", "arbitrary")),
        name="norm_inproj",
    )(x, mod, g.reshape(1, d), w_bf)


def _s5_operators(lam_re, lam_im, log_dt, b_re, b_im, c_re, c_im, d_skip):
    tc, p, n, g = S5_TC, S5_P, S5_N, S5_G
    dt = jnp.exp(log_dt)[..., None]
    er = lam_re * dt
    ei = lam_im * dt
    k = jnp.arange(tc + 1, dtype=F32)[:, None, None, None]
    mag = jnp.exp(k * er[None])
    pw_r = mag * jnp.cos(k * ei[None])
    pw_i = mag * jnp.sin(k * ei[None])
    lb_r, lb_i = pw_r[1], pw_i[1]
    den = lam_re * lam_re + lam_im * lam_im
    q_r = ((lb_r - 1.0) * lam_re + lb_i * lam_im) / den
    q_i = (lb_i * lam_re - (lb_r - 1.0) * lam_im) / den
    bb_r = q_r[..., None] * b_re - q_i[..., None] * b_im
    bb_i = q_r[..., None] * b_im + q_i[..., None] * b_re

    ar, ai = pw_r[tc], pw_i[tc]
    a1 = jnp.concatenate([ar, ar], -1)
    a2 = jnp.concatenate([-ai, ai], -1)
    a3 = jnp.concatenate([ai, -ai], -1)

    grp = lambda a: a.transpose(1, 0, 2, 3)
    pad_t = ((0, 0), (0, 0), (0, 24 - (tc + 1)), (0, 0))
    blk = lambda rows, cols: pl.BlockSpec((None, 2, rows, cols), lambda i: (i, 0, 0, 0))
    wcat, ccat = pl.pallas_call(
        _s5_ops_kernel,
        grid=(g,),
        in_specs=[blk(24, n), blk(24, n), blk(p, n), blk(p, n), blk(p, n), blk(p, n),
                  pl.BlockSpec((None, p, 1), lambda i: (i, 0, 0))],
        out_specs=[pl.BlockSpec((None, tc * p, 768), lambda i: (i, 0, 0)),
                   pl.BlockSpec((None, 4 * n, tc * p), lambda i: (i, 0, 0))],
        out_shape=[jax.ShapeDtypeStruct((g, tc * p, 768), BF16), jax.ShapeDtypeStruct((g, 4 * n, tc * p), BF16)],
        compiler_params=_cparams(("arbitrary",)),
        name="s5_operators",
    )(jnp.pad(pw_r.transpose(2, 1, 0, 3), pad_t), jnp.pad(pw_i.transpose(2, 1, 0, 3), pad_t),
      grp(c_re), grp(c_im), grp(bb_r.transpose(0, 1, 3, 2)), grp(bb_i.transpose(0, 1, 3, 2)),
      d_skip.reshape(g, p, 1))
    return wcat, ccat, a1, a2, a3


def _s5_ops_kernel(pwr_ref, pwi_ref, cr_ref, ci_ref, br_ref, bi_ref, d_ref, w_ref, c_ref):
    tc, p, n = S5_TC, S5_P, S5_N
    nt = (((1,), (1,)), ((), ()))
    lane = lax.broadcasted_iota(jnp.int32, (p, tc * p), 1)
    row = lax.broadcasted_iota(jnp.int32, (p, tc * p), 0)
    eye_n = (lax.broadcasted_iota(jnp.int32, (n, n), 0) == lax.broadcasted_iota(jnp.int32, (n, n), 1)).astype(F32)

    def c_lam(d, taus):
        pr = jnp.concatenate([jnp.broadcast_to(pwr_ref[d, t:t + 1, :], (p, n)) for t in taus], axis=0)
        pi = jnp.concatenate([jnp.broadcast_to(pwi_ref[d, t:t + 1, :], (p, n)) for t in taus], axis=0)
        cr = jnp.concatenate([cr_ref[d]] * len(taus), axis=0)
        ci = jnp.concatenate([ci_ref[d]] * len(taus), axis=0)
        return cr * pr - ci * pi, cr * pi + ci * pr

    def lag_row(d, taus):
        clr, cli = c_lam(d, taus)
        return (lax.dot_general(br_ref[d], clr, nt, preferred_element_type=F32, precision=HIGHEST)
                - lax.dot_general(bi_ref[d], cli, nt, preferred_element_type=F32, precision=HIGHEST))

    rf = lag_row(0, list(range(tc))) + jnp.where(lane == row, d_ref[...], 0.0)
    rb = lag_row(1, list(range(tc - 1, -1, -1)))
    for ti in range(tc):
        sf = p * ti
        sb = p * (tc - 1 - ti)
        a = jnp.where(lane >= sf, pltpu.roll(rf, sf, 1) if sf else rf, 0.0)
        b = jnp.where(lane < tc * p - sb, pltpu.roll(rb, tc * p - sb, 1) if sb else rb, 0.0)
        w_ref[ti * p:(ti + 1) * p, 0:tc * p] = (a + b).astype(BF16)

    for d, col in ((0, tc * p), (1, tc * p + 4 * n)):
        for t in range(tc):
            e = tc - 1 - t if d == 0 else t
            pr = pwr_ref[d, e:e + 1, :]
            pi = pwi_ref[d, e:e + 1, :]
            sr = br_ref[d] * pr - bi_ref[d] * pi
            si = br_ref[d] * pi + bi_ref[d] * pr
            w_ref[t * p:(t + 1) * p, col:col + 4 * n] = jnp.concatenate([sr, si, si, sr], axis=1).astype(BF16)

    for d, taus in ((0, list(range(1, tc + 1))), (1, list(range(tc, 0, -1)))):
        clr, cli = c_lam(d, taus)
        c_ref[2 * n * d:2 * n * d + n, :] = lax.dot_general(
            eye_n, clr, nt, preferred_element_type=F32, precision=HIGHEST).astype(BF16)
        c_ref[2 * n * d + n:2 * n * (d + 1), :] = lax.dot_general(
            eye_n, -cli, nt, preferred_element_type=F32, precision=HIGHEST).astype(BF16)


def _s5_kernel(ul_ref, uc_ref, w_ref, c_ref, a1_ref, a2_ref, a3_ref, yl_ref, yc_ref,
               tok, xg, spf, sqf, spb, sqb, *, gb, nch, nch_lat, pitch):
    t_lat = nch_lat * S5_TC
    tok[:t_lat, :] = ul_ref[...].astype(F32)
    tok[t_lat:, :] = uc_ref[...].astype(F32)

    rt = next(r for r in (80, 40, 16, 8) if nch % r == 0)
    lane_rt = lax.broadcasted_iota(jnp.int32, (rt, 128), 1) // S5_P

    def block_transpose(a):
        for d in (4, 2, 1):
            keep = (lane_rt & d) == 0
            nxt = list(a)
            for i in range(8):
                if i & d == 0:
                    j = i + d
                    nxt[i] = jnp.where(keep, a[i], pltpu.roll(a[j], d * S5_P, 1))
                    nxt[j] = jnp.where(keep, pltpu.roll(a[i], 128 - d * S5_P, 1), a[j])
            a = nxt
        return a

    def to_compact(it, carry):
        r0 = pl.multiple_of(it * rt, rt)
        for h in range(2):
            z = [tok[pl.ds(r0 * S5_TC + 8 * h + t, rt, stride=S5_TC), :] for t in range(8)]
            b = block_transpose(z)
            for g in range(gb):
                xg[g, pl.ds(r0, rt), h * 128:(h + 1) * 128] = b[g]
        return carry

    lax.fori_loop(0, nch // rt, to_compact, 0)

    for g in range(gb):
        r = jnp.dot(xg[g].astype(BF16), w_ref[g], preferred_element_type=F32)
        xg[g] = r[:, :256]
        spf[g * pitch:g * pitch + nch, :] = r[:, 256:384]
        sqf[g * pitch:g * pitch + nch, :] = r[:, 384:512]
        spb[g * pitch:g * pitch + nch, :] = r[:, 512:640]
        sqb[g * pitch:g * pitch + nch, :] = r[:, 640:768]

    a1f, a2f, a3f = a1_ref[0], a2_ref[0], a3_ref[0]
    a1b, a2b, a3b = a1_ref[1], a2_ref[1], a3_ref[1]
    nch_ctx = nch - nch_lat

    def body(s, carry):
        pf, qf, pb, qb = carry
        cf = jnp.where(s < nch_ctx, s + nch_lat, s - nch_ctx)
        cb = nch - 1 - s
        idx_f = pl.ds(cf, gb, stride=pitch)
        idx_b = pl.ds(cb, gb, stride=pitch)
        sp = spf[idx_f, :]
        sq = sqf[idx_f, :]
        spf[idx_f, :] = pf
        pf, qf = pf * a1f + qf * a2f + sp, qf * a1f + pf * a3f + sq
        sp = spb[idx_b, :]
        sq = sqb[idx_b, :]
        spb[idx_b, :] = pb
        pb, qb = pb * a1b + qb * a2b + sp, qb * a1b + pb * a3b + sq
        return pf, qf, pb, qb

    z0 = jnp.zeros((gb, 128), F32)
    lax.fori_loop(0, nch, body, (z0, z0, z0, z0), unroll=4)

    for g in range(gb):
        hf = spf[g * pitch:g * pitch + nch, :].astype(BF16)
        hb = spb[g * pitch:g * pitch + nch, :].astype(BF16)
        cm = c_ref[g]
        y = (jnp.dot(hf, cm[:128, :], preferred_element_type=F32)
             + jnp.dot(hb, cm[128:, :], preferred_element_type=F32))
        xg[g] = xg[g] + y

    def to_tokens(it, carry):
        r0 = pl.multiple_of(it * rt, rt)
        for h in range(2):
            y = [xg[g, pl.ds(r0, rt), h * 128:(h + 1) * 128] for g in range(gb)]
            b = block_transpose(y)
            for t in range(8):
                tok[pl.ds(r0 * S5_TC + 8 * h + t, rt, stride=S5_TC), :] = b[t]
        return carry

    lax.fori_loop(0, nch // rt, to_tokens, 0)
    yl_ref[...] = tok[:t_lat, :].astype(yl_ref.dtype)
    yc_ref[...] = tok[t_lat:, :].astype(yc_ref.dtype)


def _s5_mixer(p_lat, p_ctx, ops):
    wcat, ccat, a1, a2, a3 = ops
    b, t_lat, _ = p_lat.shape
    t_ctx = p_ctx.shape[1]
    nch, nch_lat = (t_lat + t_ctx) // S5_TC, t_lat // S5_TC
    gb = 128 // S5_P
    pitch = ((nch + 7) // 8) * 8 + 8
    kern = functools.partial(_s5_kernel, gb=gb, nch=nch, nch_lat=nch_lat, pitch=pitch)
    once = pl.Buffered(1)
    return pl.pallas_call(
        kern,
        grid=(b, S5_G // gb),
        in_specs=[
            pl.BlockSpec((None, t_lat, 128), lambda bb, i: (bb, 0, COL_UA // 128 + i), pipeline_mode=once),
            pl.BlockSpec((None, t_ctx, 128), lambda bb, i: (bb, 0, COL_UA // 128 + i)),
            pl.BlockSpec((gb, 256, 768), lambda bb, i: (i, 0, 0)),
            pl.BlockSpec((gb, 256, 256), lambda bb, i: (i, 0, 0)),
            pl.BlockSpec((2, gb, 128), lambda bb, i: (0, i, 0)),
            pl.BlockSpec((2, gb, 128), lambda bb, i: (0, i, 0)),
            pl.BlockSpec((2, gb, 128), lambda bb, i: (0, i, 0)),
        ],
        out_specs=[
            pl.BlockSpec((None, t_lat, 128), lambda bb, i: (bb, 0, i), pipeline_mode=once),
            pl.BlockSpec((None, t_ctx, 128), lambda bb, i: (bb, 0, i)),
        ],
        out_shape=[jax.ShapeDtypeStruct((b, t_lat, BRANCH_W), BF16),
                   jax.ShapeDtypeStruct((b, t_ctx, BRANCH_W), BF16)],
        scratch_shapes=[pltpu.VMEM((t_lat + t_ctx, 128), F32), pltpu.VMEM((gb, nch, 256), F32)]
        + [pltpu.VMEM((gb * pitch, 128), F32) for _ in range(4)],
        compiler_params=_cparams(("arbitrary", "arbitrary")),
        name="s5_chunk_scan",
    )(p_lat, p_ctx, wcat, ccat, a1, a2, a3)


def _rope_tables(t_len):
    t = jnp.arange(t_len)
    rows = (t // GRID_W).astype(F32)
    cols = (t % GRID_W).astype(F32)
    inv = ROPE_THETA ** (-jnp.arange(ROPE_HALF, dtype=F32) / ROPE_HALF)
    lane = np.arange(128)
    dd = lane % HEAD_DIM
    fi = dd % ROPE_HALF
    use_row = jnp.asarray(dd < 32)
    sign = jnp.asarray(np.where((dd % 32) < ROPE_HALF, -1.0, 1.0).astype(np.float32))
    ang = jnp.where(use_row[None, :], rows[:, None], cols[:, None]) * inv[fi][None, :]
    return jnp.cos(ang), jnp.sin(ang) * sign[None, :]


def _rope_kernel(q_ref, k_ref, cos_ref, sin_ref, qo_ref, kt_ref, *, scale):
    cos = jnp.concatenate([cos_ref[...]] * 4, axis=1)
    sin = jnp.concatenate([sin_ref[...]] * 4, axis=1)
    lane = lax.broadcasted_iota(jnp.int32, cos.shape, 1)
    low = (lane % 32) < ROPE_HALF

    def rot(x):
        n = x.shape[1]
        partner = jnp.where(low, pltpu.roll(x, n - ROPE_HALF, 1), pltpu.roll(x, ROPE_HALF, 1))
        return x * cos + partner * sin

    qo_ref[...] = (rot(q_ref[...].astype(F32)) * scale).astype(BF16)
    kr = rot(k_ref[...].astype(F32))
    for c in range(kt_ref.shape[0]):
        kt_ref[c] = kr[c * 128:(c + 1) * 128, :].T.astype(BF16)


def _rope_qk(p_lat, cos_t, sin_t):
    b, t, _ = p_lat.shape
    tm = 1024
    return pl.pallas_call(
        functools.partial(_rope_kernel, scale=HEAD_DIM ** -0.5 * LOG2E),
        grid=(b, t // tm),
        in_specs=[
            pl.BlockSpec((None, tm, 512), lambda bb, i: (bb, i, COL_Q // 512)),
            pl.BlockSpec((None, tm, 512), lambda bb, i: (bb, i, COL_K // 512)),
            pl.BlockSpec((tm, 128), lambda bb, i: (i, 0)),
            pl.BlockSpec((tm, 128), lambda bb, i: (i, 0)),
        ],
        out_specs=[pl.BlockSpec((None, tm, 512), lambda bb, i: (bb, i, 0)),
                   pl.BlockSpec((None, tm // 128, 512, 128), lambda bb, i: (bb, i, 0, 0))],
        out_shape=[jax.ShapeDtypeStruct((b, t, BRANCH_W), BF16),
                   jax.ShapeDtypeStruct((b, t // 128, BRANCH_W, 128), BF16)],
        compiler_params=_cparams(("arbitrary", "arbitrary")),
        name="rope_qk",
    )(p_lat, p_lat, cos_t, sin_t)


_NA_VARIANTS = (((4, 5), (0, 1)), ((0, 1), (0, 0)), ((2, 3), (0, 0)), ((4, 5), (0, 0)), ((6, 7), (0, 0)))
NA_WIN = NA_KR + 2


def _na_bias_tables(rpb):
    h = rpb.shape[0]
    qc = np.arange(GRID_W)
    kc = np.arange(GRID_W)
    cs = np.clip(qc - NA_KW // 2, 0, GRID_W - NA_KW)
    valid_c = (kc[None, :] >= cs[:, None]) & (kc[None, :] < cs[:, None] + NA_KW)
    rel_c = np.clip(kc[None, :] - qc[:, None] + NA_KW - 1, 0, 2 * NA_KW - 2)
    oh_c = np.eye(2 * NA_KW - 1, dtype=np.float32)[rel_c]
    j = np.arange(NA_WIN)
    oh_r = np.zeros((len(_NA_VARIANTS), 2, NA_WIN, 2 * NA_KR - 1), np.float32)
    valid_r = np.zeros((len(_NA_VARIANTS), 2, NA_WIN), bool)
    for n, (offs, los) in enumerate(_NA_VARIANTS):
        for rr in range(2):
            valid_r[n, rr] = (j >= los[rr]) & (j < los[rr] + NA_KR)
            rel_r = np.clip(j - offs[rr] + NA_KR - 1, 0, 2 * NA_KR - 2)
            oh_r[n, rr] = np.eye(2 * NA_KR - 1, dtype=np.float32)[rel_r]
    bias = jnp.einsum("nrja,hab,ckb->nhrcjk", jnp.asarray(oh_r), rpb, jnp.asarray(oh_c), precision=HIGH) * LOG2E
    valid = valid_r[:, None, :, None, :, None] & valid_c[None, None, None, :, None, :]
    bias = jnp.where(jnp.asarray(valid), bias, NEG_BIG)
    return bias.reshape(len(_NA_VARIANTS), h // 2, 4 * GRID_W, NA_WIN * GRID_W)


def _na_kernel(q_ref, kp_ref, kc_ref, kn_ref, vp_ref, vc_ref, vn_ref, kx_ref, vx_ref, bias_ref, o_ref,
               kt3, vcat, *, ni):
    half = 4 * GRID_W
    tq = 8 * GRID_W
    kt3[0:2] = kp_ref[2:4]
    kt3[2:6] = kc_ref[...]
    kt3[6:8] = kn_ref[0:2]
    vcat[:half, :] = vp_ref[half:, :]
    vcat[half:half + tq, :] = vc_ref[...]
    vcat[half + tq:, :] = vn_ref[:half, :]
    kxt = kx_ref[...].astype(F32).T.astype(BF16)
    i = pl.program_id(1)
    head0 = lax.broadcasted_iota(jnp.int32, (2 * GRID_W, 128), 1) < HEAD_DIM
    nchunk = NA_WIN // 2

    def pair_body(a, carry):
        first = i == 0
        last = i == ni - 1
        s0 = jnp.where(first, jnp.maximum(2 * a, 4), jnp.where(last, jnp.minimum(2 * a, 4), 2 * a))
        var = jnp.where(first & (a < 2), 1 + a, jnp.where(last & (a >= 2), 1 + a, 0))
        q0 = pl.multiple_of(a * 2 * GRID_W, 2 * GRID_W)
        k0 = pl.multiple_of(s0 * GRID_W, 2 * GRID_W)
        sp = s0 // 2
        outs = []
        for hp in range(NA_HEADS // 2):
            ls = slice(hp * 128, (hp + 1) * 128)
            qp = q_ref[pl.ds(q0, 2 * GRID_W), ls]
            zero = jnp.zeros_like(qp)
            q2 = jnp.concatenate([jnp.where(head0, qp, zero), jnp.where(head0, zero, qp)], axis=0)
            kw = kt3[pl.ds(sp, nchunk), ls, :]
            s_loc = jnp.concatenate(
                [jnp.dot(q2, kw[c], preferred_element_type=F32) for c in range(nchunk)], axis=1) + bias_ref[var, hp]
            s_ctx = jnp.dot(q2, kxt[ls, :], preferred_element_type=F32)
            m = jnp.maximum(jnp.max(s_loc, axis=-1, keepdims=True), jnp.max(s_ctx, axis=-1, keepdims=True))
            p_loc = jnp.exp2(s_loc - m)
            p_ctx = jnp.exp2(s_ctx - m)
            den = jnp.sum(p_loc, axis=-1, keepdims=True) + jnp.sum(p_ctx, axis=-1, keepdims=True)
            vw = vcat[pl.ds(k0, NA_WIN * GRID_W), ls]
            o = (jnp.dot(p_loc.astype(BF16), vw, preferred_element_type=F32)
                 + jnp.dot(p_ctx.astype(BF16), vx_ref[:, ls], preferred_element_type=F32))
            o = o / den
            outs.append(jnp.where(head0, o[:2 * GRID_W], o[2 * GRID_W:]))
        o_ref[pl.ds(q0, 2 * GRID_W), :] = jnp.concatenate(outs, axis=1).astype(o_ref.dtype)
        return carry

    lax.fori_loop(0, 4, pair_body, 0)


def _na_mixer(qr, kt, p_lat, p_ctx, bias):
    b, t, _ = qr.shape
    nc = p_ctx.shape[1]
    tq = 8 * GRID_W
    ni = t // tq
    assert ni >= 2
    blk = lambda off, col: pl.BlockSpec(
        (None, tq, BRANCH_W), lambda bb, i: (bb, jnp.clip(i + off, 0, ni - 1), col))
    ktb = lambda off: pl.BlockSpec(
        (None, tq // 128, BRANCH_W, 128), lambda bb, i: (bb, jnp.clip(i + off, 0, ni - 1), 0, 0))
    cv = COL_V // BRANCH_W
    return pl.pallas_call(
        functools.partial(_na_kernel, ni=ni),
        grid=(b, ni),
        in_specs=[
            blk(0, 0), ktb(-1), ktb(0), ktb(1),
            blk(-1, cv), blk(0, cv), blk(1, cv),
            pl.BlockSpec((None, nc, BRANCH_W), lambda bb, i: (bb, 0, COL_K // BRANCH_W)),
            pl.BlockSpec((None, nc, BRANCH_W), lambda bb, i: (bb, 0, cv)),
            pl.BlockSpec(bias.shape, lambda bb, i: (0, 0, 0, 0), pipeline_mode=pl.Buffered(1)),
        ],
        out_specs=pl.BlockSpec((None, tq, BRANCH_W), lambda bb, i: (bb, i, 0)),
        out_shape=jax.ShapeDtypeStruct((b, t, BRANCH_W), BF16),
        scratch_shapes=[pltpu.VMEM((2 * tq // 128, BRANCH_W, 128), BF16), pltpu.VMEM((2 * tq, BRANCH_W), BF16)],
        compiler_params=_cparams(("arbitrary", "arbitrary")),
        name="na_attention",
    )(qr, kt, kt, kt, p_lat, p_lat, p_lat, p_ctx, p_ctx, bias)


def _ctx_attn_kernel(q_ref, k_ref, v_ref, o_ref, *, scale):
    dn = (((1,), (1,)), ((), ()))
    outs = []
    for hh in range(2):
        sl = slice(hh * HEAD_DIM, (hh + 1) * HEAD_DIM)
        qh = (q_ref[:, sl].astype(F32) * scale).astype(BF16)
        s = lax.dot_general(qh, k_ref[:, sl], dn, preferred_element_type=F32)
        m = jnp.max(s, axis=-1, keepdims=True)
        p = jnp.exp(s - m)
        den = jnp.sum(p, axis=-1, keepdims=True)
        outs.append(jnp.dot(p.astype(BF16), v_ref[:, sl], preferred_element_type=F32) / den)
    o_ref[...] = jnp.concatenate(outs, axis=1).astype(o_ref.dtype)


def _ctx_attention(p_ctx):
    b, nc, _ = p_ctx.shape
    spec = lambda col: pl.BlockSpec((None, nc, 128), lambda bb, hp: (bb, 0, col // 128 + hp))
    return pl.pallas_call(
        functools.partial(_ctx_attn_kernel, scale=HEAD_DIM ** -0.5),
        grid=(b, NA_HEADS // 2),
        in_specs=[spec(COL_Q), spec(COL_K), spec(COL_V)],
        out_specs=pl.BlockSpec((None, nc, 128), lambda bb, hp: (bb, 0, hp)),
        out_shape=jax.ShapeDtypeStruct((b, nc, BRANCH_W), BF16),
        compiler_params=_cparams(("arbitrary", "arbitrary")),
        name="ctx_attention",
    )(p_ctx, p_ctx, p_ctx)


def _filter_features(t_len):
    pos = jnp.arange(t_len, dtype=F32)
    t = pos / max(t_len - 1, 1)
    w = 2.0 * math.pi * pos / t_len
    bands = jnp.linspace(1e-4, HY_BANDS - 1, HY_BANDS, dtype=F32)
    feats = jnp.concatenate([t[:, None], jnp.cos(w[:, None] * bands), -jnp.sin(w[:, None] * bands)], axis=-1)
    feats2 = jnp.concatenate([feats, feats[:1], feats[:0:-1]], axis=0)
    return jnp.pad(feats2, ((0, 0), (0, 128 - HY_EMB)))


def _filter_kernel(f_ref, w1_ref, b1_ref, fr_ref, w2_ref, b2_ref, w3_ref, dl_ref, k_ref, ss_ref):
    f = f_ref[...]
    fr = fr_ref[...]
    h = jnp.sin(fr * (jnp.dot(f, w1_ref[...], preferred_element_type=F32, precision=HIGHEST) + b1_ref[...]))
    h = jnp.sin(fr * (jnp.dot(h, w2_ref[...], preferred_element_type=F32, precision=HIGHEST) + b2_ref[...]))
    k = jnp.dot(h, w3_ref[...], preferred_element_type=F32, precision=HIGHEST)
    k = k * jnp.exp(-f[:, 0:1] * dl_ref[...])
    k_ref[...] = k

    @pl.when(pl.program_id(0) == 0)
    def _():
        ss_ref[...] = jnp.zeros_like(ss_ref)

    ss_ref[...] += jnp.sum(k * k, axis=0, keepdims=True)


def _hyena_filter_time(t_len, w1, b1, freq, w2, b2, w3):
    n = 2 * t_len
    tm = min(1024, t_len)
    feats = _filter_features(t_len)
    w1p = jnp.pad(w1, ((0, 128 - HY_EMB), (0, 0)))
    deltas = jnp.abs(jnp.linspace(HY_MIN_DECAY, HY_MAX_DECAY, BRANCH_W, dtype=F32))
    dl = jnp.concatenate([deltas, deltas]).reshape(1, 2 * BRANCH_W)
    half = t_len // tm
    nco = 2 * BRANCH_W
    return pl.pallas_call(
        _filter_kernel,
        grid=(n // tm,),
        in_specs=[
            pl.BlockSpec((tm, 128), lambda i: (i, 0)),
            pl.BlockSpec((128, HY_HID), lambda i: (0, 0)),
            pl.BlockSpec((1, HY_HID), lambda i: (0, 0)),
            pl.BlockSpec((1, HY_HID), lambda i: (0, 0)),
            pl.BlockSpec((HY_HID, HY_HID), lambda i: (0, 0)),
            pl.BlockSpec((1, HY_HID), lambda i: (0, 0)),
            pl.BlockSpec((HY_HID, nco), lambda i: (0, i // half)),
            pl.BlockSpec((1, nco), lambda i: (0, 0)),
        ],
        out_specs=[pl.BlockSpec((tm, nco), lambda i: (i, 0)), pl.BlockSpec((1, nco), lambda i: (0, 0))],
        out_shape=[jax.ShapeDtypeStruct((n, nco), F32), jax.ShapeDtypeStruct((1, nco), F32)],
        compiler_params=_cparams(("arbitrary",)),
        name="hyena_filter_ffn_ctx",
    )(feats, w1p, b1.reshape(1, -1), freq.reshape(1, -1), w2, b2.reshape(1, -1), w3, dl)


def _filter_cm_kernel(f_ref, t_ref, w1_ref, b1_ref, fr_ref, w2_ref, b2_ref, w3_ref, dl_ref, k_ref, ss_ref):
    fr = fr_ref[...]
    h = jnp.sin(fr * (jnp.dot(w1_ref[...], f_ref[...], preferred_element_type=F32, precision=HIGHEST) + b1_ref[...]))
    h = jnp.sin(fr * (jnp.dot(w2_ref[...], h, preferred_element_type=F32, precision=HIGHEST) + b2_ref[...]))
    k = lax.dot_general(h.astype(BF16), w3_ref[...].astype(BF16), (((0,), (0,)), ((), ())), preferred_element_type=F32)
    k = k * jnp.exp(-t_ref[...] * dl_ref[...])
    rows = k.shape[0] // FFT_N2
    for cb in range(k.shape[1] // 128):
        piece = k[:, cb * 128:(cb + 1) * 128].reshape(rows, FFT_N2, 128)
        k_ref[cb * 128:(cb + 1) * 128] = jnp.swapaxes(jnp.swapaxes(piece, 1, 2), 0, 1)

    @pl.when(pl.program_id(0) == 0)
    def _():
        ss_ref[...] = jnp.zeros_like(ss_ref)

    ss_ref[...] += jnp.sum(k * k, axis=0, keepdims=True)


def _hyena_filter_cm(t_len, w1, b1, freq, w2, b2, w3):
    n = 2 * t_len
    tm = 2048
    assert t_len % tm == 0
    slot = jnp.arange(n)
    pos = jnp.where(slot < t_len, slot, jnp.where(slot == t_len, 0, n - slot)).astype(F32)
    t = pos / max(t_len - 1, 1)
    w = 2.0 * math.pi * pos / t_len
    bands = jnp.linspace(1e-4, HY_BANDS - 1, HY_BANDS, dtype=F32)
    feats = jnp.concatenate([t[None, :], jnp.cos(w[None, :] * bands[:, None]), -jnp.sin(w[None, :] * bands[:, None])],
                            axis=0)
    feats = jnp.pad(feats, ((0, 128 - HY_EMB), (0, 0)))
    w1t = jnp.pad(w1, ((0, 128 - HY_EMB), (0, 0))).T
    deltas = jnp.abs(jnp.linspace(HY_MIN_DECAY, HY_MAX_DECAY, BRANCH_W, dtype=F32))
    nco = 2 * BRANCH_W
    dl = jnp.concatenate([deltas, deltas]).reshape(1, nco)
    half = t_len // tm
    col = lambda a: a.reshape(-1, 1)
    const = lambda shape: pl.BlockSpec(shape, lambda i: (0, 0))
    return pl.pallas_call(
        _filter_cm_kernel,
        grid=(n // tm,),
        in_specs=[
            pl.BlockSpec((128, tm), lambda i: (0, i)),
            pl.BlockSpec((tm, 1), lambda i: (i, 0)),
            const((HY_HID, 128)), const((HY_HID, 1)), const((HY_HID, 1)),
            const((HY_HID, HY_HID)), const((HY_HID, 1)),
            pl.BlockSpec((HY_HID, nco), lambda i: (0, i // half)),
            const((1, nco)),
        ],
        out_specs=[pl.BlockSpec((nco, tm // FFT_N2, FFT_N2), lambda i: (0, i, 0)), const((1, nco))],
        out_shape=[jax.ShapeDtypeStruct((nco, n // FFT_N2, FFT_N2), F32), jax.ShapeDtypeStruct((1, nco), F32)],
        compiler_params=_cparams(("arbitrary",)),
        name="hyena_filter_ffn",
    )(feats, col(t), w1t, col(b1), col(freq), w2.T, col(b2), w3, dl)


def _dft_consts(n1):
    n2 = FFT_N2
    n = n1 * n2
    nh = n1 // 2
    a1 = -2.0 * np.pi * np.outer(np.arange(n1), np.arange(n1)) / n1
    f1r, f1i = np.cos(a1), np.sin(a1)
    a2 = -2.0 * np.pi * np.outer(np.arange(n2), np.arange(n2)) / n2
    f2r, f2i = np.cos(a2), np.sin(a2)
    at = -2.0 * np.pi * np.outer(np.arange(n1), np.arange(n2)) / n
    bf = lambda a: jnp.asarray(a.astype(np.float32)).astype(BF16)
    return dict(
        s1_data=bf(np.block([[f1r[:, :nh], -f1i[:, :nh]], [f1i[:, :nh], f1r[:, :nh]]])),
        s1_real=bf(np.concatenate([f1r, f1i], axis=0)),
        s6=bf(np.block([[f1r[:nh], f1i[:nh]], [-f1i[:nh], f1r[:nh]]]) / n),
        fa=bf(np.concatenate([f2r, f2i], axis=1)), fb=bf(np.concatenate([-f2i, f2r], axis=1)),
        ia=bf(np.concatenate([f2r, -f2i], axis=1)), ib=bf(np.concatenate([f2i, f2r], axis=1)),
        tw_r=jnp.asarray(np.cos(at).astype(np.float32)), tw_i=jnp.asarray(np.sin(at).astype(np.float32)),
    )


def _lane_dft(ar_ref, ai_ref, fa_ref, fb_ref):
    cb, n1, n2 = ar_ref.shape
    ar = ar_ref[...].reshape(cb * n1, n2).astype(BF16)
    ai = ai_ref[...].reshape(cb * n1, n2).astype(BF16)
    x = (jnp.dot(ar, fa_ref[...], preferred_element_type=F32)
         + jnp.dot(ai, fb_ref[...], preferred_element_type=F32))
    return x[:, :n2], x[:, n2:]


def _kf_kernel(k_ref, ss_ref, s1_ref, twr_ref, twi_ref, fa_ref, fb_ref, o_ref, ar_ref, ai_ref, *, cb, n1):
    i = pl.program_id(0)
    row = lax.broadcasted_iota(jnp.int32, (n1, FFT_N2), 0)
    lane = lax.broadcasted_iota(jnp.int32, (n1, FFT_N2), 1)
    keep = jnp.logical_not((row == n1 // 2) & (lane == 0))
    twr, twi = twr_ref[...], twi_ref[...]
    for c in range(cb):
        scale = lax.rsqrt(jnp.full((n1, FFT_N2), ss_ref[i * cb + c], F32) + EPS)
        k = jnp.where(keep, k_ref[c] * scale, 0.0)
        a = jnp.dot(s1_ref[...], k.astype(BF16), preferred_element_type=F32)
        a_r, a_i = a[:n1], a[n1:]
        ar_ref[c] = a_r * twr - a_i * twi
        ai_ref[c] = a_r * twi + a_i * twr
    xr, xi = _lane_dft(ar_ref, ai_ref, fa_ref, fb_ref)
    o_ref[:, :n1, :] = xr.reshape(cb, n1, FFT_N2)
    o_ref[:, n1:, :] = xi.reshape(cb, n1, FFT_N2)


def _hyena_filter_spectrum(kt, ss, dft):
    nco, n1, _ = kt.shape
    cb = 8
    const = lambda shape: pl.BlockSpec(shape, lambda i: (0,) * len(shape))
    return pl.pallas_call(
        functools.partial(_kf_kernel, cb=cb, n1=n1),
        grid=(nco // cb,),
        in_specs=[
            pl.BlockSpec((cb, n1, FFT_N2), lambda i: (i, 0, 0)),
            pl.BlockSpec(memory_space=pltpu.SMEM),
            const((2 * n1, n1)), const((n1, FFT_N2)), const((n1, FFT_N2)),
            const((FFT_N2, 2 * FFT_N2)), const((FFT_N2, 2 * FFT_N2)),
        ],
        out_specs=pl.BlockSpec((cb, 2 * n1, FFT_N2), lambda i: (i, 0, 0)),
        out_shape=jax.ShapeDtypeStruct((nco, 2 * n1, FFT_N2), F32),
        scratch_shapes=[pltpu.VMEM((cb, n1, FFT_N2), F32), pltpu.VMEM((cb, n1, FFT_N2), F32)],
        compiler_params=_cparams(("arbitrary",)),
        name="hyena_filter_fft",
    )(kt, ss.reshape(nco), dft["s1_real"], dft["tw_r"], dft["tw_i"], dft["fa"], dft["fb"])


def _tok2cm_kernel(x_ref, o_ref):
    rows = x_ref.shape[0] // FFT_N2
    for cb in range(x_ref.shape[1] // 128):
        x3 = x_ref[:, cb * 128:(cb + 1) * 128].reshape(rows, FFT_N2, 128)
        o_ref[cb * 128:(cb + 1) * 128] = jnp.swapaxes(jnp.swapaxes(x3, 1, 2), 0, 1)


def _to_channel_major(p, col0, ncols):
    b, t, _ = p.shape
    tt = 16 * FFT_N2
    nt = t // tt
    cw = 256
    return pl.pallas_call(
        _tok2cm_kernel,
        grid=(ncols // cw, b, nt),
        in_specs=[pl.BlockSpec((None, tt, cw), lambda c, bb, i: (bb, i, col0 // cw + c))],
        out_specs=pl.BlockSpec((cw, tt // FFT_N2, FFT_N2), lambda c, bb, i: (c, bb * nt + i, 0)),
        out_shape=jax.ShapeDtypeStruct((ncols, b * t // FFT_N2, FFT_N2), p.dtype),
        compiler_params=_cparams(("arbitrary", "arbitrary", "arbitrary")),
        name="to_channel_major",
    )(p)


def _cm2tok_kernel(x_ref, o_ref):
    x3 = jnp.swapaxes(jnp.swapaxes(x_ref[...], 0, 1), 1, 2)
    o_ref[...] = x3.reshape(o_ref.shape)


def _to_token_major(y_cm, b):
    c, rows, _ = y_cm.shape
    t = rows * FFT_N2 // b
    tt = 16 * FFT_N2
    nt = t // tt
    return pl.pallas_call(
        _cm2tok_kernel,
        grid=(c // 128, b, nt),
        in_specs=[pl.BlockSpec((128, tt // FFT_N2, FFT_N2), lambda cc, bb, i: (cc, bb * nt + i, 0))],
        out_specs=pl.BlockSpec((None, tt, 128), lambda cc, bb, i: (bb, i, cc)),
        out_shape=jax.ShapeDtypeStruct((b, t, c), y_cm.dtype),
        compiler_params=_cparams(("arbitrary", "arbitrary", "arbitrary")),
        name="to_token_major",
    )(y_cm)


def _hyena_kernel(v_ref, x1_ref, x2_ref, kf1_ref, kf2_ref, cw_ref, cbias_ref, dd_ref,
                  s1_ref, s6_ref, twr_ref, twi_ref, fa_ref, fb_ref, ia_ref, ib_ref,
                  o_ref, vs, x1s, x2s, ar_ref, ai_ref, *, cb, n1, nch):
    i = pl.program_id(0)
    nh = n1 // 2
    rows = 2 * nh
    row = lax.broadcasted_iota(jnp.int32, (rows, FFT_N2), 0)
    lane = lax.broadcasted_iota(jnp.int32, (rows, FFT_N2), 1)
    first = lane == 0
    last = lane == FFT_N2 - 1
    seq_start = first & ((row % nh) == 0)
    seq_end = last & ((row % nh) == nh - 1)
    twr, twi = twr_ref[...], twi_ref[...]

    def short_conv(x, ch):
        prev = pltpu.roll(x, 1, 1)
        prev = jnp.where(first, pltpu.roll(prev, 1, 0), prev)
        prev = jnp.where(seq_start, 0.0, prev)
        nxt = pltpu.roll(x, FFT_N2 - 1, 1)
        nxt = jnp.where(last, pltpu.roll(nxt, rows - 1, 0), nxt)
        nxt = jnp.where(seq_end, 0.0, nxt)
        return cw_ref[ch] * prev + cw_ref[nch + ch] * x + cw_ref[2 * nch + ch] * nxt + cbias_ref[ch]

    def stage1(z, c):
        a = jnp.dot(s1_ref[...], z.astype(BF16), preferred_element_type=F32)
        a_r, a_i = a[:n1], a[n1:]
        ar_ref[c] = a_r * twr - a_i * twi
        ai_ref[c] = a_r * twi + a_i * twr

    def spectral(kf_ref):
        xr, xi = _lane_dft(ar_ref, ai_ref, fa_ref, fb_ref)
        kr = kf_ref[:, :n1, :].reshape(cb * n1, FFT_N2)
        ki = kf_ref[:, n1:, :].reshape(cb * n1, FFT_N2)
        yr = (xr * kr - xi * ki).astype(BF16)
        yi = (xr * ki + xi * kr).astype(BF16)
        bm = (jnp.dot(yr, ia_ref[...], preferred_element_type=F32)
              + jnp.dot(yi, ib_ref[...], preferred_element_type=F32))
        br = bm[:, :FFT_N2].reshape(cb, n1, FFT_N2)
        bi = bm[:, FFT_N2:].reshape(cb, n1, FFT_N2)
        ar_ref[...] = br * twr[None] + bi * twi[None]
        ai_ref[...] = bi * twr[None] - br * twi[None]

    def stage6(c):
        bcat = jnp.concatenate([ar_ref[c], ai_ref[c]], axis=0).astype(BF16)
        return jnp.dot(s6_ref[...], bcat, preferred_element_type=F32)

    nbr = BRANCH_W
    for c in range(cb):
        ch = i * cb + c
        v = short_conv(v_ref[c].astype(F32), ch)
        vs[c] = v
        x1s[c] = short_conv(x1_ref[c].astype(F32), nbr + ch)
        x2s[c] = short_conv(x2_ref[c].astype(F32), 2 * nbr + ch)
        stage1(v, c)
    spectral(kf1_ref)
    for c in range(cb):
        ch = i * cb + c
        z = x1s[c] * (stage6(c) + vs[c] * dd_ref[ch])
        vs[c] = z
        stage1(z, c)
    spectral(kf2_ref)
    for c in range(cb):
        ch = i * cb + c
        o_ref[c] = (x2s[c] * (stage6(c) + vs[c] * dd_ref[nbr + ch])).astype(o_ref.dtype)


def _hyena_lat(hy_cm, kf, conv_w, conv_b, hy_d, dft):
    nch, rows, _ = hy_cm.shape
    n1 = rows
    cb = 8
    nblk = BRANCH_W // cb
    const = lambda shape: pl.BlockSpec(shape, lambda i: (0,) * len(shape))
    smem = pl.BlockSpec(memory_space=pltpu.SMEM)
    data = lambda sec: pl.BlockSpec((cb, rows, FFT_N2), lambda i: (sec * nblk + i, 0, 0))
    kfs = lambda o: pl.BlockSpec((cb, 2 * n1, FFT_N2), lambda i: (o * nblk + i, 0, 0))
    return pl.pallas_call(
        functools.partial(_hyena_kernel, cb=cb, n1=n1, nch=nch),
        grid=(nblk,),
        in_specs=[
            data(0), data(1), data(2), kfs(0), kfs(1), smem, smem, smem,
            const((2 * n1, n1)), const((n1, 2 * n1)), const((n1, FFT_N2)), const((n1, FFT_N2)),
            const((FFT_N2, 2 * FFT_N2)), const((FFT_N2, 2 * FFT_N2)),
            const((FFT_N2, 2 * FFT_N2)), const((FFT_N2, 2 * FFT_N2)),
        ],
        out_specs=pl.BlockSpec((cb, rows, FFT_N2), lambda i: (i, 0, 0)),
        out_shape=jax.ShapeDtypeStruct((BRANCH_W, rows, FFT_N2), BF16),
        scratch_shapes=[pltpu.VMEM((cb, rows, FFT_N2), F32) for _ in range(3)]
        + [pltpu.VMEM((cb, n1, FFT_N2), F32) for _ in range(2)],
        compiler_params=_cparams(("arbitrary",)),
        name="hyena_fftconv",
    )(hy_cm, hy_cm, hy_cm, kf, kf, conv_w.reshape(-1), conv_b, hy_d.reshape(-1),
      dft["s1_data"], dft["s6"], dft["tw_r"], dft["tw_i"], dft["fa"], dft["fb"], dft["ia"], dft["ib"])


def _hyena_ctx_kernel(hy_ref, k_ref, ss_ref, cw_ref, cbias_ref, dd_ref, fwd_a_ref, fwd_b_ref,
                      inv_a_ref, inv_b_ref, o_ref, *, t_len):
    nbr = BRANCH_W
    lane = lax.broadcasted_iota(jnp.int32, (nbr, t_len), 1)

    def short_conv(x, sec):
        sl = slice(sec * nbr, (sec + 1) * nbr)
        prev = jnp.where(lane == 0, 0.0, pltpu.roll(x, 1, 1))
        nxt = jnp.where(lane == t_len - 1, 0.0, pltpu.roll(x, t_len - 1, 1))
        return cw_ref[0, sl, :] * prev + cw_ref[1, sl, :] * x + cw_ref[2, sl, :] * nxt + cbias_ref[sl, :]

    def sec(b, s):
        return short_conv(hy_ref[b, s * nbr:(s + 1) * nbr, :].astype(F32), s)

    klane = lax.broadcasted_iota(jnp.int32, k_ref.shape, 1)
    kk = jnp.where(klane == t_len, 0.0, k_ref[...] * lax.rsqrt(ss_ref[...] + EPS))
    kf = jnp.dot(kk.astype(BF16), fwd_a_ref[...], preferred_element_type=F32)
    n = 2 * t_len

    def conv(z0, z1, order):
        x = (jnp.dot(z0.astype(BF16), fwd_a_ref[:t_len, :], preferred_element_type=F32)
             + jnp.dot(z1.astype(BF16), fwd_b_ref[:t_len, :], preferred_element_type=F32))
        xr, xi = x[:, :n], x[:, n:]
        kr = kf[order * nbr:(order + 1) * nbr, :n]
        ki = kf[order * nbr:(order + 1) * nbr, n:]
        yr = (xr * kr - xi * ki).astype(BF16)
        yi = (xr * ki + xi * kr).astype(BF16)
        y = (jnp.dot(yr, inv_a_ref[...], preferred_element_type=F32)
             + jnp.dot(yi, inv_b_ref[...], preferred_element_type=F32))
        return y[:, :t_len], y[:, t_len:]

    v0, v1 = sec(0, 0), sec(1, 0)
    y0, y1 = conv(v0, v1, 0)
    d1 = dd_ref[:nbr, :]
    d2 = dd_ref[nbr:, :]
    z0 = sec(0, 1) * (y0 + v0 * d1)
    z1 = sec(1, 1) * (y1 + v1 * d1)
    y0, y1 = conv(z0, z1, 1)
    o_ref[0] = (sec(0, 2) * (y0 + z0 * d2)).astype(o_ref.dtype)
    o_ref[1] = (sec(1, 2) * (y1 + z1 * d2)).astype(o_ref.dtype)


def _hyena_ctx(hy_ctx, k_time, ss, conv_w, conv_b, hy_d):
    b, t_len, nch = hy_ctx.shape
    n = 2 * t_len
    ang = -2.0 * np.pi * np.outer(np.arange(n), np.arange(n)) / n
    fr, fi = np.cos(ang), np.sin(ang)
    bf = lambda a: jnp.asarray(a.astype(np.float32)).astype(BF16)
    fwd_a = bf(np.concatenate([fr, fi], axis=1))
    fwd_b = bf(np.concatenate([-fi, fr], axis=1))
    inv_a = bf(np.concatenate([fr[:, :t_len], -fi[:, :t_len]], axis=1) / n)
    inv_b = bf(np.concatenate([fi[:, :t_len], fr[:, :t_len]], axis=1) / n)
    out = pl.pallas_call(
        functools.partial(_hyena_ctx_kernel, t_len=t_len),
        out_shape=jax.ShapeDtypeStruct((b, BRANCH_W, t_len), BF16),
        compiler_params=pltpu.CompilerParams(vmem_limit_bytes=VMEM_LIMIT),
        name="hyena_ctx",
    )(hy_ctx.transpose(0, 2, 1), k_time.T, ss.reshape(-1, 1), conv_w.reshape(3, nch, 1),
      conv_b.reshape(nch, 1), hy_d.reshape(-1, 1), fwd_a, fwd_b, inv_a, inv_b)
    return out.transpose(0, 2, 1)


def _merge_kernel(x_ref, ya_ref, yb_ref, yc_ref, za_ref, zb_ref, zc_ref, g0_ref, g1_ref, g2_ref, mod_ref,
                  gw_ref, gb_ref, wb_ref, wo_ref, fg_ref, o_ref, *, mod_row, d, final):
    ya = jax.nn.gelu(ya_ref[...].astype(F32)).astype(BF16)
    glu = jnp.dot(ya, gw_ref[...], preferred_element_type=F32) + gb_ref[...]
    y_a = glu[:, :BRANCH_W] * (1.0 + jnp.tanh(glu[:, BRANCH_W:]))

    def branch(i, y, z_ref, g_ref):
        zh = z_ref[...]
        u = (zh + zh * jnp.tanh(zh)) * y
        return (1.0 + jnp.tanh(g_ref[...].astype(F32))) * jnp.dot(u, wb_ref[i], preferred_element_type=F32)

    acc = branch(0, y_a.astype(BF16), za_ref, g0_ref)
    acc = acc + branch(1, yb_ref[...], zb_ref, g1_ref)
    acc = acc + branch(2, yc_ref[...], zc_ref, g2_ref)
    out = jnp.dot(acc.astype(BF16), wo_ref[...], preferred_element_type=F32)
    if mod_row is None:
        m = mod_ref[pl.ds(pl.program_id(0), 1), :]
    else:
        m = mod_ref[mod_row:mod_row + 1, :]
    xn = x_ref[...] + m[:, 2 * d:] * out
    if final:
        ms = jnp.mean(xn * xn, axis=-1, keepdims=True)
        xn = xn * lax.rsqrt(ms + EPS) * fg_ref[...]
    o_ref[...] = xn


def _merge(x, ya, yb, yc, p, mod, glu_w, glu_b, w_branch, w_out, final_g, *, mod_row, tm, final):
    b, t, d = x.shape
    tok = lambda w, col: pl.BlockSpec((None, tm, w), lambda bb, i: (bb, i, col // w))
    const = lambda shape: pl.BlockSpec(shape, lambda bb, i: (0,) * len(shape))
    return pl.pallas_call(
        functools.partial(_merge_kernel, mod_row=mod_row, d=d, final=final),
        grid=(b, t // tm),
        in_specs=[
            tok(d, 0), tok(512, 0), tok(512, 0), tok(512, 0),
            tok(512, COL_ZA), tok(512, COL_ZB), tok(512, COL_ZC),
            tok(1024, COL_GT), tok(1024, COL_GT + 1024), tok(1024, COL_GT + 2048),
            const((8, 3 * d)), const((BRANCH_W, 2 * BRANCH_W)), const((1, 2 * BRANCH_W)),
            const((3, BRANCH_W, d)), const((d, d)), const((1, d)),
        ],
        out_specs=tok(d, 0),
        out_shape=jax.ShapeDtypeStruct((b, t, d), F32),
        compiler_params=_cparams(("arbitrary", "arbitrary")),
        name="merge_out",
    )(x, ya, yb, yc, p, p, p, p, p, p, mod, glu_w, glu_b.reshape(1, -1), w_branch, w_out, final_g.reshape(1, d))


def kernel(x, c, ctx, c_ctx, ada_w, ada_b, norm_g, w_in, s5_lam_re, s5_lam_im, s5_log_dt, s5_b_re, s5_b_im,
           s5_c_re, s5_c_im, s5_d, s5_glu_w, s5_glu_b, na_rpb, hy_conv_w, hy_conv_b, hf_w1, hf_b1, hf_freq,
           hf_w2, hf_b2, hf_w3, hy_d, w_branch, w_out, final_g):
    bsz, t_lat, d = x.shape
    t_ctx = ctx.shape[1]
    depth = ada_w.shape[0]
    rows_n = t_lat // GRID_W
    assert bsz == 2 and d == 1024 and t_lat % (16 * FFT_N2) == 0 and t_ctx % 128 == 0

    cvec = jnp.zeros((8, d), F32).at[:bsz].set(c).at[bsz].set(c_ctx)
    mods = _modulation(cvec, ada_w, ada_b)
    cos_t, sin_t = _rope_tables(t_lat)
    n1 = 2 * t_lat // FFT_N2
    dft = _dft_consts(n1)
    tm_lat = 1024 if t_lat % 1024 == 0 else 512

    x_lat, x_ctx = x, ctx
    for l in range(depth):
        ctx_out = l < depth - 1
        half_cols = jnp.asarray(np.where(np.isin(np.arange(IN_COLS) // BRANCH_W, (1, 5, 9)) | (np.arange(IN_COLS) >= COL_GT),
                                         0.5, 1.0).astype(np.float32))
        w_bf = (w_in[l] * half_cols).astype(BF16)
        p_lat = _inproj(x_lat, mods[l], norm_g[l], w_bf, mod_row=None, tm=tm_lat)
        p_ctx = _inproj(x_ctx, mods[l], norm_g[l], w_bf, mod_row=bsz, tm=t_ctx)

        ops = _s5_operators(s5_lam_re[l], s5_lam_im[l], s5_log_dt[l], s5_b_re[l], s5_b_im[l],
                            s5_c_re[l], s5_c_im[l], s5_d[l])
        ya, ya_c = _s5_mixer(p_lat, p_ctx, ops)

        qr, kt = _rope_qk(p_lat, cos_t, sin_t)
        bias = _na_bias_tables(na_rpb[l])
        yb = _na_mixer(qr, kt, p_lat, p_ctx, bias)

        k_cm, ss = _hyena_filter_cm(t_lat, hf_w1[l], hf_b1[l], hf_freq[l], hf_w2[l], hf_b2[l], hf_w3[l])
        kf = _hyena_filter_spectrum(k_cm, ss, dft)
        hy_cm = _to_channel_major(p_lat, COL_HY, 3 * BRANCH_W)
        yc_cm = _hyena_lat(hy_cm, kf, hy_conv_w[l], hy_conv_b[l], hy_d[l], dft)
        yc = _to_token_major(yc_cm, bsz)

        wb_bf = (0.5 * w_branch[l]).astype(BF16)
        wo_bf = w_out[l].astype(BF16)
        gw_bf = (0.5 * s5_glu_w[l]).astype(BF16)
        glu_b_half = 0.5 * s5_glu_b[l]
        x_lat_new = _merge(x_lat, ya, yb, yc, p_lat, mods[l], gw_bf, glu_b_half, wb_bf, wo_bf,
                           final_g, mod_row=None, tm=512, final=not ctx_out)
        if ctx_out:
            yb_c = _ctx_attention(p_ctx)
            kc_time, ss_c = _hyena_filter_time(t_ctx, hf_w1[l], hf_b1[l], hf_freq[l], hf_w2[l], hf_b2[l], hf_w3[l])
            yc_c = _hyena_ctx(p_ctx[:, :, COL_HY:COL_HY + 1536], kc_time, ss_c, hy_conv_w[l], hy_conv_b[l], hy_d[l])
            x_ctx = _merge(x_ctx, ya_c, yb_c, yc_c, p_ctx, mods[l], gw_bf, glu_b_half, wb_bf, wo_bf,
                           final_g, mod_row=bsz, tm=t_ctx, final=False)
        x_lat = x_lat_new
    return x_lat
```

```python
import functools
import math

import numpy as np
import jax
import jax.numpy as jnp
from jax import lax
from jax.experimental import pallas as pl
from jax.experimental.pallas import tpu as pltpu

F32 = jnp.float32
BF16 = jnp.bfloat16
HIGHEST = lax.Precision.HIGHEST
HIGH = lax.Precision.HIGH

GRID_W = 64
BRANCH_W = 512
S5_P = 16
S5_N = 64
S5_G = BRANCH_W // S5_P
S5_TC = 16
HEAD_DIM = 64
NA_HEADS = BRANCH_W // HEAD_DIM
NA_KR = 8
NA_KW = 16
ROPE_HALF = 16
ROPE_THETA = 10000.0
HY_EMB = 33
HY_BANDS = (HY_EMB - 1) // 2
HY_HID = 64
HY_MIN_DECAY = math.log(1e-2) / 1.5
HY_MAX_DECAY = math.log(1e-2) / 0.3
EPS = 1e-6
FFT_N2 = 256
NEG_BIG = -1e30
LOG2E = 1.4426950408889634
VMEM_LIMIT = 52 * 1024 * 1024

COL_UA, COL_ZA, COL_Q, COL_K, COL_V, COL_ZB, COL_HY, COL_ZC, COL_GT = (
    0, 512, 1024, 1536, 2048, 2560, 3072, 4608, 5120)
IN_COLS = 8192


def _cparams(sem):
    return pltpu.CompilerParams(dimension_semantics=sem, vmem_limit_bytes=VMEM_LIMIT)


def _sigmoid(x):
    return 0.5 * jnp.tanh(0.5 * x) + 0.5


def _silu(x):
    return x * _sigmoid(x)


def _mod_kernel(c_ref, w_ref, b_ref, o_ref):
    s = _silu(c_ref[...])
    o_ref[...] = jnp.dot(s, w_ref[...], preferred_element_type=F32, precision=HIGHEST) + b_ref[...]


def _modulation(cvec, ada_w, ada_b):
    depth, d, d3 = ada_w.shape
    tn = 1024
    return pl.pallas_call(
        _mod_kernel,
        grid=(depth, d3 // tn),
        in_specs=[
            pl.BlockSpec((8, d), lambda l, j: (0, 0)),
            pl.BlockSpec((None, d, tn), lambda l, j: (l, 0, j)),
            pl.BlockSpec((None, 1, tn), lambda l, j: (l, 0, j)),
        ],
        out_specs=pl.BlockSpec((None, 8, tn), lambda l, j: (l, 0, j)),
        out_shape=jax.ShapeDtypeStruct((depth, 8, d3), F32),
        compiler_params=_cparams(("arbitrary", "arbitrary")),
        name="adaln_mod",
    )(cvec, ada_w, ada_b.reshape(depth, 1, d3))


def _inproj_kernel(x_ref, mod_ref, g_ref, w_ref, o_ref, h_ref, *, mod_row, d):
    @pl.when(pl.program_id(2) == 0)
    def _():
        x = x_ref[...]
        ms = jnp.mean(x * x, axis=-1, keepdims=True)
        y = x * lax.rsqrt(ms + EPS) * g_ref[...]
        if mod_row is None:
            m = mod_ref[pl.ds(pl.program_id(0), 1), :]
        else:
            m = mod_ref[mod_row:mod_row + 1, :]
        h_ref[...] = (y * (1.0 + m[:, d:2 * d]) + m[:, :d]).astype(BF16)

    o_ref[...] = jnp.dot(h_ref[...], w_ref[...], preferred_element_type=F32).astype(o_ref.dtype)


def _inproj(x, mod, g, w_bf, *, mod_row, tm):
    b, t, d = x.shape
    n = w_bf.shape[1]
    tn = 2048
    return pl.pallas_call(
        functools.partial(_inproj_kernel, mod_row=mod_row, d=d),
        grid=(b, t // tm, n // tn),
        in_specs=[
            pl.BlockSpec((None, tm, d), lambda bb, i, j: (bb, i, 0)),
            pl.BlockSpec((8, 3 * d), lambda bb, i, j: (0, 0)),
            pl.BlockSpec((1, d), lambda bb, i, j: (0, 0)),
            pl.BlockSpec((d, tn), lambda bb, i, j: (0, j)),
        ],
        out_specs=pl.BlockSpec((None, tm, tn), lambda bb, i, j: (bb, i, j)),
        out_shape=jax.ShapeDtypeStruct((b, t, n), BF16),
        scratch_shapes=[pltpu.VMEM((tm, d), BF16)],
        compiler_params=_cparams(("arbitrary", "arbitrary", "arbitrary")),
        name="norm_inproj",
    )(x, mod, g.reshape(1, d), w_bf)


def _s5_operators(lam_re, lam_im, log_dt, b_re, b_im, c_re, c_im, d_skip):
    tc, p, n, g = S5_TC, S5_P, S5_N, S5_G
    dt = jnp.exp(log_dt)[..., None]
    er = lam_re * dt
    ei = lam_im * dt
    k = jnp.arange(tc + 1, dtype=F32)[:, None, None, None]
    mag = jnp.exp(k * er[None])
    pw_r = mag * jnp.cos(k * ei[None])
    pw_i = mag * jnp.sin(k * ei[None])
    lb_r, lb_i = pw_r[1], pw_i[1]
    den = lam_re * lam_re + lam_im * lam_im
    q_r = ((lb_r - 1.0) * lam_re + lb_i * lam_im) / den
    q_i = (lb_i * lam_re - (lb_r - 1.0) * lam_im) / den
    bb_r = q_r[..., None] * b_re - q_i[..., None] * b_im
    bb_i = q_r[..., None] * b_im + q_i[..., None] * b_re

    ar, ai = pw_r[tc], pw_i[tc]
    a1 = jnp.concatenate([ar, ar], -1)
    a2 = jnp.concatenate([-ai, ai], -1)
    a3 = jnp.concatenate([ai, -ai], -1)

    grp = lambda a: a.transpose(1, 0, 2, 3)
    pad_t = ((0, 0), (0, 0), (0, 24 - (tc + 1)), (0, 0))
    blk = lambda rows, cols: pl.BlockSpec((None, 2, rows, cols), lambda i: (i, 0, 0, 0))
    wcat, ccat = pl.pallas_call(
        _s5_ops_kernel,
        grid=(g,),
        in_specs=[blk(24, n), blk(24, n), blk(p, n), blk(p, n), blk(p, n), blk(p, n),
                  pl.BlockSpec((None, p, 1), lambda i: (i, 0, 0))],
        out_specs=[pl.BlockSpec((None, tc * p, 768), lambda i: (i, 0, 0)),
                   pl.BlockSpec((None, 4 * n, tc * p), lambda i: (i, 0, 0))],
        out_shape=[jax.ShapeDtypeStruct((g, tc * p, 768), BF16), jax.ShapeDtypeStruct((g, 4 * n, tc * p), BF16)],
        compiler_params=_cparams(("arbitrary",)),
        name="s5_operators",
    )(jnp.pad(pw_r.transpose(2, 1, 0, 3), pad_t), jnp.pad(pw_i.transpose(2, 1, 0, 3), pad_t),
      grp(c_re), grp(c_im), grp(bb_r.transpose(0, 1, 3, 2)), grp(bb_i.transpose(0, 1, 3, 2)),
      d_skip.reshape(g, p, 1))
    return wcat, ccat, a1, a2, a3


def _s5_ops_kernel(pwr_ref, pwi_ref, cr_ref, ci_ref, br_ref, bi_ref, d_ref, w_ref, c_ref):
    tc, p, n = S5_TC, S5_P, S5_N
    nt = (((1,), (1,)), ((), ()))
    lane = lax.broadcasted_iota(jnp.int32, (p, tc * p), 1)
    row = lax.broadcasted_iota(jnp.int32, (p, tc * p), 0)
    eye_n = (lax.broadcasted_iota(jnp.int32, (n, n), 0) == lax.broadcasted_iota(jnp.int32, (n, n), 1)).astype(F32)

    def c_lam(d, taus):
        pr = jnp.concatenate([jnp.broadcast_to(pwr_ref[d, t:t + 1, :], (p, n)) for t in taus], axis=0)
        pi = jnp.concatenate([jnp.broadcast_to(pwi_ref[d, t:t + 1, :], (p, n)) for t in taus], axis=0)
        cr = jnp.concatenate([cr_ref[d]] * len(taus), axis=0)
        ci = jnp.concatenate([ci_ref[d]] * len(taus), axis=0)
        return cr * pr - ci * pi, cr * pi + ci * pr

    def lag_row(d, taus):
        clr, cli = c_lam(d, taus)
        return (lax.dot_general(br_ref[d], clr, nt, preferred_element_type=F32, precision=HIGHEST)
                - lax.dot_general(bi_ref[d], cli, nt, preferred_element_type=F32, precision=HIGHEST))

    rf = lag_row(0, list(range(tc))) + jnp.where(lane == row, d_ref[...], 0.0)
    rb = lag_row(1, list(range(tc - 1, -1, -1)))
    for ti in range(tc):
        sf = p * ti
        sb = p * (tc - 1 - ti)
        a = jnp.where(lane >= sf, pltpu.roll(rf, sf, 1) if sf else rf, 0.0)
        b = jnp.where(lane < tc * p - sb, pltpu.roll(rb, tc * p - sb, 1) if sb else rb, 0.0)
        w_ref[ti * p:(ti + 1) * p, 0:tc * p] = (a + b).astype(BF16)

    for d, col in ((0, tc * p), (1, tc * p + 4 * n)):
        for t in range(tc):
            e = tc - 1 - t if d == 0 else t
            pr = pwr_ref[d, e:e + 1, :]
            pi = pwi_ref[d, e:e + 1, :]
            sr = br_ref[d] * pr - bi_ref[d] * pi
            si = br_ref[d] * pi + bi_ref[d] * pr
            w_ref[t * p:(t + 1) * p, col:col + 4 * n] = jnp.concatenate([sr, si, si, sr], axis=1).astype(BF16)

    for d, taus in ((0, list(range(1, tc + 1))), (1, list(range(tc, 0, -1)))):
        clr, cli = c_lam(d, taus)
        c_ref[2 * n * d:2 * n * d + n, :] = lax.dot_general(
            eye_n, clr, nt, preferred_element_type=F32, precision=HIGHEST).astype(BF16)
        c_ref[2 * n * d + n:2 * n * (d + 1), :] = lax.dot_general(
            eye_n, -cli, nt, preferred_element_type=F32, precision=HIGHEST).astype(BF16)


def _s5_kernel(ul_ref, uc_ref, w_ref, c_ref, a1_ref, a2_ref, a3_ref, yl_ref, yc_ref,
               tok, xg, spf, sqf, spb, sqb, *, gb, nch, nch_lat, pitch):
    t_lat = nch_lat * S5_TC
    tok[:t_lat, :] = ul_ref[...].astype(F32)
    tok[t_lat:, :] = uc_ref[...].astype(F32)

    rt = next(r for r in (80, 40, 16, 8) if nch % r == 0)
    lane_rt = lax.broadcasted_iota(jnp.int32, (rt, 128), 1) // S5_P

    def block_transpose(a):
        for d in (4, 2, 1):
            keep = (lane_rt & d) == 0
            nxt = list(a)
            for i in range(8):
                if i & d == 0:
                    j = i + d
                    nxt[i] = jnp.where(keep, a[i], pltpu.roll(a[j], d * S5_P, 1))
                    nxt[j] = jnp.where(keep, pltpu.roll(a[i], 128 - d * S5_P, 1), a[j])
            a = nxt
        return a

    def to_compact(it, carry):
        r0 = pl.multiple_of(it * rt, rt)
        for h in range(2):
            z = [tok[pl.ds(r0 * S5_TC + 8 * h + t, rt, stride=S5_TC), :] for t in range(8)]
            b = block_transpose(z)
            for g in range(gb):
                xg[g, pl.ds(r0, rt), h * 128:(h + 1) * 128] = b[g]
        return carry

    lax.fori_loop(0, nch // rt, to_compact, 0)

    for g in range(gb):
        r = jnp.dot(xg[g].astype(BF16), w_ref[g], preferred_element_type=F32)
        xg[g] = r[:, :256]
        spf[g * pitch:g * pitch + nch, :] = r[:, 256:384]
        sqf[g * pitch:g * pitch + nch, :] = r[:, 384:512]
        spb[g * pitch:g * pitch + nch, :] = r[:, 512:640]
        sqb[g * pitch:g * pitch + nch, :] = r[:, 640:768]

    a1f, a2f, a3f = a1_ref[0], a2_ref[0], a3_ref[0]
    a1b, a2b, a3b = a1_ref[1], a2_ref[1], a3_ref[1]
    nch_ctx = nch - nch_lat

    def body(s, carry):
        pf, qf, pb, qb = carry
        cf = jnp.where(s < nch_ctx, s + nch_lat, s - nch_ctx)
        cb = nch - 1 - s
        idx_f = pl.ds(cf, gb, stride=pitch)
        idx_b = pl.ds(cb, gb, stride=pitch)
        sp = spf[idx_f, :]
        sq = sqf[idx_f, :]
        spf[idx_f, :] = pf
        pf, qf = pf * a1f + qf * a2f + sp, qf * a1f + pf * a3f + sq
        sp = spb[idx_b, :]
        sq = sqb[idx_b, :]
        spb[idx_b, :] = pb
        pb, qb = pb * a1b + qb * a2b + sp, qb * a1b + pb * a3b + sq
        return pf, qf, pb, qb

    z0 = jnp.zeros((gb, 128), F32)
    lax.fori_loop(0, nch, body, (z0, z0, z0, z0), unroll=4)

    for g in range(gb):
        hf = spf[g * pitch:g * pitch + nch, :].astype(BF16)
        hb = spb[g * pitch:g * pitch + nch, :].astype(BF16)
        cm = c_ref[g]
        y = (jnp.dot(hf, cm[:128, :], preferred_element_type=F32)
             + jnp.dot(hb, cm[128:, :], preferred_element_type=F32))
        xg[g] = xg[g] + y

    def to_tokens(it, carry):
        r0 = pl.multiple_of(it * rt, rt)
        for h in range(2):
            y = [xg[g, pl.ds(r0, rt), h * 128:(h + 1) * 128] for g in range(gb)]
            b = block_transpose(y)
            for t in range(8):
                tok[pl.ds(r0 * S5_TC + 8 * h + t, rt, stride=S5_TC), :] = b[t]
        return carry

    lax.fori_loop(0, nch // rt, to_tokens, 0)
    yl_ref[...] = tok[:t_lat, :].astype(yl_ref.dtype)
    yc_ref[...] = tok[t_lat:, :].astype(yc_ref.dtype)


def _s5_mixer(p_lat, p_ctx, ops):
    wcat, ccat, a1, a2, a3 = ops
    b, t_lat, _ = p_lat.shape
    t_ctx = p_ctx.shape[1]
    nch, nch_lat = (t_lat + t_ctx) // S5_TC, t_lat // S5_TC
    gb = 128 // S5_P
    pitch = ((nch + 7) // 8) * 8 + 8
    kern = functools.partial(_s5_kernel, gb=gb, nch=nch, nch_lat=nch_lat, pitch=pitch)
    once = pl.Buffered(1)
    return pl.pallas_call(
        kern,
        grid=(b, S5_G // gb),
        in_specs=[
            pl.BlockSpec((None, t_lat, 128), lambda bb, i: (bb, 0, COL_UA // 128 + i), pipeline_mode=once),
            pl.BlockSpec((None, t_ctx, 128), lambda bb, i: (bb, 0, COL_UA // 128 + i)),
            pl.BlockSpec((gb, 256, 768), lambda bb, i: (i, 0, 0)),
            pl.BlockSpec((gb, 256, 256), lambda bb, i: (i, 0, 0)),
            pl.BlockSpec((2, gb, 128), lambda bb, i: (0, i, 0)),
            pl.BlockSpec((2, gb, 128), lambda bb, i: (0, i, 0)),
            pl.BlockSpec((2, gb, 128), lambda bb, i: (0, i, 0)),
        ],
        out_specs=[
            pl.BlockSpec((None, t_lat, 128), lambda bb, i: (bb, 0, i), pipeline_mode=once),
            pl.BlockSpec((None, t_ctx, 128), lambda bb, i: (bb, 0, i)),
        ],
        out_shape=[jax.ShapeDtypeStruct((b, t_lat, BRANCH_W), BF16),
                   jax.ShapeDtypeStruct((b, t_ctx, BRANCH_W), BF16)],
        scratch_shapes=[pltpu.VMEM((t_lat + t_ctx, 128), F32), pltpu.VMEM((gb, nch, 256), F32)]
        + [pltpu.VMEM((gb * pitch, 128), F32) for _ in range(4)],
        compiler_params=_cparams(("arbitrary", "arbitrary")),
        name="s5_chunk_scan",
    )(p_lat, p_ctx, wcat, ccat, a1, a2, a3)


def _rope_tables(t_len):
    t = jnp.arange(t_len)
    rows = (t // GRID_W).astype(F32)
    cols = (t % GRID_W).astype(F32)
    inv = ROPE_THETA ** (-jnp.arange(ROPE_HALF, dtype=F32) / ROPE_HALF)
    lane = np.arange(128)
    dd = lane % HEAD_DIM
    fi = dd % ROPE_HALF
    use_row = jnp.asarray(dd < 32)
    sign = jnp.asarray(np.where((dd % 32) < ROPE_HALF, -1.0, 1.0).astype(np.float32))
    ang = jnp.where(use_row[None, :], rows[:, None], cols[:, None]) * inv[fi][None, :]
    return jnp.cos(ang), jnp.sin(ang) * sign[None, :]


def _rope_kernel(q_ref, k_ref, cos_ref, sin_ref, qo_ref, kt_ref, *, scale):
    cos = jnp.concatenate([cos_ref[...]] * 4, axis=1)
    sin = jnp.concatenate([sin_ref[...]] * 4, axis=1)
    lane = lax.broadcasted_iota(jnp.int32, cos.shape, 1)
    low = (lane % 32) < ROPE_HALF

    def rot(x):
        n = x.shape[1]
        partner = jnp.where(low, pltpu.roll(x, n - ROPE_HALF, 1), pltpu.roll(x, ROPE_HALF, 1))
        return x * cos + partner * sin

    qo_ref[...] = (rot(q_ref[...].astype(F32)) * scale).astype(BF16)
    kr = rot(k_ref[...].astype(F32))
    for c in range(kt_ref.shape[0]):
        kt_ref[c] = kr[c * 128:(c + 1) * 128, :].T.astype(BF16)


def _rope_qk(p_lat, cos_t, sin_t):
    b, t, _ = p_lat.shape
    tm = 1024
    return pl.pallas_call(
        functools.partial(_rope_kernel, scale=HEAD_DIM ** -0.5 * LOG2E),
        grid=(b, t // tm),
        in_specs=[
            pl.BlockSpec((None, tm, 512), lambda bb, i: (bb, i, COL_Q // 512)),
            pl.BlockSpec((None, tm, 512), lambda bb, i: (bb, i, COL_K // 512)),
            pl.BlockSpec((tm, 128), lambda bb, i: (i, 0)),
            pl.BlockSpec((tm, 128), lambda bb, i: (i, 0)),
        ],
        out_specs=[pl.BlockSpec((None, tm, 512), lambda bb, i: (bb, i, 0)),
                   pl.BlockSpec((None, tm // 128, 512, 128), lambda bb, i: (bb, i, 0, 0))],
        out_shape=[jax.ShapeDtypeStruct((b, t, BRANCH_W), BF16),
                   jax.ShapeDtypeStruct((b, t // 128, BRANCH_W, 128), BF16)],
        compiler_params=_cparams(("arbitrary", "arbitrary")),
        name="rope_qk",
    )(p_lat, p_lat, cos_t, sin_t)


_NA_VARIANTS = (((4, 5), (0, 1)), ((0, 1), (0, 0)), ((2, 3), (0, 0)), ((4, 5), (0, 0)), ((6, 7), (0, 0)))
NA_WIN = NA_KR + 2


def _na_bias_tables(rpb):
    h = rpb.shape[0]
    qc = np.arange(GRID_W)
    kc = np.arange(GRID_W)
    cs = np.clip(qc - NA_KW // 2, 0, GRID_W - NA_KW)
    valid_c = (kc[None, :] >= cs[:, None]) & (kc[None, :] < cs[:, None] + NA_KW)
    rel_c = np.clip(kc[None, :] - qc[:, None] + NA_KW - 1, 0, 2 * NA_KW - 2)
    oh_c = np.eye(2 * NA_KW - 1, dtype=np.float32)[rel_c]
    j = np.arange(NA_WIN)
    oh_r = np.zeros((len(_NA_VARIANTS), 2, NA_WIN, 2 * NA_KR - 1), np.float32)
    valid_r = np.zeros((len(_NA_VARIANTS), 2, NA_WIN), bool)
    for n, (offs, los) in enumerate(_NA_VARIANTS):
        for rr in range(2):
            valid_r[n, rr] = (j >= los[rr]) & (j < los[rr] + NA_KR)
            rel_r = np.clip(j - offs[rr] + NA_KR - 1, 0, 2 * NA_KR - 2)
            oh_r[n, rr] = np.eye(2 * NA_KR - 1, dtype=np.float32)[rel_r]
    bias = jnp.einsum("nrja,hab,ckb->nhrcjk", jnp.asarray(oh_r), rpb, jnp.asarray(oh_c), precision=HIGH) * LOG2E
    valid = valid_r[:, None, :, None, :, None] & valid_c[None, None, None, :, None, :]
    bias = jnp.where(jnp.asarray(valid), bias, NEG_BIG)
    return bias.reshape(len(_NA_VARIANTS), h // 2, 4 * GRID_W, NA_WIN * GRID_W)


def _na_kernel(q_ref, kp_ref, kc_ref, kn_ref, vp_ref, vc_ref, vn_ref, kx_ref, vx_ref, bias_ref, o_ref,
               kt3, vcat, *, ni):
    half = 4 * GRID_W
    tq = 8 * GRID_W
    kt3[0:2] = kp_ref[2:4]
    kt3[2:6] = kc_ref[...]
    kt3[6:8] = kn_ref[0:2]
    vcat[:half, :] = vp_ref[half:, :]
    vcat[half:half + tq, :] = vc_ref[...]
    vcat[half + tq:, :] = vn_ref[:half, :]
    kxt = kx_ref[...].astype(F32).T.astype(BF16)
    i = pl.program_id(1)
    head0 = lax.broadcasted_iota(jnp.int32, (2 * GRID_W, 128), 1) < HEAD_DIM
    nchunk = NA_WIN // 2

    def pair_body(a, carry):
        first = i == 0
        last = i == ni - 1
        s0 = jnp.where(first, jnp.maximum(2 * a, 4), jnp.where(last, jnp.minimum(2 * a, 4), 2 * a))
        var = jnp.where(first & (a < 2), 1 + a, jnp.where(last & (a >= 2), 1 + a, 0))
        q0 = pl.multiple_of(a * 2 * GRID_W, 2 * GRID_W)
        k0 = pl.multiple_of(s0 * GRID_W, 2 * GRID_W)
        sp = s0 // 2
        nhp = NA_HEADS // 2
        lss = [slice(hp * 128, (hp + 1) * 128) for hp in range(nhp)]
        scores = []
        for hp in range(nhp):
            qp = q_ref[pl.ds(q0, 2 * GRID_W), lss[hp]]
            zero = jnp.zeros_like(qp)
            q2 = jnp.concatenate([jnp.where(head0, qp, zero), jnp.where(head0, zero, qp)], axis=0)
            kw = kt3[pl.ds(sp, nchunk), lss[hp], :]
            s_loc = jnp.concatenate(
                [jnp.dot(q2, kw[c], preferred_element_type=F32) for c in range(nchunk)], axis=1) + bias_ref[var, hp]
            s_ctx = jnp.dot(q2, kxt[lss[hp], :], preferred_element_type=F32)
            scores.append((s_loc, s_ctx))
        probs = []
        for s_loc, s_ctx in scores:
            m = jnp.maximum(jnp.max(s_loc, axis=-1, keepdims=True), jnp.max(s_ctx, axis=-1, keepdims=True))
            p_loc = jnp.exp2(s_loc - m)
            p_ctx = jnp.exp2(s_ctx - m)
            den = jnp.sum(p_loc, axis=-1, keepdims=True) + jnp.sum(p_ctx, axis=-1, keepdims=True)
            probs.append((p_loc.astype(BF16), p_ctx.astype(BF16), den))
        outs = []
        for hp in range(nhp):
            p_loc, p_ctx, den = probs[hp]
            vw = vcat[pl.ds(k0, NA_WIN * GRID_W), lss[hp]]
            o = (jnp.dot(p_loc, vw, preferred_element_type=F32)
                 + jnp.dot(p_ctx, vx_ref[:, lss[hp]], preferred_element_type=F32))
            o = o / den
            outs.append(jnp.where(head0, o[:2 * GRID_W], o[2 * GRID_W:]))
        o_ref[pl.ds(q0, 2 * GRID_W), :] = jnp.concatenate(outs, axis=1).astype(o_ref.dtype)
        return carry

    lax.fori_loop(0, 4, pair_body, 0)


def _na_mixer(qr, kt, p_lat, p_ctx, bias):
    b, t, _ = qr.shape
    nc = p_ctx.shape[1]
    tq = 8 * GRID_W
    ni = t // tq
    assert ni >= 2
    blk = lambda off, col: pl.BlockSpec(
        (None, tq, BRANCH_W), lambda bb, i: (bb, jnp.clip(i + off, 0, ni - 1), col))
    ktb = lambda off: pl.BlockSpec(
        (None, tq // 128, BRANCH_W, 128), lambda bb, i: (bb, jnp.clip(i + off, 0, ni - 1), 0, 0))
    cv = COL_V // BRANCH_W
    return pl.pallas_call(
        functools.partial(_na_kernel, ni=ni),
        grid=(b, ni),
        in_specs=[
            blk(0, 0), ktb(-1), ktb(0), ktb(1),
            blk(-1, cv), blk(0, cv), blk(1, cv),
            pl.BlockSpec((None, nc, BRANCH_W), lambda bb, i: (bb, 0, COL_K // BRANCH_W)),
            pl.BlockSpec((None, nc, BRANCH_W), lambda bb, i: (bb, 0, cv)),
            pl.BlockSpec(bias.shape, lambda bb, i: (0, 0, 0, 0), pipeline_mode=pl.Buffered(1)),
        ],
        out_specs=pl.BlockSpec((None, tq, BRANCH_W), lambda bb, i: (bb, i, 0)),
        out_shape=jax.ShapeDtypeStruct((b, t, BRANCH_W), BF16),
        scratch_shapes=[pltpu.VMEM((2 * tq // 128, BRANCH_W, 128), BF16), pltpu.VMEM((2 * tq, BRANCH_W), BF16)],
        compiler_params=_cparams(("arbitrary", "arbitrary")),
        name="na_attention",
    )(qr, kt, kt, kt, p_lat, p_lat, p_lat, p_ctx, p_ctx, bias)


def _ctx_attn_kernel(q_ref, k_ref, v_ref, o_ref, *, scale):
    dn = (((1,), (1,)), ((), ()))
    outs = []
    for hh in range(2):
        sl = slice(hh * HEAD_DIM, (hh + 1) * HEAD_DIM)
        qh = (q_ref[:, sl].astype(F32) * scale).astype(BF16)
        s = lax.dot_general(qh, k_ref[:, sl], dn, preferred_element_type=F32)
        m = jnp.max(s, axis=-1, keepdims=True)
        p = jnp.exp(s - m)
        den = jnp.sum(p, axis=-1, keepdims=True)
        outs.append(jnp.dot(p.astype(BF16), v_ref[:, sl], preferred_element_type=F32) / den)
    o_ref[...] = jnp.concatenate(outs, axis=1).astype(o_ref.dtype)


def _ctx_attention(p_ctx):
    b, nc, _ = p_ctx.shape
    spec = lambda col: pl.BlockSpec((None, nc, 128), lambda bb, hp: (bb, 0, col // 128 + hp))
    return pl.pallas_call(
        functools.partial(_ctx_attn_kernel, scale=HEAD_DIM ** -0.5),
        grid=(b, NA_HEADS // 2),
        in_specs=[spec(COL_Q), spec(COL_K), spec(COL_V)],
        out_specs=pl.BlockSpec((None, nc, 128), lambda bb, hp: (bb, 0, hp)),
        out_shape=jax.ShapeDtypeStruct((b, nc, BRANCH_W), BF16),
        compiler_params=_cparams(("arbitrary", "arbitrary")),
        name="ctx_attention",
    )(p_ctx, p_ctx, p_ctx)


def _filter_features(t_len):
    pos = jnp.arange(t_len, dtype=F32)
    t = pos / max(t_len - 1, 1)
    w = 2.0 * math.pi * pos / t_len
    bands = jnp.linspace(1e-4, HY_BANDS - 1, HY_BANDS, dtype=F32)
    feats = jnp.concatenate([t[:, None], jnp.cos(w[:, None] * bands), -jnp.sin(w[:, None] * bands)], axis=-1)
    feats2 = jnp.concatenate([feats, feats[:1], feats[:0:-1]], axis=0)
    return jnp.pad(feats2, ((0, 0), (0, 128 - HY_EMB)))


def _filter_kernel(f_ref, w1_ref, b1_ref, fr_ref, w2_ref, b2_ref, w3_ref, dl_ref, k_ref, ss_ref):
    f = f_ref[...]
    fr = fr_ref[...]
    h = jnp.sin(fr * (jnp.dot(f, w1_ref[...], preferred_element_type=F32, precision=HIGHEST) + b1_ref[...]))
    h = jnp.sin(fr * (jnp.dot(h, w2_ref[...], preferred_element_type=F32, precision=HIGHEST) + b2_ref[...]))
    k = jnp.dot(h, w3_ref[...], preferred_element_type=F32, precision=HIGHEST)
    k = k * jnp.exp(-f[:, 0:1] * dl_ref[...])
    k_ref[...] = k

    @pl.when(pl.program_id(0) == 0)
    def _():
        ss_ref[...] = jnp.zeros_like(ss_ref)

    ss_ref[...] += jnp.sum(k * k, axis=0, keepdims=True)


def _hyena_filter_time(t_len, w1, b1, freq, w2, b2, w3):
    n = 2 * t_len
    tm = min(1024, t_len)
    feats = _filter_features(t_len)
    w1p = jnp.pad(w1, ((0, 128 - HY_EMB), (0, 0)))
    deltas = jnp.abs(jnp.linspace(HY_MIN_DECAY, HY_MAX_DECAY, BRANCH_W, dtype=F32))
    dl = jnp.concatenate([deltas, deltas]).reshape(1, 2 * BRANCH_W)
    half = t_len // tm
    nco = 2 * BRANCH_W
    return pl.pallas_call(
        _filter_kernel,
        grid=(n // tm,),
        in_specs=[
            pl.BlockSpec((tm, 128), lambda i: (i, 0)),
            pl.BlockSpec((128, HY_HID), lambda i: (0, 0)),
            pl.BlockSpec((1, HY_HID), lambda i: (0, 0)),
            pl.BlockSpec((1, HY_HID), lambda i: (0, 0)),
            pl.BlockSpec((HY_HID, HY_HID), lambda i: (0, 0)),
            pl.BlockSpec((1, HY_HID), lambda i: (0, 0)),
            pl.BlockSpec((HY_HID, nco), lambda i: (0, i // half)),
            pl.BlockSpec((1, nco), lambda i: (0, 0)),
        ],
        out_specs=[pl.BlockSpec((tm, nco), lambda i: (i, 0)), pl.BlockSpec((1, nco), lambda i: (0, 0))],
        out_shape=[jax.ShapeDtypeStruct((n, nco), F32), jax.ShapeDtypeStruct((1, nco), F32)],
        compiler_params=_cparams(("arbitrary",)),
        name="hyena_filter_ffn_ctx",
    )(feats, w1p, b1.reshape(1, -1), freq.reshape(1, -1), w2, b2.reshape(1, -1), w3, dl)


def _filter_cm_kernel(f_ref, t_ref, w1_ref, b1_ref, fr_ref, w2_ref, b2_ref, w3_ref, dl_ref, k_ref, ss_ref):
    fr = fr_ref[...]
    h = jnp.sin(fr * (jnp.dot(w1_ref[...], f_ref[...], preferred_element_type=F32, precision=HIGHEST) + b1_ref[...]))
    h = jnp.sin(fr * (jnp.dot(w2_ref[...], h, preferred_element_type=F32, precision=HIGHEST) + b2_ref[...]))
    k = lax.dot_general(h.astype(BF16), w3_ref[...].astype(BF16), (((0,), (0,)), ((), ())), preferred_element_type=F32)
    k = k * jnp.exp(-t_ref[...] * dl_ref[...])
    rows = k.shape[0] // FFT_N2
    for cb in range(k.shape[1] // 128):
        piece = k[:, cb * 128:(cb + 1) * 128].reshape(rows, FFT_N2, 128)
        k_ref[cb * 128:(cb + 1) * 128] = jnp.swapaxes(jnp.swapaxes(piece, 1, 2), 0, 1)

    @pl.when(pl.program_id(0) == 0)
    def _():
        ss_ref[...] = jnp.zeros_like(ss_ref)

    ss_ref[...] += jnp.sum(k * k, axis=0, keepdims=True)


def _hyena_filter_cm(t_len, w1, b1, freq, w2, b2, w3):
    n = 2 * t_len
    tm = 2048
    assert t_len % tm == 0
    slot = jnp.arange(n)
    pos = jnp.where(slot < t_len, slot, jnp.where(slot == t_len, 0, n - slot)).astype(F32)
    t = pos / max(t_len - 1, 1)
    w = 2.0 * math.pi * pos / t_len
    bands = jnp.linspace(1e-4, HY_BANDS - 1, HY_BANDS, dtype=F32)
    feats = jnp.concatenate([t[None, :], jnp.cos(w[None, :] * bands[:, None]), -jnp.sin(w[None, :] * bands[:, None])],
                            axis=0)
    feats = jnp.pad(feats, ((0, 128 - HY_EMB), (0, 0)))
    w1t = jnp.pad(w1, ((0, 128 - HY_EMB), (0, 0))).T
    deltas = jnp.abs(jnp.linspace(HY_MIN_DECAY, HY_MAX_DECAY, BRANCH_W, dtype=F32))
    nco = 2 * BRANCH_W
    dl = jnp.concatenate([deltas, deltas]).reshape(1, nco)
    half = t_len // tm
    col = lambda a: a.reshape(-1, 1)
    const = lambda shape: pl.BlockSpec(shape, lambda i: (0, 0))
    return pl.pallas_call(
        _filter_cm_kernel,
        grid=(n // tm,),
        in_specs=[
            pl.BlockSpec((128, tm), lambda i: (0, i)),
            pl.BlockSpec((tm, 1), lambda i: (i, 0)),
            const((HY_HID, 128)), const((HY_HID, 1)), const((HY_HID, 1)),
            const((HY_HID, HY_HID)), const((HY_HID, 1)),
            pl.BlockSpec((HY_HID, nco), lambda i: (0, i // half)),
            const((1, nco)),
        ],
        out_specs=[pl.BlockSpec((nco, tm // FFT_N2, FFT_N2), lambda i: (0, i, 0)), const((1, nco))],
        out_shape=[jax.ShapeDtypeStruct((nco, n // FFT_N2, FFT_N2), F32), jax.ShapeDtypeStruct((1, nco), F32)],
        compiler_params=_cparams(("arbitrary",)),
        name="hyena_filter_ffn",
    )(feats, col(t), w1t, col(b1), col(freq), w2.T, col(b2), w3, dl)


def _dft_consts(n1):
    n2 = FFT_N2
    n = n1 * n2
    nh = n1 // 2
    a1 = -2.0 * np.pi * np.outer(np.arange(n1), np.arange(n1)) / n1
    f1r, f1i = np.cos(a1), np.sin(a1)
    a2 = -2.0 * np.pi * np.outer(np.arange(n2), np.arange(n2)) / n2
    f2r, f2i = np.cos(a2), np.sin(a2)
    at = -2.0 * np.pi * np.outer(np.arange(n1), np.arange(n2)) / n
    bf = lambda a: jnp.asarray(a.astype(np.float32)).astype(BF16)
    return dict(
        s1_data=bf(np.block([[f1r[:, :nh], -f1i[:, :nh]], [f1i[:, :nh], f1r[:, :nh]]])),
        s1_real=bf(np.concatenate([f1r, f1i], axis=0)),
        s6=bf(np.block([[f1r[:nh], f1i[:nh]], [-f1i[:nh], f1r[:nh]]]) / n),
        fa=bf(np.concatenate([f2r, f2i], axis=1)), fb=bf(np.concatenate([-f2i, f2r], axis=1)),
        ia=bf(np.concatenate([f2r, -f2i], axis=1)), ib=bf(np.concatenate([f2i, f2r], axis=1)),
        tw_r=bf(np.cos(at)), tw_i=bf(np.sin(at)),
    )


def _lane_dft(ar_ref, ai_ref, fa_ref, fb_ref):
    cb, n1, n2 = ar_ref.shape
    ar = ar_ref[...].reshape(cb * n1, n2)
    ai = ai_ref[...].reshape(cb * n1, n2)
    x = (jnp.dot(ar, fa_ref[...], preferred_element_type=F32)
         + jnp.dot(ai, fb_ref[...], preferred_element_type=F32))
    return x[:, :n2], x[:, n2:]


def _kf_kernel(k_ref, ss_ref, s1_ref, twr_ref, twi_ref, fa_ref, fb_ref, o_ref, ar_ref, ai_ref, *, cb, n1):
    i = pl.program_id(0)
    row = lax.broadcasted_iota(jnp.int32, (n1, FFT_N2), 0)
    lane = lax.broadcasted_iota(jnp.int32, (n1, FFT_N2), 1)
    keep = jnp.logical_not((row == n1 // 2) & (lane == 0))
    twr, twi = twr_ref[...], twi_ref[...]
    for c in range(cb):
        scale = lax.rsqrt(jnp.full((n1, FFT_N2), ss_ref[i * cb + c], F32) + EPS)
        k = jnp.where(keep, k_ref[c] * scale, 0.0)
        a = jnp.dot(s1_ref[...], k.astype(BF16), preferred_element_type=F32)
        a_r, a_i = a[:n1].astype(BF16), a[n1:].astype(BF16)
        ar_ref[c] = a_r * twr - a_i * twi
        ai_ref[c] = a_r * twi + a_i * twr
    xr, xi = _lane_dft(ar_ref, ai_ref, fa_ref, fb_ref)
    o_ref[:, :n1, :] = xr.reshape(cb, n1, FFT_N2).astype(o_ref.dtype)
    o_ref[:, n1:, :] = xi.reshape(cb, n1, FFT_N2).astype(o_ref.dtype)


def _hyena_filter_spectrum(kt, ss, dft):
    nco, n1, _ = kt.shape
    cb = 8
    const = lambda shape: pl.BlockSpec(shape, lambda i: (0,) * len(shape))
    return pl.pallas_call(
        functools.partial(_kf_kernel, cb=cb, n1=n1),
        grid=(nco // cb,),
        in_specs=[
            pl.BlockSpec((cb, n1, FFT_N2), lambda i: (i, 0, 0)),
            pl.BlockSpec(memory_space=pltpu.SMEM),
            const((2 * n1, n1)), const((n1, FFT_N2)), const((n1, FFT_N2)),
            const((FFT_N2, 2 * FFT_N2)), const((FFT_N2, 2 * FFT_N2)),
        ],
        out_specs=pl.BlockSpec((cb, 2 * n1, FFT_N2), lambda i: (i, 0, 0)),
        out_shape=jax.ShapeDtypeStruct((nco, 2 * n1, FFT_N2), BF16),
        scratch_shapes=[pltpu.VMEM((cb, n1, FFT_N2), BF16), pltpu.VMEM((cb, n1, FFT_N2), BF16)],
        compiler_params=_cparams(("arbitrary",)),
        name="hyena_filter_fft",
    )(kt, ss.reshape(nco), dft["s1_real"], dft["tw_r"], dft["tw_i"], dft["fa"], dft["fb"])


def _tok2cm_kernel(x_ref, o_ref):
    rows = x_ref.shape[0] // FFT_N2
    for cb in range(x_ref.shape[1] // 128):
        x3 = x_ref[:, cb * 128:(cb + 1) * 128].reshape(rows, FFT_N2, 128)
        o_ref[cb * 128:(cb + 1) * 128] = jnp.swapaxes(jnp.swapaxes(x3, 1, 2), 0, 1)


def _to_channel_major(p, col0, ncols):
    b, t, _ = p.shape
    tt = 16 * FFT_N2
    nt = t // tt
    cw = 256
    return pl.pallas_call(
        _tok2cm_kernel,
        grid=(ncols // cw, b, nt),
        in_specs=[pl.BlockSpec((None, tt, cw), lambda c, bb, i: (bb, i, col0 // cw + c))],
        out_specs=pl.BlockSpec((cw, tt // FFT_N2, FFT_N2), lambda c, bb, i: (c, bb * nt + i, 0)),
        out_shape=jax.ShapeDtypeStruct((ncols, b * t // FFT_N2, FFT_N2), p.dtype),
        compiler_params=_cparams(("arbitrary", "arbitrary", "arbitrary")),
        name="to_channel_major",
    )(p)


def _cm2tok_kernel(x_ref, o_ref):
    x3 = jnp.swapaxes(jnp.swapaxes(x_ref[...], 0, 1), 1, 2)
    o_ref[...] = x3.reshape(o_ref.shape)


def _to_token_major(y_cm, b):
    c, rows, _ = y_cm.shape
    t = rows * FFT_N2 // b
    tt = 16 * FFT_N2
    nt = t // tt
    return pl.pallas_call(
        _cm2tok_kernel,
        grid=(c // 128, b, nt),
        in_specs=[pl.BlockSpec((128, tt // FFT_N2, FFT_N2), lambda cc, bb, i: (cc, bb * nt + i, 0))],
        out_specs=pl.BlockSpec((None, tt, 128), lambda cc, bb, i: (bb, i, cc)),
        out_shape=jax.ShapeDtypeStruct((b, t, c), y_cm.dtype),
        compiler_params=_cparams(("arbitrary", "arbitrary", "arbitrary")),
        name="to_token_major",
    )(y_cm)


def _hyena_kernel(v_ref, x1_ref, x2_ref, kf1_ref, kf2_ref, cw_ref, cbias_ref, dd_ref,
                  s1_ref, s6_ref, twr_ref, twi_ref, fa_ref, fb_ref, ia_ref, ib_ref,
                  o_ref, vs, x1s, x2s, ar_ref, ai_ref, *, cb, n1, nch):
    i = pl.program_id(0)
    nh = n1 // 2
    rows = 2 * nh
    row = lax.broadcasted_iota(jnp.int32, (rows, FFT_N2), 0)
    lane = lax.broadcasted_iota(jnp.int32, (rows, FFT_N2), 1)
    first = lane == 0
    last = lane == FFT_N2 - 1
    seq_start = first & ((row % nh) == 0)
    seq_end = last & ((row % nh) == nh - 1)
    twr, twi = twr_ref[...], twi_ref[...]

    def short_conv(x, ch):
        prev = pltpu.roll(x, 1, 1)
        prev = jnp.where(first, pltpu.roll(prev, 1, 0), prev)
        prev = jnp.where(seq_start, 0.0, prev)
        nxt = pltpu.roll(x, FFT_N2 - 1, 1)
        nxt = jnp.where(last, pltpu.roll(nxt, rows - 1, 0), nxt)
        nxt = jnp.where(seq_end, 0.0, nxt)
        return cw_ref[ch] * prev + cw_ref[nch + ch] * x + cw_ref[2 * nch + ch] * nxt + cbias_ref[ch]

    def stage1_all(zs):
        acc = [jnp.dot(s1_ref[...], z.astype(BF16), preferred_element_type=F32) for z in zs]
        for c, a in enumerate(acc):
            a_r, a_i = a[:n1].astype(BF16), a[n1:].astype(BF16)
            ar_ref[c] = a_r * twr - a_i * twi
            ai_ref[c] = a_r * twi + a_i * twr

    def spectral(kf_ref):
        xr, xi = _lane_dft(ar_ref, ai_ref, fa_ref, fb_ref)
        kr = kf_ref[:, :n1, :].reshape(cb * n1, FFT_N2)
        ki = kf_ref[:, n1:, :].reshape(cb * n1, FFT_N2)
        xr, xi = xr.astype(BF16), xi.astype(BF16)
        yr = xr * kr - xi * ki
        yi = xr * ki + xi * kr
        bm = (jnp.dot(yr, ia_ref[...], preferred_element_type=F32)
              + jnp.dot(yi, ib_ref[...], preferred_element_type=F32))
        br = bm[:, :FFT_N2].astype(BF16).reshape(cb, n1, FFT_N2)
        bi = bm[:, FFT_N2:].astype(BF16).reshape(cb, n1, FFT_N2)
        ar_ref[...] = br * twr[None] + bi * twi[None]
        ai_ref[...] = bi * twr[None] - br * twi[None]

    def stage6(c):
        bcat = jnp.concatenate([ar_ref[c], ai_ref[c]], axis=0)
        return jnp.dot(s6_ref[...], bcat, preferred_element_type=F32)

    nbr = BRANCH_W
    for c in range(cb):
        ch = i * cb + c
        vs[c] = short_conv(v_ref[c].astype(F32), ch)
        x1s[c] = short_conv(x1_ref[c].astype(F32), nbr + ch)
        x2s[c] = short_conv(x2_ref[c].astype(F32), 2 * nbr + ch)
    stage1_all([vs[c] for c in range(cb)])
    spectral(kf1_ref)
    ys = [stage6(c) for c in range(cb)]
    for c in range(cb):
        vs[c] = x1s[c] * (ys[c] + vs[c] * dd_ref[i * cb + c])
    stage1_all([vs[c] for c in range(cb)])
    spectral(kf2_ref)
    ys = [stage6(c) for c in range(cb)]
    for c in range(cb):
        o_ref[c] = (x2s[c] * (ys[c] + vs[c] * dd_ref[nbr + i * cb + c])).astype(o_ref.dtype)


def _hyena_lat(hy_cm, kf, conv_w, conv_b, hy_d, dft):
    nch, rows, _ = hy_cm.shape
    n1 = rows
    cb = 8
    nblk = BRANCH_W // cb
    const = lambda shape: pl.BlockSpec(shape, lambda i: (0,) * len(shape))
    smem = pl.BlockSpec(memory_space=pltpu.SMEM)
    data = lambda sec: pl.BlockSpec((cb, rows, FFT_N2), lambda i: (sec * nblk + i, 0, 0))
    kfs = lambda o: pl.BlockSpec((cb, 2 * n1, FFT_N2), lambda i: (o * nblk + i, 0, 0))
    return pl.pallas_call(
        functools.partial(_hyena_kernel, cb=cb, n1=n1, nch=nch),
        grid=(nblk,),
        in_specs=[
            data(0), data(1), data(2), kfs(0), kfs(1), smem, smem, smem,
            const((2 * n1, n1)), const((n1, 2 * n1)), const((n1, FFT_N2)), const((n1, FFT_N2)),
            const((FFT_N2, 2 * FFT_N2)), const((FFT_N2, 2 * FFT_N2)),
            const((FFT_N2, 2 * FFT_N2)), const((FFT_N2, 2 * FFT_N2)),
        ],
        out_specs=pl.BlockSpec((cb, rows, FFT_N2), lambda i: (i, 0, 0)),
        out_shape=jax.ShapeDtypeStruct((BRANCH_W, rows, FFT_N2), BF16),
        scratch_shapes=[pltpu.VMEM((cb, rows, FFT_N2), F32) for _ in range(3)]
        + [pltpu.VMEM((cb, n1, FFT_N2), BF16) for _ in range(2)],
        compiler_params=_cparams(("arbitrary",)),
        name="hyena_fftconv",
    )(hy_cm, hy_cm, hy_cm, kf, kf, conv_w.reshape(-1), conv_b, hy_d.reshape(-1),
      dft["s1_data"], dft["s6"], dft["tw_r"], dft["tw_i"], dft["fa"], dft["fb"], dft["ia"], dft["ib"])


def _hyena_ctx_kernel(hy_ref, k_ref, ss_ref, cw_ref, cbias_ref, dd_ref, fwd_a_ref, fwd_b_ref,
                      inv_a_ref, inv_b_ref, o_ref, *, t_len):
    nbr = BRANCH_W
    lane = lax.broadcasted_iota(jnp.int32, (nbr, t_len), 1)

    def short_conv(x, sec):
        sl = slice(sec * nbr, (sec + 1) * nbr)
        prev = jnp.where(lane == 0, 0.0, pltpu.roll(x, 1, 1))
        nxt = jnp.where(lane == t_len - 1, 0.0, pltpu.roll(x, t_len - 1, 1))
        return cw_ref[0, sl, :] * prev + cw_ref[1, sl, :] * x + cw_ref[2, sl, :] * nxt + cbias_ref[sl, :]

    def sec(b, s):
        return short_conv(hy_ref[b, s * nbr:(s + 1) * nbr, :].astype(F32), s)

    klane = lax.broadcasted_iota(jnp.int32, k_ref.shape, 1)
    kk = jnp.where(klane == t_len, 0.0, k_ref[...] * lax.rsqrt(ss_ref[...] + EPS))
    kf = jnp.dot(kk.astype(BF16), fwd_a_ref[...], preferred_element_type=F32)
    n = 2 * t_len

    def conv(z0, z1, order):
        x = (jnp.dot(z0.astype(BF16), fwd_a_ref[:t_len, :], preferred_element_type=F32)
             + jnp.dot(z1.astype(BF16), fwd_b_ref[:t_len, :], preferred_element_type=F32))
        xr, xi = x[:, :n], x[:, n:]
        kr = kf[order * nbr:(order + 1) * nbr, :n]
        ki = kf[order * nbr:(order + 1) * nbr, n:]
        yr = (xr * kr - xi * ki).astype(BF16)
        yi = (xr * ki + xi * kr).astype(BF16)
        y = (jnp.dot(yr, inv_a_ref[...], preferred_element_type=F32)
             + jnp.dot(yi, inv_b_ref[...], preferred_element_type=F32))
        return y[:, :t_len], y[:, t_len:]

    v0, v1 = sec(0, 0), sec(1, 0)
    y0, y1 = conv(v0, v1, 0)
    d1 = dd_ref[:nbr, :]
    d2 = dd_ref[nbr:, :]
    z0 = sec(0, 1) * (y0 + v0 * d1)
    z1 = sec(1, 1) * (y1 + v1 * d1)
    y0, y1 = conv(z0, z1, 1)
    o_ref[0] = (sec(0, 2) * (y0 + z0 * d2)).astype(o_ref.dtype)
    o_ref[1] = (sec(1, 2) * (y1 + z1 * d2)).astype(o_ref.dtype)


def _hyena_ctx(hy_ctx, k_time, ss, conv_w, conv_b, hy_d):
    b, t_len, nch = hy_ctx.shape
    n = 2 * t_len
    ang = -2.0 * np.pi * np.outer(np.arange(n), np.arange(n)) / n
    fr, fi = np.cos(ang), np.sin(ang)
    bf = lambda a: jnp.asarray(a.astype(np.float32)).astype(BF16)
    fwd_a = bf(np.concatenate([fr, fi], axis=1))
    fwd_b = bf(np.concatenate([-fi, fr], axis=1))
    inv_a = bf(np.concatenate([fr[:, :t_len], -fi[:, :t_len]], axis=1) / n)
    inv_b = bf(np.concatenate([fi[:, :t_len], fr[:, :t_len]], axis=1) / n)
    out = pl.pallas_call(
        functools.partial(_hyena_ctx_kernel, t_len=t_len),
        out_shape=jax.ShapeDtypeStruct((b, BRANCH_W, t_len), BF16),
        compiler_params=pltpu.CompilerParams(vmem_limit_bytes=VMEM_LIMIT),
        name="hyena_ctx",
    )(hy_ctx.transpose(0, 2, 1), k_time.T, ss.reshape(-1, 1), conv_w.reshape(3, nch, 1),
      conv_b.reshape(nch, 1), hy_d.reshape(-1, 1), fwd_a, fwd_b, inv_a, inv_b)
    return out.transpose(0, 2, 1)


def _merge_kernel(x_ref, ya_ref, yb_ref, yc_ref, za_ref, zb_ref, zc_ref, g0_ref, g1_ref, g2_ref, mod_ref,
                  gw_ref, gb_ref, wb_ref, wo_ref, fg_ref, o_ref, *, mod_row, d, final):
    ya = jax.nn.gelu(ya_ref[...].astype(F32)).astype(BF16)
    glu = jnp.dot(ya, gw_ref[...], preferred_element_type=F32) + gb_ref[...]
    y_a = glu[:, :BRANCH_W] * (1.0 + jnp.tanh(glu[:, BRANCH_W:]))

    def branch(i, y, z_ref, g_ref):
        zh = z_ref[...]
        u = (zh + zh * jnp.tanh(zh)) * y
        return (1.0 + jnp.tanh(g_ref[...].astype(F32))) * jnp.dot(u, wb_ref[i], preferred_element_type=F32)

    acc = branch(0, y_a.astype(BF16), za_ref, g0_ref)
    acc = acc + branch(1, yb_ref[...], zb_ref, g1_ref)
    acc = acc + branch(2, yc_ref[...], zc_ref, g2_ref)
    out = jnp.dot(acc.astype(BF16), wo_ref[...], preferred_element_type=F32)
    if mod_row is None:
        m = mod_ref[pl.ds(pl.program_id(0), 1), :]
    else:
        m = mod_ref[mod_row:mod_row + 1, :]
    xn = x_ref[...] + m[:, 2 * d:] * out
    if final:
        ms = jnp.mean(xn * xn, axis=-1, keepdims=True)
        xn = xn * lax.rsqrt(ms + EPS) * fg_ref[...]
    o_ref[...] = xn


def _merge(x, ya, yb, yc, p, mod, glu_w, glu_b, w_branch, w_out, final_g, *, mod_row, tm, final):
    b, t, d = x.shape
    tok = lambda w, col: pl.BlockSpec((None, tm, w), lambda bb, i: (bb, i, col // w))
    const = lambda shape: pl.BlockSpec(shape, lambda bb, i: (0,) * len(shape))
    return pl.pallas_call(
        functools.partial(_merge_kernel, mod_row=mod_row, d=d, final=final),
        grid=(b, t // tm),
        in_specs=[
            tok(d, 0), tok(512, 0), tok(512, 0), tok(512, 0),
            tok(512, COL_ZA), tok(512, COL_ZB), tok(512, COL_ZC),
            tok(1024, COL_GT), tok(1024, COL_GT + 1024), tok(1024, COL_GT + 2048),
            const((8, 3 * d)), const((BRANCH_W, 2 * BRANCH_W)), const((1, 2 * BRANCH_W)),
            const((3, BRANCH_W, d)), const((d, d)), const((1, d)),
        ],
        out_specs=tok(d, 0),
        out_shape=jax.ShapeDtypeStruct((b, t, d), F32),
        compiler_params=_cparams(("arbitrary", "arbitrary")),
        name="merge_out",
    )(x, ya, yb, yc, p, p, p, p, p, p, mod, glu_w, glu_b.reshape(1, -1), w_branch, w_out, final_g.reshape(1, d))


def kernel(x, c, ctx, c_ctx, ada_w, ada_b, norm_g, w_in, s5_lam_re, s5_lam_im, s5_log_dt, s5_b_re, s5_b_im,
           s5_c_re, s5_c_im, s5_d, s5_glu_w, s5_glu_b, na_rpb, hy_conv_w, hy_conv_b, hf_w1, hf_b1, hf_freq,
           hf_w2, hf_b2, hf_w3, hy_d, w_branch, w_out, final_g):
    bsz, t_lat, d = x.shape
    t_ctx = ctx.shape[1]
    depth = ada_w.shape[0]
    rows_n = t_lat // GRID_W
    assert bsz == 2 and d == 1024 and t_lat % (16 * FFT_N2) == 0 and t_ctx % 128 == 0

    cvec = jnp.zeros((8, d), F32).at[:bsz].set(c).at[bsz].set(c_ctx)
    mods = _modulation(cvec, ada_w, ada_b)
    cos_t, sin_t = _rope_tables(t_lat)
    n1 = 2 * t_lat // FFT_N2
    dft = _dft_consts(n1)
    tm_lat = 1024 if t_lat % 1024 == 0 else 512

    x_lat, x_ctx = x, ctx
    for l in range(depth):
        ctx_out = l < depth - 1
        half_cols = jnp.asarray(np.where(np.isin(np.arange(IN_COLS) // BRANCH_W, (1, 5, 9)) | (np.arange(IN_COLS) >= COL_GT),
                                         0.5, 1.0).astype(np.float32))
        w_bf = (w_in[l] * half_cols).astype(BF16)
        p_lat = _inproj(x_lat, mods[l], norm_g[l], w_bf, mod_row=None, tm=tm_lat)
        p_ctx = _inproj(x_ctx, mods[l], norm_g[l], w_bf, mod_row=bsz, tm=t_ctx)

        ops = _s5_operators(s5_lam_re[l], s5_lam_im[l], s5_log_dt[l], s5_b_re[l], s5_b_im[l],
                            s5_c_re[l], s5_c_im[l], s5_d[l])
        ya, ya_c = _s5_mixer(p_lat, p_ctx, ops)

        qr, kt = _rope_qk(p_lat, cos_t, sin_t)
        bias = _na_bias_tables(na_rpb[l])
        yb = _na_mixer(qr, kt, p_lat, p_ctx, bias)

        k_cm, ss = _hyena_filter_cm(t_lat, hf_w1[l], hf_b1[l], hf_freq[l], hf_w2[l], hf_b2[l], hf_w3[l])
        kf = _hyena_filter_spectrum(k_cm, ss, dft)
        hy_cm = _to_channel_major(p_lat, COL_HY, 3 * BRANCH_W)
        yc_cm = _hyena_lat(hy_cm, kf, hy_conv_w[l], hy_conv_b[l], hy_d[l], dft)
        yc = _to_token_major(yc_cm, bsz)

        wb_bf = (0.5 * w_branch[l]).astype(BF16)
        wo_bf = w_out[l].astype(BF16)
        gw_bf = (0.5 * s5_glu_w[l]).astype(BF16)
        glu_b_half = 0.5 * s5_glu_b[l]
        x_lat_new = _merge(x_lat, ya, yb, yc, p_lat, mods[l], gw_bf, glu_b_half, wb_bf, wo_bf,
                           final_g, mod_row=None, tm=512, final=not ctx_out)
        if ctx_out:
            yb_c = _ctx_attention(p_ctx)
            kc_time, ss_c = _hyena_filter_time(t_ctx, hf_w1[l], hf_b1[l], hf_freq[l], hf_w2[l], hf_b2[l], hf_w3[l])
            yc_c = _hyena_ctx(p_ctx[:, :, COL_HY:COL_HY + 1536], kc_time, ss_c, hy_conv_w[l], hy_conv_b[l], hy_d[l])
            x_ctx = _merge(x_ctx, ya_c, yb_c, yc_c, p_ctx, mods[l], gw_bf, glu_b_half, wb_bf, wo_bf,
                           final_g, mod_row=bsz, tm=t_ctx, final=False)
        x_lat = x_lat_new
    return x_lat
```

```python
import functools
import math

import numpy as np
import jax
import jax.numpy as jnp
from jax import lax
from jax.experimental import pallas as pl
from jax.experimental.pallas import tpu as pltpu

F32 = jnp.float32
BF16 = jnp.bfloat16
HIGHEST = lax.Precision.HIGHEST
HIGH = lax.Precision.HIGH

GRID_W = 64
BRANCH_W = 512
S5_P = 16
S5_N = 64
S5_G = BRANCH_W // S5_P
S5_TC = 16
HEAD_DIM = 64
NA_HEADS = BRANCH_W // HEAD_DIM
NA_KR = 8
NA_KW = 16
ROPE_HALF = 16
ROPE_THETA = 10000.0
HY_EMB = 33
HY_BANDS = (HY_EMB - 1) // 2
HY_HID = 64
HY_MIN_DECAY = math.log(1e-2) / 1.5
HY_MAX_DECAY = math.log(1e-2) / 0.3
EPS = 1e-6
FFT_N2 = 256
NEG_BIG = -1e30
LOG2E = 1.4426950408889634
VMEM_LIMIT = 52 * 1024 * 1024

COL_UA, COL_ZA, COL_Q, COL_K, COL_V, COL_ZB, COL_HY, COL_ZC, COL_GT = (
    0, 512, 1024, 1536, 2048, 2560, 3072, 4608, 5120)
IN_COLS = 8192


def _cparams(sem):
    return pltpu.CompilerParams(dimension_semantics=sem, vmem_limit_bytes=VMEM_LIMIT)


def _sigmoid(x):
    return 0.5 * jnp.tanh(0.5 * x) + 0.5


def _silu(x):
    return x * _sigmoid(x)


def _mod_kernel(c_ref, w_ref, b_ref, o_ref):
    s = _silu(c_ref[...])
    o_ref[...] = jnp.dot(s, w_ref[...], preferred_element_type=F32, precision=HIGHEST) + b_ref[...]


def _modulation(cvec, ada_w, ada_b):
    depth, d, d3 = ada_w.shape
    tn = 1024
    return pl.pallas_call(
        _mod_kernel,
        grid=(depth, d3 // tn),
        in_specs=[
            pl.BlockSpec((8, d), lambda l, j: (0, 0)),
            pl.BlockSpec((None, d, tn), lambda l, j: (l, 0, j)),
            pl.BlockSpec((None, 1, tn), lambda l, j: (l, 0, j)),
        ],
        out_specs=pl.BlockSpec((None, 8, tn), lambda l, j: (l, 0, j)),
        out_shape=jax.ShapeDtypeStruct((depth, 8, d3), F32),
        compiler_params=_cparams(("arbitrary", "arbitrary")),
        name="adaln_mod",
    )(cvec, ada_w, ada_b.reshape(depth, 1, d3))


def _inproj_kernel(x_ref, mod_ref, g_ref, w_ref, o_ref, h_ref, *, mod_row, d):
    @pl.when(pl.program_id(2) == 0)
    def _():
        x = x_ref[...]
        ms = jnp.mean(x * x, axis=-1, keepdims=True)
        y = x * lax.rsqrt(ms + EPS) * g_ref[...]
        if mod_row is None:
            m = mod_ref[pl.ds(pl.program_id(0), 1), :]
        else:
            m = mod_ref[mod_row:mod_row + 1, :]
        h_ref[...] = (y * (1.0 + m[:, d:2 * d]) + m[:, :d]).astype(BF16)

    o_ref[...] = jnp.dot(h_ref[...], w_ref[...], preferred_element_type=F32).astype(o_ref.dtype)


def _inproj(x, mod, g, w_bf, *, mod_row, tm):
    b, t, d = x.shape
    n = w_bf.shape[1]
    tn = 2048
    return pl.pallas_call(
        functools.partial(_inproj_kernel, mod_row=mod_row, d=d),
        grid=(b, t // tm, n // tn),
        in_specs=[
            pl.BlockSpec((None, tm, d), lambda bb, i, j: (bb, i, 0)),
            pl.BlockSpec((8, 3 * d), lambda bb, i, j: (0, 0)),
            pl.BlockSpec((1, d), lambda bb, i, j: (0, 0)),
            pl.BlockSpec((d, tn), lambda bb, i, j: (0, j)),
        ],
        out_specs=pl.BlockSpec((None, tm, tn), lambda bb, i, j: (bb, i, j)),
        out_shape=jax.ShapeDtypeStruct((b, t, n), BF16),
        scratch_shapes=[pltpu.VMEM((tm, d), BF16)],
        compiler_params=_cparams(("arbitrary", "arbitrary", "arbitrary")),
        name="norm_inproj",
    )(x, mod, g.reshape(1, d), w_bf)


def _s5_operators(lam_re, lam_im, log_dt, b_re, b_im, c_re, c_im, d_skip):
    tc, p, n, g = S5_TC, S5_P, S5_N, S5_G
    dt = jnp.exp(log_dt)[..., None]
    er = lam_re * dt
    ei = lam_im * dt
    k = jnp.arange(tc + 1, dtype=F32)[:, None, None, None]
    mag = jnp.exp(k * er[None])
    pw_r = mag * jnp.cos(k * ei[None])
    pw_i = mag * jnp.sin(k * ei[None])
    lb_r, lb_i = pw_r[1], pw_i[1]
    den = lam_re * lam_re + lam_im * lam_im
    q_r = ((lb_r - 1.0) * lam_re + lb_i * lam_im) / den
    q_i = (lb_i * lam_re - (lb_r - 1.0) * lam_im) / den
    bb_r = q_r[..., None] * b_re - q_i[..., None] * b_im
    bb_i = q_r[..., None] * b_im + q_i[..., None] * b_re

    ar, ai = pw_r[tc], pw_i[tc]
    a1 = jnp.concatenate([ar, ar], -1)
    a2 = jnp.concatenate([-ai, ai], -1)
    a3 = jnp.concatenate([ai, -ai], -1)

    grp = lambda a: a.transpose(1, 0, 2, 3)
    pad_t = ((0, 0), (0, 0), (0, 24 - (tc + 1)), (0, 0))
    blk = lambda rows, cols: pl.BlockSpec((None, 2, rows, cols), lambda i: (i, 0, 0, 0))
    wcat, ccat = pl.pallas_call(
        _s5_ops_kernel,
        grid=(g,),
        in_specs=[blk(24, n), blk(24, n), blk(p, n), blk(p, n), blk(p, n), blk(p, n),
                  pl.BlockSpec((None, p, 1), lambda i: (i, 0, 0))],
        out_specs=[pl.BlockSpec((None, tc * p, 768), lambda i: (i, 0, 0)),
                   pl.BlockSpec((None, 4 * n, tc * p), lambda i: (i, 0, 0))],
        out_shape=[jax.ShapeDtypeStruct((g, tc * p, 768), BF16), jax.ShapeDtypeStruct((g, 4 * n, tc * p), BF16)],
        compiler_params=_cparams(("arbitrary",)),
        name="s5_operators",
    )(jnp.pad(pw_r.transpose(2, 1, 0, 3), pad_t), jnp.pad(pw_i.transpose(2, 1, 0, 3), pad_t),
      grp(c_re), grp(c_im), grp(bb_r.transpose(0, 1, 3, 2)), grp(bb_i.transpose(0, 1, 3, 2)),
      d_skip.reshape(g, p, 1))
    return wcat, ccat, a1, a2, a3


def _s5_ops_kernel(pwr_ref, pwi_ref, cr_ref, ci_ref, br_ref, bi_ref, d_ref, w_ref, c_ref):
    tc, p, n = S5_TC, S5_P, S5_N
    nt = (((1,), (1,)), ((), ()))
    lane = lax.broadcasted_iota(jnp.int32, (p, tc * p), 1)
    row = lax.broadcasted_iota(jnp.int32, (p, tc * p), 0)
    eye_n = (lax.broadcasted_iota(jnp.int32, (n, n), 0) == lax.broadcasted_iota(jnp.int32, (n, n), 1)).astype(F32)

    def c_lam(d, taus):
        pr = jnp.concatenate([jnp.broadcast_to(pwr_ref[d, t:t + 1, :], (p, n)) for t in taus], axis=0)
        pi = jnp.concatenate([jnp.broadcast_to(pwi_ref[d, t:t + 1, :], (p, n)) for t in taus], axis=0)
        cr = jnp.concatenate([cr_ref[d]] * len(taus), axis=0)
        ci = jnp.concatenate([ci_ref[d]] * len(taus), axis=0)
        return cr * pr - ci * pi, cr * pi + ci * pr

    def lag_row(d, taus):
        clr, cli = c_lam(d, taus)
        return (lax.dot_general(br_ref[d], clr, nt, preferred_element_type=F32, precision=HIGHEST)
                - lax.dot_general(bi_ref[d], cli, nt, preferred_element_type=F32, precision=HIGHEST))

    rf = lag_row(0, list(range(tc))) + jnp.where(lane == row, d_ref[...], 0.0)
    rb = lag_row(1, list(range(tc - 1, -1, -1)))
    for ti in range(tc):
        sf = p * ti
        sb = p * (tc - 1 - ti)
        a = jnp.where(lane >= sf, pltpu.roll(rf, sf, 1) if sf else rf, 0.0)
        b = jnp.where(lane < tc * p - sb, pltpu.roll(rb, tc * p - sb, 1) if sb else rb, 0.0)
        w_ref[ti * p:(ti + 1) * p, 0:tc * p] = (a + b).astype(BF16)

    for d, col in ((0, tc * p), (1, tc * p + 4 * n)):
        for t in range(tc):
            e = tc - 1 - t if d == 0 else t
            pr = pwr_ref[d, e:e + 1, :]
            pi = pwi_ref[d, e:e + 1, :]
            sr = br_ref[d] * pr - bi_ref[d] * pi
            si = br_ref[d] * pi + bi_ref[d] * pr
            w_ref[t * p:(t + 1) * p, col:col + 4 * n] = jnp.concatenate([sr, si, si, sr], axis=1).astype(BF16)

    for d, taus in ((0, list(range(1, tc + 1))), (1, list(range(tc, 0, -1)))):
        clr, cli = c_lam(d, taus)
        c_ref[2 * n * d:2 * n * d + n, :] = lax.dot_general(
            eye_n, clr, nt, preferred_element_type=F32, precision=HIGHEST).astype(BF16)
        c_ref[2 * n * d + n:2 * n * (d + 1), :] = lax.dot_general(
            eye_n, -cli, nt, preferred_element_type=F32, precision=HIGHEST).astype(BF16)


def _s5_kernel(ul_ref, uc_ref, w_ref, c_ref, a1_ref, a2_ref, a3_ref, yl_ref, yc_ref,
               tok, xg, spf, sqf, spb, sqb, *, gb, nch, nch_lat, pitch):
    t_lat = nch_lat * S5_TC
    tok[:t_lat, :] = ul_ref[...].astype(F32)
    tok[t_lat:, :] = uc_ref[...].astype(F32)

    rt = next(r for r in (80, 40, 16, 8) if nch % r == 0)
    lane_rt = lax.broadcasted_iota(jnp.int32, (rt, 128), 1) // S5_P

    def block_transpose(a):
        for d in (4, 2, 1):
            keep = (lane_rt & d) == 0
            nxt = list(a)
            for i in range(8):
                if i & d == 0:
                    j = i + d
                    nxt[i] = jnp.where(keep, a[i], pltpu.roll(a[j], d * S5_P, 1))
                    nxt[j] = jnp.where(keep, pltpu.roll(a[i], 128 - d * S5_P, 1), a[j])
            a = nxt
        return a

    def to_compact(it, carry):
        r0 = pl.multiple_of(it * rt, rt)
        for h in range(2):
            z = [tok[pl.ds(r0 * S5_TC + 8 * h + t, rt, stride=S5_TC), :] for t in range(8)]
            b = block_transpose(z)
            for g in range(gb):
                xg[g, pl.ds(r0, rt), h * 128:(h + 1) * 128] = b[g]
        return carry

    lax.fori_loop(0, nch // rt, to_compact, 0)

    for g in range(gb):
        r = jnp.dot(xg[g].astype(BF16), w_ref[g], preferred_element_type=F32)
        xg[g] = r[:, :256]
        spf[g * pitch:g * pitch + nch, :] = r[:, 256:384]
        sqf[g * pitch:g * pitch + nch, :] = r[:, 384:512]
        spb[g * pitch:g * pitch + nch, :] = r[:, 512:640]
        sqb[g * pitch:g * pitch + nch, :] = r[:, 640:768]

    a1f, a2f, a3f = a1_ref[0], a2_ref[0], a3_ref[0]
    a1b, a2b, a3b = a1_ref[1], a2_ref[1], a3_ref[1]
    nch_ctx = nch - nch_lat

    def body(s, carry):
        pf, qf, pb, qb = carry
        cf = jnp.where(s < nch_ctx, s + nch_lat, s - nch_ctx)
        cb = nch - 1 - s
        idx_f = pl.ds(cf, gb, stride=pitch)
        idx_b = pl.ds(cb, gb, stride=pitch)
        sp = spf[idx_f, :]
        sq = sqf[idx_f, :]
        spf[idx_f, :] = pf
        pf, qf = pf * a1f + qf * a2f + sp, qf * a1f + pf * a3f + sq
        sp = spb[idx_b, :]
        sq = sqb[idx_b, :]
        spb[idx_b, :] = pb
        pb, qb = pb * a1b + qb * a2b + sp, qb * a1b + pb * a3b + sq
        return pf, qf, pb, qb

    z0 = jnp.zeros((gb, 128), F32)
    lax.fori_loop(0, nch, body, (z0, z0, z0, z0), unroll=4)

    for g in range(gb):
        hf = spf[g * pitch:g * pitch + nch, :].astype(BF16)
        hb = spb[g * pitch:g * pitch + nch, :].astype(BF16)
        cm = c_ref[g]
        y = (jnp.dot(hf, cm[:128, :], preferred_element_type=F32)
             + jnp.dot(hb, cm[128:, :], preferred_element_type=F32))
        xg[g] = xg[g] + y

    def to_tokens(it, carry):
        r0 = pl.multiple_of(it * rt, rt)
        for h in range(2):
            y = [xg[g, pl.ds(r0, rt), h * 128:(h + 1) * 128] for g in range(gb)]
            b = block_transpose(y)
            for t in range(8):
                tok[pl.ds(r0 * S5_TC + 8 * h + t, rt, stride=S5_TC), :] = b[t]
        return carry

    lax.fori_loop(0, nch // rt, to_tokens, 0)
    yl_ref[...] = tok[:t_lat, :].astype(yl_ref.dtype)
    yc_ref[...] = tok[t_lat:, :].astype(yc_ref.dtype)


def _s5_mixer(p_lat, p_ctx, ops):
    wcat, ccat, a1, a2, a3 = ops
    b, t_lat, _ = p_lat.shape
    t_ctx = p_ctx.shape[1]
    nch, nch_lat = (t_lat + t_ctx) // S5_TC, t_lat // S5_TC
    gb = 128 // S5_P
    pitch = ((nch + 7) // 8) * 8 + 8
    kern = functools.partial(_s5_kernel, gb=gb, nch=nch, nch_lat=nch_lat, pitch=pitch)
    once = pl.Buffered(1)
    return pl.pallas_call(
        kern,
        grid=(b, S5_G // gb),
        in_specs=[
            pl.BlockSpec((None, t_lat, 128), lambda bb, i: (bb, 0, COL_UA // 128 + i), pipeline_mode=once),
            pl.BlockSpec((None, t_ctx, 128), lambda bb, i: (bb, 0, COL_UA // 128 + i)),
            pl.BlockSpec((gb, 256, 768), lambda bb, i: (i, 0, 0)),
            pl.BlockSpec((gb, 256, 256), lambda bb, i: (i, 0, 0)),
            pl.BlockSpec((2, gb, 128), lambda bb, i: (0, i, 0)),
            pl.BlockSpec((2, gb, 128), lambda bb, i: (0, i, 0)),
            pl.BlockSpec((2, gb, 128), lambda bb, i: (0, i, 0)),
        ],
        out_specs=[
            pl.BlockSpec((None, t_lat, 128), lambda bb, i: (bb, 0, i), pipeline_mode=once),
            pl.BlockSpec((None, t_ctx, 128), lambda bb, i: (bb, 0, i)),
        ],
        out_shape=[jax.ShapeDtypeStruct((b, t_lat, BRANCH_W), BF16),
                   jax.ShapeDtypeStruct((b, t_ctx, BRANCH_W), BF16)],
        scratch_shapes=[pltpu.VMEM((t_lat + t_ctx, 128), F32), pltpu.VMEM((gb, nch, 256), F32)]
        + [pltpu.VMEM((gb * pitch, 128), F32) for _ in range(4)],
        compiler_params=_cparams(("arbitrary", "arbitrary")),
        name="s5_chunk_scan",
    )(p_lat, p_ctx, wcat, ccat, a1, a2, a3)


def _rope_tables(t_len):
    t = jnp.arange(t_len)
    rows = (t // GRID_W).astype(F32)
    cols = (t % GRID_W).astype(F32)
    inv = ROPE_THETA ** (-jnp.arange(ROPE_HALF, dtype=F32) / ROPE_HALF)
    lane = np.arange(128)
    dd = lane % HEAD_DIM
    fi = dd % ROPE_HALF
    use_row = jnp.asarray(dd < 32)
    sign = jnp.asarray(np.where((dd % 32) < ROPE_HALF, -1.0, 1.0).astype(np.float32))
    ang = jnp.where(use_row[None, :], rows[:, None], cols[:, None]) * inv[fi][None, :]
    return jnp.cos(ang), jnp.sin(ang) * sign[None, :]


def _rope_kernel(q_ref, k_ref, cos_ref, sin_ref, qo_ref, kt_ref, *, scale):
    cos = jnp.concatenate([cos_ref[...]] * 4, axis=1)
    sin = jnp.concatenate([sin_ref[...]] * 4, axis=1)
    lane = lax.broadcasted_iota(jnp.int32, cos.shape, 1)
    low = (lane % 32) < ROPE_HALF

    def rot(x):
        n = x.shape[1]
        partner = jnp.where(low, pltpu.roll(x, n - ROPE_HALF, 1), pltpu.roll(x, ROPE_HALF, 1))
        return x * cos + partner * sin

    qo_ref[...] = (rot(q_ref[...].astype(F32)) * scale).astype(BF16)
    kr = rot(k_ref[...].astype(F32))
    for c in range(kt_ref.shape[0]):
        kt_ref[c] = kr[c * 128:(c + 1) * 128, :].T.astype(BF16)


def _rope_qk(p_lat, cos_t, sin_t):
    b, t, _ = p_lat.shape
    tm = 1024
    return pl.pallas_call(
        functools.partial(_rope_kernel, scale=HEAD_DIM ** -0.5 * LOG2E),
        grid=(b, t // tm),
        in_specs=[
            pl.BlockSpec((None, tm, 512), lambda bb, i: (bb, i, COL_Q // 512)),
            pl.BlockSpec((None, tm, 512), lambda bb, i: (bb, i, COL_K // 512)),
            pl.BlockSpec((tm, 128), lambda bb, i: (i, 0)),
            pl.BlockSpec((tm, 128), lambda bb, i: (i, 0)),
        ],
        out_specs=[pl.BlockSpec((None, tm, 512), lambda bb, i: (bb, i, 0)),
                   pl.BlockSpec((None, tm // 128, 512, 128), lambda bb, i: (bb, i, 0, 0))],
        out_shape=[jax.ShapeDtypeStruct((b, t, BRANCH_W), BF16),
                   jax.ShapeDtypeStruct((b, t // 128, BRANCH_W, 128), BF16)],
        compiler_params=_cparams(("arbitrary", "arbitrary")),
        name="rope_qk",
    )(p_lat, p_lat, cos_t, sin_t)


_NA_VARIANTS = (((4, 5), (0, 1)), ((0, 1), (0, 0)), ((2, 3), (0, 0)), ((4, 5), (0, 0)), ((6, 7), (0, 0)))
NA_WIN = NA_KR + 2


def _na_bias_tables(rpb):
    h = rpb.shape[0]
    qc = np.arange(GRID_W)
    kc = np.arange(GRID_W)
    cs = np.clip(qc - NA_KW // 2, 0, GRID_W - NA_KW)
    valid_c = (kc[None, :] >= cs[:, None]) & (kc[None, :] < cs[:, None] + NA_KW)
    rel_c = np.clip(kc[None, :] - qc[:, None] + NA_KW - 1, 0, 2 * NA_KW - 2)
    oh_c = np.eye(2 * NA_KW - 1, dtype=np.float32)[rel_c]
    j = np.arange(NA_WIN)
    oh_r = np.zeros((len(_NA_VARIANTS), 2, NA_WIN, 2 * NA_KR - 1), np.float32)
    valid_r = np.zeros((len(_NA_VARIANTS), 2, NA_WIN), bool)
    for n, (offs, los) in enumerate(_NA_VARIANTS):
        for rr in range(2):
            valid_r[n, rr] = (j >= los[rr]) & (j < los[rr] + NA_KR)
            rel_r = np.clip(j - offs[rr] + NA_KR - 1, 0, 2 * NA_KR - 2)
            oh_r[n, rr] = np.eye(2 * NA_KR - 1, dtype=np.float32)[rel_r]
    bias = jnp.einsum("nrja,hab,ckb->nhrcjk", jnp.asarray(oh_r), rpb, jnp.asarray(oh_c), precision=HIGH) * LOG2E
    valid = valid_r[:, None, :, None, :, None] & valid_c[None, None, None, :, None, :]
    bias = jnp.where(jnp.asarray(valid), bias, NEG_BIG)
    return bias.reshape(len(_NA_VARIANTS), h // 2, 4 * GRID_W, NA_WIN * GRID_W)


def _na_kernel(q_ref, kp_ref, kc_ref, kn_ref, vp_ref, vc_ref, vn_ref, kx_ref, vx_ref, bias_ref, o_ref,
               kt3, vcat, *, ni):
    half = 4 * GRID_W
    tq = 8 * GRID_W
    kt3[0:2] = kp_ref[2:4]
    kt3[2:6] = kc_ref[...]
    kt3[6:8] = kn_ref[0:2]
    vcat[:half, :] = vp_ref[half:, :]
    vcat[half:half + tq, :] = vc_ref[...]
    vcat[half + tq:, :] = vn_ref[:half, :]
    kxt = kx_ref[...].astype(F32).T.astype(BF16)
    i = pl.program_id(1)
    head0 = lax.broadcasted_iota(jnp.int32, (2 * GRID_W, 128), 1) < HEAD_DIM
    nchunk = NA_WIN // 2

    def pair_body(a, carry):
        first = i == 0
        last = i == ni - 1
        s0 = jnp.where(first, jnp.maximum(2 * a, 4), jnp.where(last, jnp.minimum(2 * a, 4), 2 * a))
        var = jnp.where(first & (a < 2), 1 + a, jnp.where(last & (a >= 2), 1 + a, 0))
        q0 = pl.multiple_of(a * 2 * GRID_W, 2 * GRID_W)
        k0 = pl.multiple_of(s0 * GRID_W, 2 * GRID_W)
        sp = s0 // 2
        nhp = NA_HEADS // 2
        lss = [slice(hp * 128, (hp + 1) * 128) for hp in range(nhp)]
        scores = []
        for hp in range(nhp):
            qp = q_ref[pl.ds(q0, 2 * GRID_W), lss[hp]]
            zero = jnp.zeros_like(qp)
            q2 = jnp.concatenate([jnp.where(head0, qp, zero), jnp.where(head0, zero, qp)], axis=0)
            kw = kt3[pl.ds(sp, nchunk), lss[hp], :]
            s_loc = jnp.concatenate(
                [jnp.dot(q2, kw[c], preferred_element_type=F32) for c in range(nchunk)], axis=1) + bias_ref[var, hp]
            s_ctx = jnp.dot(q2, kxt[lss[hp], :], preferred_element_type=F32)
            scores.append((s_loc, s_ctx))
        probs = []
        for s_loc, s_ctx in scores:
            m = jnp.maximum(jnp.max(s_loc, axis=-1, keepdims=True), jnp.max(s_ctx, axis=-1, keepdims=True))
            p_loc = jnp.exp2(s_loc - m)
            p_ctx = jnp.exp2(s_ctx - m)
            den = jnp.sum(p_loc, axis=-1, keepdims=True) + jnp.sum(p_ctx, axis=-1, keepdims=True)
            probs.append((p_loc.astype(BF16), p_ctx.astype(BF16), den))
        outs = []
        for hp in range(nhp):
            p_loc, p_ctx, den = probs[hp]
            vw = vcat[pl.ds(k0, NA_WIN * GRID_W), lss[hp]]
            o = (jnp.dot(p_loc, vw, preferred_element_type=F32)
                 + jnp.dot(p_ctx, vx_ref[:, lss[hp]], preferred_element_type=F32))
            o = o / den
            outs.append(jnp.where(head0, o[:2 * GRID_W], o[2 * GRID_W:]))
        o_ref[pl.ds(q0, 2 * GRID_W), :] = jnp.concatenate(outs, axis=1).astype(o_ref.dtype)
        return carry

    lax.fori_loop(0, 4, pair_body, 0, unroll=2)


def _na_mixer(qr, kt, p_lat, p_ctx, bias):
    b, t, _ = qr.shape
    nc = p_ctx.shape[1]
    tq = 8 * GRID_W
    ni = t // tq
    assert ni >= 2
    blk = lambda off, col: pl.BlockSpec(
        (None, tq, BRANCH_W), lambda bb, i: (bb, jnp.clip(i + off, 0, ni - 1), col))
    ktb = lambda off: pl.BlockSpec(
        (None, tq // 128, BRANCH_W, 128), lambda bb, i: (bb, jnp.clip(i + off, 0, ni - 1), 0, 0))
    cv = COL_V // BRANCH_W
    return pl.pallas_call(
        functools.partial(_na_kernel, ni=ni),
        grid=(b, ni),
        in_specs=[
            blk(0, 0), ktb(-1), ktb(0), ktb(1),
            blk(-1, cv), blk(0, cv), blk(1, cv),
            pl.BlockSpec((None, nc, BRANCH_W), lambda bb, i: (bb, 0, COL_K // BRANCH_W)),
            pl.BlockSpec((None, nc, BRANCH_W), lambda bb, i: (bb, 0, cv)),
            pl.BlockSpec(bias.shape, lambda bb, i: (0, 0, 0, 0), pipeline_mode=pl.Buffered(1)),
        ],
        out_specs=pl.BlockSpec((None, tq, BRANCH_W), lambda bb, i: (bb, i, 0)),
        out_shape=jax.ShapeDtypeStruct((b, t, BRANCH_W), BF16),
        scratch_shapes=[pltpu.VMEM((2 * tq // 128, BRANCH_W, 128), BF16), pltpu.VMEM((2 * tq, BRANCH_W), BF16)],
        compiler_params=_cparams(("arbitrary", "arbitrary")),
        name="na_attention",
    )(qr, kt, kt, kt, p_lat, p_lat, p_lat, p_ctx, p_ctx, bias)


def _ctx_attn_kernel(q_ref, k_ref, v_ref, o_ref, *, scale):
    dn = (((1,), (1,)), ((), ()))
    outs = []
    for hh in range(2):
        sl = slice(hh * HEAD_DIM, (hh + 1) * HEAD_DIM)
        qh = (q_ref[:, sl].astype(F32) * scale).astype(BF16)
        s = lax.dot_general(qh, k_ref[:, sl], dn, preferred_element_type=F32)
        m = jnp.max(s, axis=-1, keepdims=True)
        p = jnp.exp(s - m)
        den = jnp.sum(p, axis=-1, keepdims=True)
        outs.append(jnp.dot(p.astype(BF16), v_ref[:, sl], preferred_element_type=F32) / den)
    o_ref[...] = jnp.concatenate(outs, axis=1).astype(o_ref.dtype)


def _ctx_attention(p_ctx):
    b, nc, _ = p_ctx.shape
    spec = lambda col: pl.BlockSpec((None, nc, 128), lambda bb, hp: (bb, 0, col // 128 + hp))
    return pl.pallas_call(
        functools.partial(_ctx_attn_kernel, scale=HEAD_DIM ** -0.5),
        grid=(b, NA_HEADS // 2),
        in_specs=[spec(COL_Q), spec(COL_K), spec(COL_V)],
        out_specs=pl.BlockSpec((None, nc, 128), lambda bb, hp: (bb, 0, hp)),
        out_shape=jax.ShapeDtypeStruct((b, nc, BRANCH_W), BF16),
        compiler_params=_cparams(("arbitrary", "arbitrary")),
        name="ctx_attention",
    )(p_ctx, p_ctx, p_ctx)


def _filter_features(t_len):
    pos = jnp.arange(t_len, dtype=F32)
    t = pos / max(t_len - 1, 1)
    w = 2.0 * math.pi * pos / t_len
    bands = jnp.linspace(1e-4, HY_BANDS - 1, HY_BANDS, dtype=F32)
    feats = jnp.concatenate([t[:, None], jnp.cos(w[:, None] * bands), -jnp.sin(w[:, None] * bands)], axis=-1)
    feats2 = jnp.concatenate([feats, feats[:1], feats[:0:-1]], axis=0)
    return jnp.pad(feats2, ((0, 0), (0, 128 - HY_EMB)))


def _filter_kernel(f_ref, w1_ref, b1_ref, fr_ref, w2_ref, b2_ref, w3_ref, dl_ref, k_ref, ss_ref):
    f = f_ref[...]
    fr = fr_ref[...]
    h = jnp.sin(fr * (jnp.dot(f, w1_ref[...], preferred_element_type=F32, precision=HIGHEST) + b1_ref[...]))
    h = jnp.sin(fr * (jnp.dot(h, w2_ref[...], preferred_element_type=F32, precision=HIGHEST) + b2_ref[...]))
    k = jnp.dot(h, w3_ref[...], preferred_element_type=F32, precision=HIGHEST)
    k = k * jnp.exp(-f[:, 0:1] * dl_ref[...])
    k_ref[...] = k

    @pl.when(pl.program_id(0) == 0)
    def _():
        ss_ref[...] = jnp.zeros_like(ss_ref)

    ss_ref[...] += jnp.sum(k * k, axis=0, keepdims=True)


def _hyena_filter_time(t_len, w1, b1, freq, w2, b2, w3):
    n = 2 * t_len
    tm = min(1024, t_len)
    feats = _filter_features(t_len)
    w1p = jnp.pad(w1, ((0, 128 - HY_EMB), (0, 0)))
    deltas = jnp.abs(jnp.linspace(HY_MIN_DECAY, HY_MAX_DECAY, BRANCH_W, dtype=F32))
    dl = jnp.concatenate([deltas, deltas]).reshape(1, 2 * BRANCH_W)
    half = t_len // tm
    nco = 2 * BRANCH_W
    return pl.pallas_call(
        _filter_kernel,
        grid=(n // tm,),
        in_specs=[
            pl.BlockSpec((tm, 128), lambda i: (i, 0)),
            pl.BlockSpec((128, HY_HID), lambda i: (0, 0)),
            pl.BlockSpec((1, HY_HID), lambda i: (0, 0)),
            pl.BlockSpec((1, HY_HID), lambda i: (0, 0)),
            pl.BlockSpec((HY_HID, HY_HID), lambda i: (0, 0)),
            pl.BlockSpec((1, HY_HID), lambda i: (0, 0)),
            pl.BlockSpec((HY_HID, nco), lambda i: (0, i // half)),
            pl.BlockSpec((1, nco), lambda i: (0, 0)),
        ],
        out_specs=[pl.BlockSpec((tm, nco), lambda i: (i, 0)), pl.BlockSpec((1, nco), lambda i: (0, 0))],
        out_shape=[jax.ShapeDtypeStruct((n, nco), F32), jax.ShapeDtypeStruct((1, nco), F32)],
        compiler_params=_cparams(("arbitrary",)),
        name="hyena_filter_ffn_ctx",
    )(feats, w1p, b1.reshape(1, -1), freq.reshape(1, -1), w2, b2.reshape(1, -1), w3, dl)


def _filter_cm_kernel(f_ref, t_ref, w1_ref, b1_ref, fr_ref, w2_ref, b2_ref, w3_ref, dl_ref, k_ref, ss_ref):
    fr = fr_ref[...]
    h = jnp.sin(fr * (jnp.dot(w1_ref[...], f_ref[...], preferred_element_type=F32, precision=HIGHEST) + b1_ref[...]))
    h = jnp.sin(fr * (jnp.dot(w2_ref[...], h, preferred_element_type=F32, precision=HIGHEST) + b2_ref[...]))
    k = lax.dot_general(h.astype(BF16), w3_ref[...].astype(BF16), (((0,), (0,)), ((), ())), preferred_element_type=F32)
    k = k * jnp.exp(-t_ref[...] * dl_ref[...])
    rows = k.shape[0] // FFT_N2
    for cb in range(k.shape[1] // 128):
        piece = k[:, cb * 128:(cb + 1) * 128].reshape(rows, FFT_N2, 128)
        k_ref[cb * 128:(cb + 1) * 128] = jnp.swapaxes(jnp.swapaxes(piece, 1, 2), 0, 1)

    @pl.when(pl.program_id(0) == 0)
    def _():
        ss_ref[...] = jnp.zeros_like(ss_ref)

    ss_ref[...] += jnp.sum(k * k, axis=0, keepdims=True)


def _hyena_filter_cm(t_len, w1, b1, freq, w2, b2, w3):
    n = 2 * t_len
    tm = 2048
    assert t_len % tm == 0
    slot = jnp.arange(n)
    pos = jnp.where(slot < t_len, slot, jnp.where(slot == t_len, 0, n - slot)).astype(F32)
    t = pos / max(t_len - 1, 1)
    w = 2.0 * math.pi * pos / t_len
    bands = jnp.linspace(1e-4, HY_BANDS - 1, HY_BANDS, dtype=F32)
    feats = jnp.concatenate([t[None, :], jnp.cos(w[None, :] * bands[:, None]), -jnp.sin(w[None, :] * bands[:, None])],
                            axis=0)
    feats = jnp.pad(feats, ((0, 128 - HY_EMB), (0, 0)))
    w1t = jnp.pad(w1, ((0, 128 - HY_EMB), (0, 0))).T
    deltas = jnp.abs(jnp.linspace(HY_MIN_DECAY, HY_MAX_DECAY, BRANCH_W, dtype=F32))
    nco = 2 * BRANCH_W
    dl = jnp.concatenate([deltas, deltas]).reshape(1, nco)
    half = t_len // tm
    col = lambda a: a.reshape(-1, 1)
    const = lambda shape: pl.BlockSpec(shape, lambda i: (0, 0))
    return pl.pallas_call(
        _filter_cm_kernel,
        grid=(n // tm,),
        in_specs=[
            pl.BlockSpec((128, tm), lambda i: (0, i)),
            pl.BlockSpec((tm, 1), lambda i: (i, 0)),
            const((HY_HID, 128)), const((HY_HID, 1)), const((HY_HID, 1)),
            const((HY_HID, HY_HID)), const((HY_HID, 1)),
            pl.BlockSpec((HY_HID, nco), lambda i: (0, i // half)),
            const((1, nco)),
        ],
        out_specs=[pl.BlockSpec((nco, tm // FFT_N2, FFT_N2), lambda i: (0, i, 0)), const((1, nco))],
        out_shape=[jax.ShapeDtypeStruct((nco, n // FFT_N2, FFT_N2), F32), jax.ShapeDtypeStruct((1, nco), F32)],
        compiler_params=_cparams(("arbitrary",)),
        name="hyena_filter_ffn",
    )(feats, col(t), w1t, col(b1), col(freq), w2.T, col(b2), w3, dl)


def _dft_consts(n1):
    n2 = FFT_N2
    n = n1 * n2
    nh = n1 // 2
    a1 = -2.0 * np.pi * np.outer(np.arange(n1), np.arange(n1)) / n1
    f1r, f1i = np.cos(a1), np.sin(a1)
    a2 = -2.0 * np.pi * np.outer(np.arange(n2), np.arange(n2)) / n2
    f2r, f2i = np.cos(a2), np.sin(a2)
    at = -2.0 * np.pi * np.outer(np.arange(n1), np.arange(n2)) / n
    bf = lambda a: jnp.asarray(a.astype(np.float32)).astype(BF16)
    return dict(
        s1_data=bf(np.block([[f1r[:, :nh], -f1i[:, :nh]], [f1i[:, :nh], f1r[:, :nh]]])),
        s1_real=bf(np.concatenate([f1r, f1i], axis=0)),
        s6=bf(np.block([[f1r[:nh], f1i[:nh]], [-f1i[:nh], f1r[:nh]]]) / n),
        fa=bf(np.concatenate([f2r, f2i], axis=1)), fb=bf(np.concatenate([-f2i, f2r], axis=1)),
        ia=bf(np.concatenate([f2r, -f2i], axis=1)), ib=bf(np.concatenate([f2i, f2r], axis=1)),
        tw_r=bf(np.cos(at)), tw_i=bf(np.sin(at)),
    )


def _lane_dft(ar_ref, ai_ref, fa_ref, fb_ref):
    cb, n1, n2 = ar_ref.shape
    ar = ar_ref[...].reshape(cb * n1, n2)
    ai = ai_ref[...].reshape(cb * n1, n2)
    x = (jnp.dot(ar, fa_ref[...], preferred_element_type=F32)
         + jnp.dot(ai, fb_ref[...], preferred_element_type=F32))
    return x[:, :n2], x[:, n2:]


def _kf_kernel(k_ref, ss_ref, s1_ref, twr_ref, twi_ref, fa_ref, fb_ref, o_ref, ar_ref, ai_ref, *, cb, n1):
    i = pl.program_id(0)
    row = lax.broadcasted_iota(jnp.int32, (n1, FFT_N2), 0)
    lane = lax.broadcasted_iota(jnp.int32, (n1, FFT_N2), 1)
    keep = jnp.logical_not((row == n1 // 2) & (lane == 0))
    twr, twi = twr_ref[...], twi_ref[...]
    for c in range(cb):
        scale = lax.rsqrt(jnp.full((n1, FFT_N2), ss_ref[i * cb + c], F32) + EPS)
        k = jnp.where(keep, k_ref[c] * scale, 0.0)
        a = jnp.dot(s1_ref[...], k.astype(BF16), preferred_element_type=F32)
        a_r, a_i = a[:n1].astype(BF16), a[n1:].astype(BF16)
        ar_ref[c] = a_r * twr - a_i * twi
        ai_ref[c] = a_r * twi + a_i * twr
    xr, xi = _lane_dft(ar_ref, ai_ref, fa_ref, fb_ref)
    o_ref[:, :n1, :] = xr.reshape(cb, n1, FFT_N2).astype(o_ref.dtype)
    o_ref[:, n1:, :] = xi.reshape(cb, n1, FFT_N2).astype(o_ref.dtype)


def _hyena_filter_spectrum(kt, ss, dft):
    nco, n1, _ = kt.shape
    cb = 8
    const = lambda shape: pl.BlockSpec(shape, lambda i: (0,) * len(shape))
    return pl.pallas_call(
        functools.partial(_kf_kernel, cb=cb, n1=n1),
        grid=(nco // cb,),
        in_specs=[
            pl.BlockSpec((cb, n1, FFT_N2), lambda i: (i, 0, 0)),
            pl.BlockSpec(memory_space=pltpu.SMEM),
            const((2 * n1, n1)), const((n1, FFT_N2)), const((n1, FFT_N2)),
            const((FFT_N2, 2 * FFT_N2)), const((FFT_N2, 2 * FFT_N2)),
        ],
        out_specs=pl.BlockSpec((cb, 2 * n1, FFT_N2), lambda i: (i, 0, 0)),
        out_shape=jax.ShapeDtypeStruct((nco, 2 * n1, FFT_N2), BF16),
        scratch_shapes=[pltpu.VMEM((cb, n1, FFT_N2), BF16), pltpu.VMEM((cb, n1, FFT_N2), BF16)],
        compiler_params=_cparams(("arbitrary",)),
        name="hyena_filter_fft",
    )(kt, ss.reshape(nco), dft["s1_real"], dft["tw_r"], dft["tw_i"], dft["fa"], dft["fb"])


def _tok2cm_kernel(x_ref, o_ref):
    rows = x_ref.shape[0] // FFT_N2
    for cb in range(x_ref.shape[1] // 128):
        x3 = x_ref[:, cb * 128:(cb + 1) * 128].reshape(rows, FFT_N2, 128)
        o_ref[cb * 128:(cb + 1) * 128] = jnp.swapaxes(jnp.swapaxes(x3, 1, 2), 0, 1)


def _to_channel_major(p, col0, ncols):
    b, t, _ = p.shape
    tt = 16 * FFT_N2
    nt = t // tt
    cw = 256
    return pl.pallas_call(
        _tok2cm_kernel,
        grid=(ncols // cw, b, nt),
        in_specs=[pl.BlockSpec((None, tt, cw), lambda c, bb, i: (bb, i, col0 // cw + c))],
        out_specs=pl.BlockSpec((cw, tt // FFT_N2, FFT_N2), lambda c, bb, i: (c, bb * nt + i, 0)),
        out_shape=jax.ShapeDtypeStruct((ncols, b * t // FFT_N2, FFT_N2), p.dtype),
        compiler_params=_cparams(("arbitrary", "arbitrary", "arbitrary")),
        name="to_channel_major",
    )(p)


def _cm2tok_kernel(x_ref, o_ref):
    x3 = jnp.swapaxes(jnp.swapaxes(x_ref[...], 0, 1), 1, 2)
    o_ref[...] = x3.reshape(o_ref.shape)


def _to_token_major(y_cm, b):
    c, rows, _ = y_cm.shape
    t = rows * FFT_N2 // b
    tt = 16 * FFT_N2
    nt = t // tt
    return pl.pallas_call(
        _cm2tok_kernel,
        grid=(c // 128, b, nt),
        in_specs=[pl.BlockSpec((128, tt // FFT_N2, FFT_N2), lambda cc, bb, i: (cc, bb * nt + i, 0))],
        out_specs=pl.BlockSpec((None, tt, 128), lambda cc, bb, i: (bb, i, cc)),
        out_shape=jax.ShapeDtypeStruct((b, t, c), y_cm.dtype),
        compiler_params=_cparams(("arbitrary", "arbitrary", "arbitrary")),
        name="to_token_major",
    )(y_cm)


def _hyena_kernel(v_ref, x1_ref, x2_ref, kf1_ref, kf2_ref, cw_ref, cbias_ref, dd_ref,
                  s1_ref, s6_ref, twr_ref, twi_ref, fa_ref, fb_ref, ia_ref, ib_ref,
                  o_ref, vs, x1s, x2s, ar_ref, ai_ref, *, cb, n1, nch):
    i = pl.program_id(0)
    nh = n1 // 2
    rows = 2 * nh
    row = lax.broadcasted_iota(jnp.int32, (rows, FFT_N2), 0)
    lane = lax.broadcasted_iota(jnp.int32, (rows, FFT_N2), 1)
    first = lane == 0
    last = lane == FFT_N2 - 1
    seq_start = first & ((row % nh) == 0)
    seq_end = last & ((row % nh) == nh - 1)
    twr, twi = twr_ref[...], twi_ref[...]

    def short_conv(x, ch):
        prev = pltpu.roll(x, 1, 1)
        prev = jnp.where(first, pltpu.roll(prev, 1, 0), prev)
        prev = jnp.where(seq_start, 0.0, prev)
        nxt = pltpu.roll(x, FFT_N2 - 1, 1)
        nxt = jnp.where(last, pltpu.roll(nxt, rows - 1, 0), nxt)
        nxt = jnp.where(seq_end, 0.0, nxt)
        return cw_ref[ch] * prev + cw_ref[nch + ch] * x + cw_ref[2 * nch + ch] * nxt + cbias_ref[ch]

    def stage1_all(zs):
        acc = [jnp.dot(s1_ref[...], z.astype(BF16), preferred_element_type=F32) for z in zs]
        for c, a in enumerate(acc):
            a_r, a_i = a[:n1].astype(BF16), a[n1:].astype(BF16)
            ar_ref[c] = a_r * twr - a_i * twi
            ai_ref[c] = a_r * twi + a_i * twr

    def spectral(kf_ref):
        xr, xi = _lane_dft(ar_ref, ai_ref, fa_ref, fb_ref)
        kr = kf_ref[:, :n1, :].reshape(cb * n1, FFT_N2)
        ki = kf_ref[:, n1:, :].reshape(cb * n1, FFT_N2)
        xr, xi = xr.astype(BF16), xi.astype(BF16)
        yr = xr * kr - xi * ki
        yi = xr * ki + xi * kr
        bm = (jnp.dot(yr, ia_ref[...], preferred_element_type=F32)
              + jnp.dot(yi, ib_ref[...], preferred_element_type=F32))
        br = bm[:, :FFT_N2].astype(BF16).reshape(cb, n1, FFT_N2)
        bi = bm[:, FFT_N2:].astype(BF16).reshape(cb, n1, FFT_N2)
        ar_ref[...] = br * twr[None] + bi * twi[None]
        ai_ref[...] = bi * twr[None] - br * twi[None]

    def stage6(c):
        bcat = jnp.concatenate([ar_ref[c], ai_ref[c]], axis=0)
        return jnp.dot(s6_ref[...], bcat, preferred_element_type=F32)

    nbr = BRANCH_W
    for c in range(cb):
        ch = i * cb + c
        vs[c] = short_conv(v_ref[c].astype(F32), ch)
        x1s[c] = short_conv(x1_ref[c].astype(F32), nbr + ch)
        x2s[c] = short_conv(x2_ref[c].astype(F32), 2 * nbr + ch)
    stage1_all([vs[c] for c in range(cb)])
    spectral(kf1_ref)
    ys = [stage6(c) for c in range(cb)]
    for c in range(cb):
        vs[c] = x1s[c] * (ys[c] + vs[c] * dd_ref[i * cb + c])
    stage1_all([vs[c] for c in range(cb)])
    spectral(kf2_ref)
    ys = [stage6(c) for c in range(cb)]
    for c in range(cb):
        o_ref[c] = (x2s[c] * (ys[c] + vs[c] * dd_ref[nbr + i * cb + c])).astype(o_ref.dtype)


def _hyena_lat(hy_cm, kf, conv_w, conv_b, hy_d, dft):
    nch, rows, _ = hy_cm.shape
    n1 = rows
    cb = 8
    nblk = BRANCH_W // cb
    const = lambda shape: pl.BlockSpec(shape, lambda i: (0,) * len(shape))
    smem = pl.BlockSpec(memory_space=pltpu.SMEM)
    data = lambda sec: pl.BlockSpec((cb, rows, FFT_N2), lambda i: (sec * nblk + i, 0, 0))
    kfs = lambda o: pl.BlockSpec((cb, 2 * n1, FFT_N2), lambda i: (o * nblk + i, 0, 0))
    return pl.pallas_call(
        functools.partial(_hyena_kernel, cb=cb, n1=n1, nch=nch),
        grid=(nblk,),
        in_specs=[
            data(0), data(1), data(2), kfs(0), kfs(1), smem, smem, smem,
            const((2 * n1, n1)), const((n1, 2 * n1)), const((n1, FFT_N2)), const((n1, FFT_N2)),
            const((FFT_N2, 2 * FFT_N2)), const((FFT_N2, 2 * FFT_N2)),
            const((FFT_N2, 2 * FFT_N2)), const((FFT_N2, 2 * FFT_N2)),
        ],
        out_specs=pl.BlockSpec((cb, rows, FFT_N2), lambda i: (i, 0, 0)),
        out_shape=jax.ShapeDtypeStruct((BRANCH_W, rows, FFT_N2), BF16),
        scratch_shapes=[pltpu.VMEM((cb, rows, FFT_N2), F32) for _ in range(3)]
        + [pltpu.VMEM((cb, n1, FFT_N2), BF16) for _ in range(2)],
        compiler_params=_cparams(("arbitrary",)),
        name="hyena_fftconv",
    )(hy_cm, hy_cm, hy_cm, kf, kf, conv_w.reshape(-1), conv_b, hy_d.reshape(-1),
      dft["s1_data"], dft["s6"], dft["tw_r"], dft["tw_i"], dft["fa"], dft["fb"], dft["ia"], dft["ib"])


def _hyena_ctx_kernel(hy_ref, k_ref, ss_ref, cw_ref, cbias_ref, dd_ref, fwd_a_ref, fwd_b_ref,
                      inv_a_ref, inv_b_ref, o_ref, *, t_len):
    nbr = BRANCH_W
    lane = lax.broadcasted_iota(jnp.int32, (nbr, t_len), 1)

    def short_conv(x, sec):
        sl = slice(sec * nbr, (sec + 1) * nbr)
        prev = jnp.where(lane == 0, 0.0, pltpu.roll(x, 1, 1))
        nxt = jnp.where(lane == t_len - 1, 0.0, pltpu.roll(x, t_len - 1, 1))
        return cw_ref[0, sl, :] * prev + cw_ref[1, sl, :] * x + cw_ref[2, sl, :] * nxt + cbias_ref[sl, :]

    def sec(b, s):
        return short_conv(hy_ref[b, s * nbr:(s + 1) * nbr, :].astype(F32), s)

    klane = lax.broadcasted_iota(jnp.int32, k_ref.shape, 1)
    kk = jnp.where(klane == t_len, 0.0, k_ref[...] * lax.rsqrt(ss_ref[...] + EPS))
    kf = jnp.dot(kk.astype(BF16), fwd_a_ref[...], preferred_element_type=F32)
    n = 2 * t_len

    def conv(z0, z1, order):
        x = (jnp.dot(z0.astype(BF16), fwd_a_ref[:t_len, :], preferred_element_type=F32)
             + jnp.dot(z1.astype(BF16), fwd_b_ref[:t_len, :], preferred_element_type=F32))
        xr, xi = x[:, :n], x[:, n:]
        kr = kf[order * nbr:(order + 1) * nbr, :n]
        ki = kf[order * nbr:(order + 1) * nbr, n:]
        yr = (xr * kr - xi * ki).astype(BF16)
        yi = (xr * ki + xi * kr).astype(BF16)
        y = (jnp.dot(yr, inv_a_ref[...], preferred_element_type=F32)
             + jnp.dot(yi, inv_b_ref[...], preferred_element_type=F32))
        return y[:, :t_len], y[:, t_len:]

    v0, v1 = sec(0, 0), sec(1, 0)
    y0, y1 = conv(v0, v1, 0)
    d1 = dd_ref[:nbr, :]
    d2 = dd_ref[nbr:, :]
    z0 = sec(0, 1) * (y0 + v0 * d1)
    z1 = sec(1, 1) * (y1 + v1 * d1)
    y0, y1 = conv(z0, z1, 1)
    o_ref[0] = (sec(0, 2) * (y0 + z0 * d2)).astype(o_ref.dtype)
    o_ref[1] = (sec(1, 2) * (y1 + z1 * d2)).astype(o_ref.dtype)


def _hyena_ctx(hy_ctx, k_time, ss, conv_w, conv_b, hy_d):
    b, t_len, nch = hy_ctx.shape
    n = 2 * t_len
    ang = -2.0 * np.pi * np.outer(np.arange(n), np.arange(n)) / n
    fr, fi = np.cos(ang), np.sin(ang)
    bf = lambda a: jnp.asarray(a.astype(np.float32)).astype(BF16)
    fwd_a = bf(np.concatenate([fr, fi], axis=1))
    fwd_b = bf(np.concatenate([-fi, fr], axis=1))
    inv_a = bf(np.concatenate([fr[:, :t_len], -fi[:, :t_len]], axis=1) / n)
    inv_b = bf(np.concatenate([fi[:, :t_len], fr[:, :t_len]], axis=1) / n)
    out = pl.pallas_call(
        functools.partial(_hyena_ctx_kernel, t_len=t_len),
        out_shape=jax.ShapeDtypeStruct((b, BRANCH_W, t_len), BF16),
        compiler_params=pltpu.CompilerParams(vmem_limit_bytes=VMEM_LIMIT),
        name="hyena_ctx",
    )(hy_ctx.transpose(0, 2, 1), k_time.T, ss.reshape(-1, 1), conv_w.reshape(3, nch, 1),
      conv_b.reshape(nch, 1), hy_d.reshape(-1, 1), fwd_a, fwd_b, inv_a, inv_b)
    return out.transpose(0, 2, 1)


def _merge_kernel(x_ref, ya_ref, yb_ref, yc_ref, za_ref, zb_ref, zc_ref, g0_ref, g1_ref, g2_ref, mod_ref,
                  gw_ref, gb_ref, wb_ref, wo_ref, fg_ref, o_ref, *, mod_row, d, final):
    ya = jax.nn.gelu(ya_ref[...].astype(F32)).astype(BF16)
    glu = jnp.dot(ya, gw_ref[...], preferred_element_type=F32) + gb_ref[...]
    y_a = glu[:, :BRANCH_W] * (1.0 + jnp.tanh(glu[:, BRANCH_W:]))

    def branch(i, y, z_ref, g_ref):
        zh = z_ref[...]
        u = (zh + zh * jnp.tanh(zh)) * y
        return (1.0 + jnp.tanh(g_ref[...].astype(F32))) * jnp.dot(u, wb_ref[i], preferred_element_type=F32)

    acc = branch(0, y_a.astype(BF16), za_ref, g0_ref)
    acc = acc + branch(1, yb_ref[...], zb_ref, g1_ref)
    acc = acc + branch(2, yc_ref[...], zc_ref, g2_ref)
    out = jnp.dot(acc.astype(BF16), wo_ref[...], preferred_element_type=F32)
    if mod_row is None:
        m = mod_ref[pl.ds(pl.program_id(0), 1), :]
    else:
        m = mod_ref[mod_row:mod_row + 1, :]
    xn = x_ref[...] + m[:, 2 * d:] * out
    if final:
        ms = jnp.mean(xn * xn, axis=-1, keepdims=True)
        xn = xn * lax.rsqrt(ms + EPS) * fg_ref[...]
    o_ref[...] = xn


def _merge(x, ya, yb, yc, p, mod, glu_w, glu_b, w_branch, w_out, final_g, *, mod_row, tm, final):
    b, t, d = x.shape
    tok = lambda w, col: pl.BlockSpec((None, tm, w), lambda bb, i: (bb, i, col // w))
    const = lambda shape: pl.BlockSpec(shape, lambda bb, i: (0,) * len(shape))
    return pl.pallas_call(
        functools.partial(_merge_kernel, mod_row=mod_row, d=d, final=final),
        grid=(b, t // tm),
        in_specs=[
            tok(d, 0), tok(512, 0), tok(512, 0), tok(512, 0),
            tok(512, COL_ZA), tok(512, COL_ZB), tok(512, COL_ZC),
            tok(1024, COL_GT), tok(1024, COL_GT + 1024), tok(1024, COL_GT + 2048),
            const((8, 3 * d)), const((BRANCH_W, 2 * BRANCH_W)), const((1, 2 * BRANCH_W)),
            const((3, BRANCH_W, d)), const((d, d)), const((1, d)),
        ],
        out_specs=tok(d, 0),
        out_shape=jax.ShapeDtypeStruct((b, t, d), F32),
        compiler_params=_cparams(("arbitrary", "arbitrary")),
        name="merge_out",
    )(x, ya, yb, yc, p, p, p, p, p, p, mod, glu_w, glu_b.reshape(1, -1), w_branch, w_out, final_g.reshape(1, d))


def kernel(x, c, ctx, c_ctx, ada_w, ada_b, norm_g, w_in, s5_lam_re, s5_lam_im, s5_log_dt, s5_b_re, s5_b_im,
           s5_c_re, s5_c_im, s5_d, s5_glu_w, s5_glu_b, na_rpb, hy_conv_w, hy_conv_b, hf_w1, hf_b1, hf_freq,
           hf_w2, hf_b2, hf_w3, hy_d, w_branch, w_out, final_g):
    bsz, t_lat, d = x.shape
    t_ctx = ctx.shape[1]
    depth = ada_w.shape[0]
    assert bsz == 2 and d == 1024 and t_lat % (16 * FFT_N2) == 0 and t_ctx % 128 == 0

    cvec = jnp.zeros((8, d), F32).at[:bsz].set(c).at[bsz].set(c_ctx)
    mods = _modulation(cvec, ada_w, ada_b)
    cos_t, sin_t = _rope_tables(t_lat)
    n1 = 2 * t_lat // FFT_N2
    dft = _dft_consts(n1)
    tm_lat = 1024 if t_lat % 1024 == 0 else 512

    x_lat, x_ctx = x, ctx
    for l in range(depth):
        ctx_out = l < depth - 1
        half_cols = jnp.asarray(np.where(np.isin(np.arange(IN_COLS) // BRANCH_W, (1, 5, 9)) | (np.arange(IN_COLS) >= COL_GT),
                                         0.5, 1.0).astype(np.float32))
        w_bf = (w_in[l] * half_cols).astype(BF16)
        p_lat = _inproj(x_lat, mods[l], norm_g[l], w_bf, mod_row=None, tm=tm_lat)
        p_ctx = _inproj(x_ctx, mods[l], norm_g[l], w_bf, mod_row=bsz, tm=t_ctx)

        ops = _s5_operators(s5_lam_re[l], s5_lam_im[l], s5_log_dt[l], s5_b_re[l], s5_b_im[l],
                            s5_c_re[l], s5_c_im[l], s5_d[l])
        ya, ya_c = _s5_mixer(p_lat, p_ctx, ops)

        qr, kt = _rope_qk(p_lat, cos_t, sin_t)
        bias = _na_bias_tables(na_rpb[l])
        yb = _na_mixer(qr, kt, p_lat, p_ctx, bias)

        k_cm, ss = _hyena_filter_cm(t_lat, hf_w1[l], hf_b1[l], hf_freq[l], hf_w2[l], hf_b2[l], hf_w3[l])
        kf = _hyena_filter_spectrum(k_cm, ss, dft)
        hy_cm = _to_channel_major(p_lat, COL_HY, 3 * BRANCH_W)
        yc_cm = _hyena_lat(hy_cm, kf, hy_conv_w[l], hy_conv_b[l], hy_d[l], dft)
        yc = _to_token_major(yc_cm, bsz)

        wb_bf = (0.5 * w_branch[l]).astype(BF16)
        wo_bf = w_out[l].astype(BF16)
        gw_bf = (0.5 * s5_glu_w[l]).astype(BF16)
        glu_b_half = 0.5 * s5_glu_b[l]
        x_lat_new = _merge(x_lat, ya, yb, yc, p_lat, mods[l], gw_bf, glu_b_half, wb_bf, wo_bf,
                           final_g, mod_row=None, tm=512, final=not ctx_out)
        if ctx_out:
            yb_c = _ctx_attention(p_ctx)
            kc_time, ss_c = _hyena_filter_time(t_ctx, hf_w1[l], hf_b1[l], hf_freq[l], hf_w2[l], hf_b2[l], hf_w3[l])
            yc_c = _hyena_ctx(p_ctx[:, :, COL_HY:COL_HY + 1536], kc_time, ss_c, hy_conv_w[l], hy_conv_b[l], hy_d[l])
            x_ctx = _merge(x_ctx, ya_c, yb_c, yc_c, p_ctx, mods[l], gw_bf, glu_b_half, wb_bf, wo_bf,
                           final_g, mod_row=bsz, tm=t_ctx, final=False)
        x_lat = x_lat_new
    return x_lat
```

```python
import functools
import math

import numpy as np
import jax
import jax.numpy as jnp
from jax import lax
from jax.experimental import pallas as pl
from jax.experimental.pallas import tpu as pltpu

F32 = jnp.float32
BF16 = jnp.bfloat16
HIGHEST = lax.Precision.HIGHEST
HIGH = lax.Precision.HIGH

GRID_W = 64
BRANCH_W = 512
S5_P = 16
S5_N = 64
S5_G = BRANCH_W // S5_P
S5_TC = 16
HEAD_DIM = 64
NA_HEADS = BRANCH_W // HEAD_DIM
NA_KR = 8
NA_KW = 16
ROPE_HALF = 16
ROPE_THETA = 10000.0
HY_EMB = 33
HY_BANDS = (HY_EMB - 1) // 2
HY_HID = 64
HY_MIN_DECAY = math.log(1e-2) / 1.5
HY_MAX_DECAY = math.log(1e-2) / 0.3
EPS = 1e-6
FFT_N2 = 256
NEG_BIG = -1e30
LOG2E = 1.4426950408889634
VMEM_LIMIT = 52 * 1024 * 1024

COL_UA, COL_ZA, COL_Q, COL_K, COL_V, COL_ZB, COL_HY, COL_ZC, COL_GT = (
    0, 512, 1024, 1536, 2048, 2560, 3072, 4608, 5120)
IN_COLS = 8192


def _cparams(sem):
    return pltpu.CompilerParams(dimension_semantics=sem, vmem_limit_bytes=VMEM_LIMIT)


def _sigmoid(x):
    return 0.5 * jnp.tanh(0.5 * x) + 0.5


def _silu(x):
    return x * _sigmoid(x)


def _mod_kernel(c_ref, w_ref, b_ref, o_ref):
    s = _silu(c_ref[...])
    o_ref[...] = jnp.dot(s, w_ref[...], preferred_element_type=F32, precision=HIGHEST) + b_ref[...]


def _modulation(cvec, ada_w, ada_b):
    depth, d, d3 = ada_w.shape
    tn = 1024
    return pl.pallas_call(
        _mod_kernel,
        grid=(depth, d3 // tn),
        in_specs=[
            pl.BlockSpec((8, d), lambda l, j: (0, 0)),
            pl.BlockSpec((None, d, tn), lambda l, j: (l, 0, j)),
            pl.BlockSpec((None, 1, tn), lambda l, j: (l, 0, j)),
        ],
        out_specs=pl.BlockSpec((None, 8, tn), lambda l, j: (l, 0, j)),
        out_shape=jax.ShapeDtypeStruct((depth, 8, d3), F32),
        compiler_params=_cparams(("arbitrary", "arbitrary")),
        name="adaln_mod",
    )(cvec, ada_w, ada_b.reshape(depth, 1, d3))


def _inproj_kernel(x_ref, mod_ref, g_ref, w_ref, o_ref, h_ref, *, mod_row, d):
    @pl.when(pl.program_id(2) == 0)
    def _():
        x = x_ref[...]
        ms = jnp.mean(x * x, axis=-1, keepdims=True)
        y = x * lax.rsqrt(ms + EPS) * g_ref[...]
        if mod_row is None:
            m = mod_ref[pl.ds(pl.program_id(0), 1), :]
        else:
            m = mod_ref[mod_row:mod_row + 1, :]
        h_ref[...] = (y * (1.0 + m[:, d:2 * d]) + m[:, :d]).astype(BF16)

    o_ref[...] = jnp.dot(h_ref[...], w_ref[...], preferred_element_type=F32).astype(o_ref.dtype)


def _inproj(x, mod, g, w_bf, *, mod_row, tm):
    b, t, d = x.shape
    n = w_bf.shape[1]
    tn = 2048
    return pl.pallas_call(
        functools.partial(_inproj_kernel, mod_row=mod_row, d=d),
        grid=(b, t // tm, n // tn),
        in_specs=[
            pl.BlockSpec((None, tm, d), lambda bb, i, j: (bb, i, 0)),
            pl.BlockSpec((8, 3 * d), lambda bb, i, j: (0, 0)),
            pl.BlockSpec((1, d), lambda bb, i, j: (0, 0)),
            pl.BlockSpec((d, tn), lambda bb, i, j: (0, j)),
        ],
        out_specs=pl.BlockSpec((None, tm, tn), lambda bb, i, j: (bb, i, j)),
        out_shape=jax.ShapeDtypeStruct((b, t, n), BF16),
        scratch_shapes=[pltpu.VMEM((tm, d), BF16)],
        compiler_params=_cparams(("arbitrary", "arbitrary", "arbitrary")),
        name="norm_inproj",
    )(x, mod, g.reshape(1, d), w_bf)


def _s5_operators(lam_re, lam_im, log_dt, b_re, b_im, c_re, c_im, d_skip):
    tc, p, n, g = S5_TC, S5_P, S5_N, S5_G
    dt = jnp.exp(log_dt)[..., None]
    er = lam_re * dt
    ei = lam_im * dt
    k = jnp.arange(tc + 1, dtype=F32)[:, None, None, None]
    mag = jnp.exp(k * er[None])
    pw_r = mag * jnp.cos(k * ei[None])
    pw_i = mag * jnp.sin(k * ei[None])
    lb_r, lb_i = pw_r[1], pw_i[1]
    den = lam_re * lam_re + lam_im * lam_im
    q_r = ((lb_r - 1.0) * lam_re + lb_i * lam_im) / den
    q_i = (lb_i * lam_re - (lb_r - 1.0) * lam_im) / den
    bb_r = q_r[..., None] * b_re - q_i[..., None] * b_im
    bb_i = q_r[..., None] * b_im + q_i[..., None] * b_re

    ar, ai = pw_r[tc], pw_i[tc]
    a1 = jnp.concatenate([ar, ar], -1)
    a2 = jnp.concatenate([-ai, ai], -1)
    a3 = jnp.concatenate([ai, -ai], -1)

    grp = lambda a: a.transpose(1, 0, 2, 3)
    pad_t = ((0, 0), (0, 0), (0, 24 - (tc + 1)), (0, 0))
    blk = lambda rows, cols: pl.BlockSpec((None, 2, rows, cols), lambda i: (i, 0, 0, 0))
    wcat, ccat = pl.pallas_call(
        _s5_ops_kernel,
        grid=(g,),
        in_specs=[blk(24, n), blk(24, n), blk(p, n), blk(p, n), blk(p, n), blk(p, n),
                  pl.BlockSpec((None, p, 1), lambda i: (i, 0, 0))],
        out_specs=[pl.BlockSpec((None, tc * p, 768), lambda i: (i, 0, 0)),
                   pl.BlockSpec((None, 4 * n, tc * p), lambda i: (i, 0, 0))],
        out_shape=[jax.ShapeDtypeStruct((g, tc * p, 768), BF16), jax.ShapeDtypeStruct((g, 4 * n, tc * p), BF16)],
        compiler_params=_cparams(("arbitrary",)),
        name="s5_operators",
    )(jnp.pad(pw_r.transpose(2, 1, 0, 3), pad_t), jnp.pad(pw_i.transpose(2, 1, 0, 3), pad_t),
      grp(c_re), grp(c_im), grp(bb_r.transpose(0, 1, 3, 2)), grp(bb_i.transpose(0, 1, 3, 2)),
      d_skip.reshape(g, p, 1))
    return wcat, ccat, a1, a2, a3


def _s5_ops_kernel(pwr_ref, pwi_ref, cr_ref, ci_ref, br_ref, bi_ref, d_ref, w_ref, c_ref):
    tc, p, n = S5_TC, S5_P, S5_N
    nt = (((1,), (1,)), ((), ()))
    lane = lax.broadcasted_iota(jnp.int32, (p, tc * p), 1)
    row = lax.broadcasted_iota(jnp.int32, (p, tc * p), 0)
    eye_n = (lax.broadcasted_iota(jnp.int32, (n, n), 0) == lax.broadcasted_iota(jnp.int32, (n, n), 1)).astype(F32)

    def c_lam(d, taus):
        pr = jnp.concatenate([jnp.broadcast_to(pwr_ref[d, t:t + 1, :], (p, n)) for t in taus], axis=0)
        pi = jnp.concatenate([jnp.broadcast_to(pwi_ref[d, t:t + 1, :], (p, n)) for t in taus], axis=0)
        cr = jnp.concatenate([cr_ref[d]] * len(taus), axis=0)
        ci = jnp.concatenate([ci_ref[d]] * len(taus), axis=0)
        return cr * pr - ci * pi, cr * pi + ci * pr

    def lag_row(d, taus):
        clr, cli = c_lam(d, taus)
        return (lax.dot_general(br_ref[d], clr, nt, preferred_element_type=F32, precision=HIGHEST)
                - lax.dot_general(bi_ref[d], cli, nt, preferred_element_type=F32, precision=HIGHEST))

    rf = lag_row(0, list(range(tc))) + jnp.where(lane == row, d_ref[...], 0.0)
    rb = lag_row(1, list(range(tc - 1, -1, -1)))
    for ti in range(tc):
        sf = p * ti
        sb = p * (tc - 1 - ti)
        a = jnp.where(lane >= sf, pltpu.roll(rf, sf, 1) if sf else rf, 0.0)
        b = jnp.where(lane < tc * p - sb, pltpu.roll(rb, tc * p - sb, 1) if sb else rb, 0.0)
        w_ref[ti * p:(ti + 1) * p, 0:tc * p] = (a + b).astype(BF16)

    for d, col in ((0, tc * p), (1, tc * p + 4 * n)):
        for t in range(tc):
            e = tc - 1 - t if d == 0 else t
            pr = pwr_ref[d, e:e + 1, :]
            pi = pwi_ref[d, e:e + 1, :]
            sr = br_ref[d] * pr - bi_ref[d] * pi
            si = br_ref[d] * pi + bi_ref[d] * pr
            w_ref[t * p:(t + 1) * p, col:col + 4 * n] = jnp.concatenate([sr, si, si, sr], axis=1).astype(BF16)

    for d, taus in ((0, list(range(1, tc + 1))), (1, list(range(tc, 0, -1)))):
        clr, cli = c_lam(d, taus)
        c_ref[2 * n * d:2 * n * d + n, :] = lax.dot_general(
            eye_n, clr, nt, preferred_element_type=F32, precision=HIGHEST).astype(BF16)
        c_ref[2 * n * d + n:2 * n * (d + 1), :] = lax.dot_general(
            eye_n, -cli, nt, preferred_element_type=F32, precision=HIGHEST).astype(BF16)


def _s5_kernel(ul_ref, uc_ref, w_ref, c_ref, a1_ref, a2_ref, a3_ref, yl_ref, yc_ref,
               tok, xg, spf, sqf, spb, sqb, *, gb, nch, nch_lat, pitch):
    t_lat = nch_lat * S5_TC
    tok[:t_lat, :] = ul_ref[...].astype(F32)
    tok[t_lat:, :] = uc_ref[...].astype(F32)

    rt = next(r for r in (80, 40, 16, 8) if nch % r == 0)
    lane_rt = lax.broadcasted_iota(jnp.int32, (rt, 128), 1) // S5_P

    def block_transpose(a):
        for d in (4, 2, 1):
            keep = (lane_rt & d) == 0
            nxt = list(a)
            for i in range(8):
                if i & d == 0:
                    j = i + d
                    nxt[i] = jnp.where(keep, a[i], pltpu.roll(a[j], d * S5_P, 1))
                    nxt[j] = jnp.where(keep, pltpu.roll(a[i], 128 - d * S5_P, 1), a[j])
            a = nxt
        return a

    def to_compact(it, carry):
        r0 = pl.multiple_of(it * rt, rt)
        for h in range(2):
            z = [tok[pl.ds(r0 * S5_TC + 8 * h + t, rt, stride=S5_TC), :] for t in range(8)]
            b = block_transpose(z)
            for g in range(gb):
                xg[g, pl.ds(r0, rt), h * 128:(h + 1) * 128] = b[g]
        return carry

    lax.fori_loop(0, nch // rt, to_compact, 0)

    for g in range(gb):
        r = jnp.dot(xg[g].astype(BF16), w_ref[g], preferred_element_type=F32)
        xg[g] = r[:, :256]
        spf[g * pitch:g * pitch + nch, :] = r[:, 256:384]
        sqf[g * pitch:g * pitch + nch, :] = r[:, 384:512]
        spb[g * pitch:g * pitch + nch, :] = r[:, 512:640]
        sqb[g * pitch:g * pitch + nch, :] = r[:, 640:768]

    a1f, a2f, a3f = a1_ref[0], a2_ref[0], a3_ref[0]
    a1b, a2b, a3b = a1_ref[1], a2_ref[1], a3_ref[1]
    nch_ctx = nch - nch_lat

    def body(s, carry):
        pf, qf, pb, qb = carry
        cf = jnp.where(s < nch_ctx, s + nch_lat, s - nch_ctx)
        cb = nch - 1 - s
        idx_f = pl.ds(cf, gb, stride=pitch)
        idx_b = pl.ds(cb, gb, stride=pitch)
        sp = spf[idx_f, :]
        sq = sqf[idx_f, :]
        spf[idx_f, :] = pf
        pf, qf = pf * a1f + qf * a2f + sp, qf * a1f + pf * a3f + sq
        sp = spb[idx_b, :]
        sq = sqb[idx_b, :]
        spb[idx_b, :] = pb
        pb, qb = pb * a1b + qb * a2b + sp, qb * a1b + pb * a3b + sq
        return pf, qf, pb, qb

    z0 = jnp.zeros((gb, 128), F32)
    lax.fori_loop(0, nch, body, (z0, z0, z0, z0), unroll=4)

    for g in range(gb):
        hf = spf[g * pitch:g * pitch + nch, :].astype(BF16)
        hb = spb[g * pitch:g * pitch + nch, :].astype(BF16)
        cm = c_ref[g]
        y = (jnp.dot(hf, cm[:128, :], preferred_element_type=F32)
             + jnp.dot(hb, cm[128:, :], preferred_element_type=F32))
        xg[g] = xg[g] + y

    def to_tokens(it, carry):
        r0 = pl.multiple_of(it * rt, rt)
        for h in range(2):
            y = [xg[g, pl.ds(r0, rt), h * 128:(h + 1) * 128] for g in range(gb)]
            b = block_transpose(y)
            for t in range(8):
                tok[pl.ds(r0 * S5_TC + 8 * h + t, rt, stride=S5_TC), :] = b[t]
        return carry

    lax.fori_loop(0, nch // rt, to_tokens, 0)
    yl_ref[...] = tok[:t_lat, :].astype(yl_ref.dtype)
    yc_ref[...] = tok[t_lat:, :].astype(yc_ref.dtype)


def _s5_mixer(p_lat, p_ctx, ops):
    wcat, ccat, a1, a2, a3 = ops
    b, t_lat, _ = p_lat.shape
    t_ctx = p_ctx.shape[1]
    nch, nch_lat = (t_lat + t_ctx) // S5_TC, t_lat // S5_TC
    gb = 128 // S5_P
    pitch = ((nch + 7) // 8) * 8 + 8
    kern = functools.partial(_s5_kernel, gb=gb, nch=nch, nch_lat=nch_lat, pitch=pitch)
    once = pl.Buffered(1)
    return pl.pallas_call(
        kern,
        grid=(b, S5_G // gb),
        in_specs=[
            pl.BlockSpec((None, t_lat, 128), lambda bb, i: (bb, 0, COL_UA // 128 + i), pipeline_mode=once),
            pl.BlockSpec((None, t_ctx, 128), lambda bb, i: (bb, 0, COL_UA // 128 + i)),
            pl.BlockSpec((gb, 256, 768), lambda bb, i: (i, 0, 0)),
            pl.BlockSpec((gb, 256, 256), lambda bb, i: (i, 0, 0)),
            pl.BlockSpec((2, gb, 128), lambda bb, i: (0, i, 0)),
            pl.BlockSpec((2, gb, 128), lambda bb, i: (0, i, 0)),
            pl.BlockSpec((2, gb, 128), lambda bb, i: (0, i, 0)),
        ],
        out_specs=[
            pl.BlockSpec((None, t_lat, 128), lambda bb, i: (bb, 0, i), pipeline_mode=once),
            pl.BlockSpec((None, t_ctx, 128), lambda bb, i: (bb, 0, i)),
        ],
        out_shape=[jax.ShapeDtypeStruct((b, t_lat, BRANCH_W), BF16),
                   jax.ShapeDtypeStruct((b, t_ctx, BRANCH_W), BF16)],
        scratch_shapes=[pltpu.VMEM((t_lat + t_ctx, 128), F32), pltpu.VMEM((gb, nch, 256), F32)]
        + [pltpu.VMEM((gb * pitch, 128), F32) for _ in range(4)],
        compiler_params=_cparams(("arbitrary", "arbitrary")),
        name="s5_chunk_scan",
    )(p_lat, p_ctx, wcat, ccat, a1, a2, a3)


def _rope_tables(t_len):
    t = jnp.arange(t_len)
    rows = (t // GRID_W).astype(F32)
    cols = (t % GRID_W).astype(F32)
    inv = ROPE_THETA ** (-jnp.arange(ROPE_HALF, dtype=F32) / ROPE_HALF)
    lane = np.arange(128)
    dd = lane % HEAD_DIM
    fi = dd % ROPE_HALF
    use_row = jnp.asarray(dd < 32)
    sign = jnp.asarray(np.where((dd % 32) < ROPE_HALF, -1.0, 1.0).astype(np.float32))
    ang = jnp.where(use_row[None, :], rows[:, None], cols[:, None]) * inv[fi][None, :]
    return jnp.cos(ang), jnp.sin(ang) * sign[None, :]


def _rope_kernel(q_ref, k_ref, cos_ref, sin_ref, qo_ref, kt_ref, *, scale):
    cos = jnp.concatenate([cos_ref[...]] * 4, axis=1)
    sin = jnp.concatenate([sin_ref[...]] * 4, axis=1)
    lane = lax.broadcasted_iota(jnp.int32, cos.shape, 1)
    low = (lane % 32) < ROPE_HALF

    def rot(x):
        n = x.shape[1]
        partner = jnp.where(low, pltpu.roll(x, n - ROPE_HALF, 1), pltpu.roll(x, ROPE_HALF, 1))
        return x * cos + partner * sin

    qo_ref[...] = (rot(q_ref[...].astype(F32)) * scale).astype(BF16)
    kr = rot(k_ref[...].astype(F32))
    for c in range(kt_ref.shape[0]):
        kt_ref[c] = kr[c * 128:(c + 1) * 128, :].T.astype(BF16)


def _rope_qk(p_lat, cos_t, sin_t):
    b, t, _ = p_lat.shape
    tm = 1024
    return pl.pallas_call(
        functools.partial(_rope_kernel, scale=HEAD_DIM ** -0.5 * LOG2E),
        grid=(b, t // tm),
        in_specs=[
            pl.BlockSpec((None, tm, 512), lambda bb, i: (bb, i, COL_Q // 512)),
            pl.BlockSpec((None, tm, 512), lambda bb, i: (bb, i, COL_K // 512)),
            pl.BlockSpec((tm, 128), lambda bb, i: (i, 0)),
            pl.BlockSpec((tm, 128), lambda bb, i: (i, 0)),
        ],
        out_specs=[pl.BlockSpec((None, tm, 512), lambda bb, i: (bb, i, 0)),
                   pl.BlockSpec((None, tm // 128, 512, 128), lambda bb, i: (bb, i, 0, 0))],
        out_shape=[jax.ShapeDtypeStruct((b, t, BRANCH_W), BF16),
                   jax.ShapeDtypeStruct((b, t // 128, BRANCH_W, 128), BF16)],
        compiler_params=_cparams(("arbitrary", "arbitrary")),
        name="rope_qk",
    )(p_lat, p_lat, cos_t, sin_t)


_NA_VARIANTS = (((4, 5), (0, 1)), ((0, 1), (0, 0)), ((2, 3), (0, 0)), ((4, 5), (0, 0)), ((6, 7), (0, 0)))
NA_WIN = NA_KR + 2


def _na_bias_tables(rpb):
    h = rpb.shape[0]
    qc = np.arange(GRID_W)
    kc = np.arange(GRID_W)
    cs = np.clip(qc - NA_KW // 2, 0, GRID_W - NA_KW)
    valid_c = (kc[None, :] >= cs[:, None]) & (kc[None, :] < cs[:, None] + NA_KW)
    rel_c = np.clip(kc[None, :] - qc[:, None] + NA_KW - 1, 0, 2 * NA_KW - 2)
    oh_c = np.eye(2 * NA_KW - 1, dtype=np.float32)[rel_c]
    j = np.arange(NA_WIN)
    oh_r = np.zeros((len(_NA_VARIANTS), 2, NA_WIN, 2 * NA_KR - 1), np.float32)
    valid_r = np.zeros((len(_NA_VARIANTS), 2, NA_WIN), bool)
    for n, (offs, los) in enumerate(_NA_VARIANTS):
        for rr in range(2):
            valid_r[n, rr] = (j >= los[rr]) & (j < los[rr] + NA_KR)
            rel_r = np.clip(j - offs[rr] + NA_KR - 1, 0, 2 * NA_KR - 2)
            oh_r[n, rr] = np.eye(2 * NA_KR - 1, dtype=np.float32)[rel_r]
    bias = jnp.einsum("nrja,hab,ckb->nhrcjk", jnp.asarray(oh_r), rpb, jnp.asarray(oh_c), precision=HIGH) * LOG2E
    valid = valid_r[:, None, :, None, :, None] & valid_c[None, None, None, :, None, :]
    bias = jnp.where(jnp.asarray(valid), bias, NEG_BIG)
    return bias.reshape(len(_NA_VARIANTS), h // 2, 4 * GRID_W, NA_WIN * GRID_W)


def _na_kernel(q_ref, kp_ref, kc_ref, kn_ref, vp_ref, vc_ref, vn_ref, kx_ref, vx_ref, bias_ref, o_ref,
               kt3, vcat, *, ni):
    half = 4 * GRID_W
    tq = 8 * GRID_W
    kt3[0:2] = kp_ref[2:4]
    kt3[2:6] = kc_ref[...]
    kt3[6:8] = kn_ref[0:2]
    vcat[:half, :] = vp_ref[half:, :]
    vcat[half:half + tq, :] = vc_ref[...]
    vcat[half + tq:, :] = vn_ref[:half, :]
    kxt = kx_ref[...].astype(F32).T.astype(BF16)
    i = pl.program_id(1)
    head0 = lax.broadcasted_iota(jnp.int32, (2 * GRID_W, 128), 1) < HEAD_DIM
    nchunk = NA_WIN // 2
    ones_loc = jnp.ones((NA_WIN * GRID_W, 128), BF16)
    ones_ctx = jnp.ones((kx_ref.shape[0], 128), BF16)

    def pair_body(a, carry):
        first = i == 0
        last = i == ni - 1
        s0 = jnp.where(first, jnp.maximum(2 * a, 4), jnp.where(last, jnp.minimum(2 * a, 4), 2 * a))
        var = jnp.where(first & (a < 2), 1 + a, jnp.where(last & (a >= 2), 1 + a, 0))
        q0 = pl.multiple_of(a * 2 * GRID_W, 2 * GRID_W)
        k0 = pl.multiple_of(s0 * GRID_W, 2 * GRID_W)
        sp = s0 // 2
        nhp = NA_HEADS // 2
        lss = [slice(hp * 128, (hp + 1) * 128) for hp in range(nhp)]
        scores = []
        for hp in range(nhp):
            qp = q_ref[pl.ds(q0, 2 * GRID_W), lss[hp]]
            zero = jnp.zeros_like(qp)
            q2 = jnp.concatenate([jnp.where(head0, qp, zero), jnp.where(head0, zero, qp)], axis=0)
            kw = kt3[pl.ds(sp, nchunk), lss[hp], :]
            s_loc = jnp.concatenate(
                [jnp.dot(q2, kw[c], preferred_element_type=F32) for c in range(nchunk)], axis=1) + bias_ref[var, hp]
            s_ctx = jnp.dot(q2, kxt[lss[hp], :], preferred_element_type=F32)
            scores.append((s_loc, s_ctx))
        probs = []
        for s_loc, s_ctx in scores:
            m = jnp.maximum(jnp.max(s_loc, axis=-1, keepdims=True), jnp.max(s_ctx, axis=-1, keepdims=True))
            probs.append((jnp.exp2((s_loc - m).astype(BF16)), jnp.exp2((s_ctx - m).astype(BF16))))
        outs = []
        for hp in range(nhp):
            p_loc, p_ctx = probs[hp]
            vw = jnp.concatenate([vcat[pl.ds(k0, NA_WIN * GRID_W), lss[hp]], ones_loc], axis=1)
            vx = jnp.concatenate([vx_ref[:, lss[hp]], ones_ctx], axis=1)
            o = jnp.dot(p_loc, vw, preferred_element_type=F32) + jnp.dot(p_ctx, vx, preferred_element_type=F32)
            o = o[:, :128] / o[:, 128:]
            outs.append(jnp.where(head0, o[:2 * GRID_W], o[2 * GRID_W:]))
        o_ref[pl.ds(q0, 2 * GRID_W), :] = jnp.concatenate(outs, axis=1).astype(o_ref.dtype)
        return carry

    lax.fori_loop(0, 4, pair_body, 0, unroll=2)


def _na_mixer(qr, kt, p_lat, p_ctx, bias):
    b, t, _ = qr.shape
    nc = p_ctx.shape[1]
    tq = 8 * GRID_W
    ni = t // tq
    assert ni >= 2
    blk = lambda off, col: pl.BlockSpec(
        (None, tq, BRANCH_W), lambda bb, i: (bb, jnp.clip(i + off, 0, ni - 1), col))
    ktb = lambda off: pl.BlockSpec(
        (None, tq // 128, BRANCH_W, 128), lambda bb, i: (bb, jnp.clip(i + off, 0, ni - 1), 0, 0))
    cv = COL_V // BRANCH_W
    return pl.pallas_call(
        functools.partial(_na_kernel, ni=ni),
        grid=(b, ni),
        in_specs=[
            blk(0, 0), ktb(-1), ktb(0), ktb(1),
            blk(-1, cv), blk(0, cv), blk(1, cv),
            pl.BlockSpec((None, nc, BRANCH_W), lambda bb, i: (bb, 0, COL_K // BRANCH_W)),
            pl.BlockSpec((None, nc, BRANCH_W), lambda bb, i: (bb, 0, cv)),
            pl.BlockSpec(bias.shape, lambda bb, i: (0, 0, 0, 0), pipeline_mode=pl.Buffered(1)),
        ],
        out_specs=pl.BlockSpec((None, tq, BRANCH_W), lambda bb, i: (bb, i, 0)),
        out_shape=jax.ShapeDtypeStruct((b, t, BRANCH_W), BF16),
        scratch_shapes=[pltpu.VMEM((2 * tq // 128, BRANCH_W, 128), BF16), pltpu.VMEM((2 * tq, BRANCH_W), BF16)],
        compiler_params=_cparams(("arbitrary", "arbitrary")),
        name="na_attention",
    )(qr, kt, kt, kt, p_lat, p_lat, p_lat, p_ctx, p_ctx, bias)


def _ctx_attn_kernel(q_ref, k_ref, v_ref, o_ref, *, scale):
    dn = (((1,), (1,)), ((), ()))
    outs = []
    for hh in range(2):
        sl = slice(hh * HEAD_DIM, (hh + 1) * HEAD_DIM)
        qh = (q_ref[:, sl].astype(F32) * scale).astype(BF16)
        s = lax.dot_general(qh, k_ref[:, sl], dn, preferred_element_type=F32)
        m = jnp.max(s, axis=-1, keepdims=True)
        p = jnp.exp(s - m)
        den = jnp.sum(p, axis=-1, keepdims=True)
        outs.append(jnp.dot(p.astype(BF16), v_ref[:, sl], preferred_element_type=F32) / den)
    o_ref[...] = jnp.concatenate(outs, axis=1).astype(o_ref.dtype)


def _ctx_attention(p_ctx):
    b, nc, _ = p_ctx.shape
    spec = lambda col: pl.BlockSpec((None, nc, 128), lambda bb, hp: (bb, 0, col // 128 + hp))
    return pl.pallas_call(
        functools.partial(_ctx_attn_kernel, scale=HEAD_DIM ** -0.5),
        grid=(b, NA_HEADS // 2),
        in_specs=[spec(COL_Q), spec(COL_K), spec(COL_V)],
        out_specs=pl.BlockSpec((None, nc, 128), lambda bb, hp: (bb, 0, hp)),
        out_shape=jax.ShapeDtypeStruct((b, nc, BRANCH_W), BF16),
        compiler_params=_cparams(("arbitrary", "arbitrary")),
        name="ctx_attention",
    )(p_ctx, p_ctx, p_ctx)


def _filter_features(t_len):
    pos = jnp.arange(t_len, dtype=F32)
    t = pos / max(t_len - 1, 1)
    w = 2.0 * math.pi * pos / t_len
    bands = jnp.linspace(1e-4, HY_BANDS - 1, HY_BANDS, dtype=F32)
    feats = jnp.concatenate([t[:, None], jnp.cos(w[:, None] * bands), -jnp.sin(w[:, None] * bands)], axis=-1)
    feats2 = jnp.concatenate([feats, feats[:1], feats[:0:-1]], axis=0)
    return jnp.pad(feats2, ((0, 0), (0, 128 - HY_EMB)))


def _filter_kernel(f_ref, w1_ref, b1_ref, fr_ref, w2_ref, b2_ref, w3_ref, dl_ref, k_ref, ss_ref):
    f = f_ref[...]
    fr = fr_ref[...]
    h = jnp.sin(fr * (jnp.dot(f, w1_ref[...], preferred_element_type=F32, precision=HIGHEST) + b1_ref[...]))
    h = jnp.sin(fr * (jnp.dot(h, w2_ref[...], preferred_element_type=F32, precision=HIGHEST) + b2_ref[...]))
    k = jnp.dot(h, w3_ref[...], preferred_element_type=F32, precision=HIGHEST)
    k = k * jnp.exp(-f[:, 0:1] * dl_ref[...])
    k_ref[...] = k

    @pl.when(pl.program_id(0) == 0)
    def _():
        ss_ref[...] = jnp.zeros_like(ss_ref)

    ss_ref[...] += jnp.sum(k * k, axis=0, keepdims=True)


def _hyena_filter_time(t_len, w1, b1, freq, w2, b2, w3):
    n = 2 * t_len
    tm = min(1024, t_len)
    feats = _filter_features(t_len)
    w1p = jnp.pad(w1, ((0, 128 - HY_EMB), (0, 0)))
    deltas = jnp.abs(jnp.linspace(HY_MIN_DECAY, HY_MAX_DECAY, BRANCH_W, dtype=F32))
    dl = jnp.concatenate([deltas, deltas]).reshape(1, 2 * BRANCH_W)
    half = t_len // tm
    nco = 2 * BRANCH_W
    return pl.pallas_call(
        _filter_kernel,
        grid=(n // tm,),
        in_specs=[
            pl.BlockSpec((tm, 128), lambda i: (i, 0)),
            pl.BlockSpec((128, HY_HID), lambda i: (0, 0)),
            pl.BlockSpec((1, HY_HID), lambda i: (0, 0)),
            pl.BlockSpec((1, HY_HID), lambda i: (0, 0)),
            pl.BlockSpec((HY_HID, HY_HID), lambda i: (0, 0)),
            pl.BlockSpec((1, HY_HID), lambda i: (0, 0)),
            pl.BlockSpec((HY_HID, nco), lambda i: (0, i // half)),
            pl.BlockSpec((1, nco), lambda i: (0, 0)),
        ],
        out_specs=[pl.BlockSpec((tm, nco), lambda i: (i, 0)), pl.BlockSpec((1, nco), lambda i: (0, 0))],
        out_shape=[jax.ShapeDtypeStruct((n, nco), F32), jax.ShapeDtypeStruct((1, nco), F32)],
        compiler_params=_cparams(("arbitrary",)),
        name="hyena_filter_ffn_ctx",
    )(feats, w1p, b1.reshape(1, -1), freq.reshape(1, -1), w2, b2.reshape(1, -1), w3, dl)


def _filter_cm_kernel(f_ref, t_ref, w1_ref, b1_ref, fr_ref, w2_ref, b2_ref, w3_ref, dl_ref, k_ref, ss_ref):
    fr = fr_ref[...]
    h = jnp.sin(fr * (jnp.dot(w1_ref[...], f_ref[...], preferred_element_type=F32, precision=HIGHEST) + b1_ref[...]))
    h = jnp.sin(fr * (jnp.dot(w2_ref[...], h, preferred_element_type=F32, precision=HIGHEST) + b2_ref[...]))
    k = lax.dot_general(h.astype(BF16), w3_ref[...].astype(BF16), (((0,), (0,)), ((), ())), preferred_element_type=F32)
    k = k * jnp.exp(-t_ref[...] * dl_ref[...])
    rows = k.shape[0] // FFT_N2
    for cb in range(k.shape[1] // 128):
        piece = k[:, cb * 128:(cb + 1) * 128].reshape(rows, FFT_N2, 128)
        k_ref[cb * 128:(cb + 1) * 128] = jnp.swapaxes(jnp.swapaxes(piece, 1, 2), 0, 1)

    @pl.when(pl.program_id(0) == 0)
    def _():
        ss_ref[...] = jnp.zeros_like(ss_ref)

    ss_ref[...] += jnp.sum(k * k, axis=0, keepdims=True)


def _hyena_filter_cm(t_len, w1, b1, freq, w2, b2, w3):
    n = 2 * t_len
    tm = 2048
    assert t_len % tm == 0
    slot = jnp.arange(n)
    pos = jnp.where(slot < t_len, slot, jnp.where(slot == t_len, 0, n - slot)).astype(F32)
    t = pos / max(t_len - 1, 1)
    w = 2.0 * math.pi * pos / t_len
    bands = jnp.linspace(1e-4, HY_BANDS - 1, HY_BANDS, dtype=F32)
    feats = jnp.concatenate([t[None, :], jnp.cos(w[None, :] * bands[:, None]), -jnp.sin(w[None, :] * bands[:, None])],
                            axis=0)
    feats = jnp.pad(feats, ((0, 128 - HY_EMB), (0, 0)))
    w1t = jnp.pad(w1, ((0, 128 - HY_EMB), (0, 0))).T
    deltas = jnp.abs(jnp.linspace(HY_MIN_DECAY, HY_MAX_DECAY, BRANCH_W, dtype=F32))
    nco = 2 * BRANCH_W
    dl = jnp.concatenate([deltas, deltas]).reshape(1, nco)
    half = t_len // tm
    col = lambda a: a.reshape(-1, 1)
    const = lambda shape: pl.BlockSpec(shape, lambda i: (0, 0))
    return pl.pallas_call(
        _filter_cm_kernel,
        grid=(n // tm,),
        in_specs=[
            pl.BlockSpec((128, tm), lambda i: (0, i)),
            pl.BlockSpec((tm, 1), lambda i: (i, 0)),
            const((HY_HID, 128)), const((HY_HID, 1)), const((HY_HID, 1)),
            const((HY_HID, HY_HID)), const((HY_HID, 1)),
            pl.BlockSpec((HY_HID, nco), lambda i: (0, i // half)),
            const((1, nco)),
        ],
        out_specs=[pl.BlockSpec((nco, tm // FFT_N2, FFT_N2), lambda i: (0, i, 0)), const((1, nco))],
        out_shape=[jax.ShapeDtypeStruct((nco, n // FFT_N2, FFT_N2), F32), jax.ShapeDtypeStruct((1, nco), F32)],
        compiler_params=_cparams(("arbitrary",)),
        name="hyena_filter_ffn",
    )(feats, col(t), w1t, col(b1), col(freq), w2.T, col(b2), w3, dl)


def _dft_consts(n1):
    n2 = FFT_N2
    n = n1 * n2
    nh = n1 // 2
    a1 = -2.0 * np.pi * np.outer(np.arange(n1), np.arange(n1)) / n1
    f1r, f1i = np.cos(a1), np.sin(a1)
    a2 = -2.0 * np.pi * np.outer(np.arange(n2), np.arange(n2)) / n2
    f2r, f2i = np.cos(a2), np.sin(a2)
    at = -2.0 * np.pi * np.outer(np.arange(n1), np.arange(n2)) / n
    bf = lambda a: jnp.asarray(a.astype(np.float32)).astype(BF16)
    return dict(
        s1_data=bf(np.block([[f1r[:, :nh], -f1i[:, :nh]], [f1i[:, :nh], f1r[:, :nh]]])),
        s1_real=bf(np.concatenate([f1r, f1i], axis=0)),
        s6=bf(np.block([[f1r[:nh], f1i[:nh]], [-f1i[:nh], f1r[:nh]]]) / n),
        fa=bf(np.concatenate([f2r, f2i], axis=1)), fb=bf(np.concatenate([-f2i, f2r], axis=1)),
        ia=bf(np.concatenate([f2r, -f2i], axis=1)), ib=bf(np.concatenate([f2i, f2r], axis=1)),
        tw_r=bf(np.cos(at)), tw_i=bf(np.sin(at)),
    )


def _lane_dft(ar_ref, ai_ref, fa_ref, fb_ref):
    cb, n1, n2 = ar_ref.shape
    ar = ar_ref[...].reshape(cb * n1, n2)
    ai = ai_ref[...].reshape(cb * n1, n2)
    x = (jnp.dot(ar, fa_ref[...], preferred_element_type=F32)
         + jnp.dot(ai, fb_ref[...], preferred_element_type=F32))
    return x[:, :n2], x[:, n2:]


def _kf_kernel(k_ref, ss_ref, s1_ref, twr_ref, twi_ref, fa_ref, fb_ref, o_ref, ar_ref, ai_ref, *, cb, n1):
    i = pl.program_id(0)
    row = lax.broadcasted_iota(jnp.int32, (n1, FFT_N2), 0)
    lane = lax.broadcasted_iota(jnp.int32, (n1, FFT_N2), 1)
    keep = jnp.logical_not((row == n1 // 2) & (lane == 0))
    twr, twi = twr_ref[...], twi_ref[...]
    for c in range(cb):
        scale = lax.rsqrt(jnp.full((n1, FFT_N2), ss_ref[i * cb + c], F32) + EPS)
        k = jnp.where(keep, k_ref[c] * scale, 0.0)
        a = jnp.dot(s1_ref[...], k.astype(BF16), preferred_element_type=F32)
        a_r, a_i = a[:n1].astype(BF16), a[n1:].astype(BF16)
        ar_ref[c] = a_r * twr - a_i * twi
        ai_ref[c] = a_r * twi + a_i * twr
    xr, xi = _lane_dft(ar_ref, ai_ref, fa_ref, fb_ref)
    o_ref[:, :n1, :] = xr.reshape(cb, n1, FFT_N2).astype(o_ref.dtype)
    o_ref[:, n1:, :] = xi.reshape(cb, n1, FFT_N2).astype(o_ref.dtype)


def _hyena_filter_spectrum(kt, ss, dft):
    nco, n1, _ = kt.shape
    cb = 8
    const = lambda shape: pl.BlockSpec(shape, lambda i: (0,) * len(shape))
    return pl.pallas_call(
        functools.partial(_kf_kernel, cb=cb, n1=n1),
        grid=(nco // cb,),
        in_specs=[
            pl.BlockSpec((cb, n1, FFT_N2), lambda i: (i, 0, 0)),
            pl.BlockSpec(memory_space=pltpu.SMEM),
            const((2 * n1, n1)), const((n1, FFT_N2)), const((n1, FFT_N2)),
            const((FFT_N2, 2 * FFT_N2)), const((FFT_N2, 2 * FFT_N2)),
        ],
        out_specs=pl.BlockSpec((cb, 2 * n1, FFT_N2), lambda i: (i, 0, 0)),
        out_shape=jax.ShapeDtypeStruct((nco, 2 * n1, FFT_N2), BF16),
        scratch_shapes=[pltpu.VMEM((cb, n1, FFT_N2), BF16), pltpu.VMEM((cb, n1, FFT_N2), BF16)],
        compiler_params=_cparams(("arbitrary",)),
        name="hyena_filter_fft",
    )(kt, ss.reshape(nco), dft["s1_real"], dft["tw_r"], dft["tw_i"], dft["fa"], dft["fb"])


def _tok2cm_kernel(x_ref, o_ref):
    rows = x_ref.shape[0] // FFT_N2
    for cb in range(x_ref.shape[1] // 128):
        x3 = x_ref[:, cb * 128:(cb + 1) * 128].reshape(rows, FFT_N2, 128)
        o_ref[cb * 128:(cb + 1) * 128] = jnp.swapaxes(jnp.swapaxes(x3, 1, 2), 0, 1)


def _to_channel_major(p, col0, ncols):
    b, t, _ = p.shape
    tt = 16 * FFT_N2
    nt = t // tt
    cw = 256
    return pl.pallas_call(
        _tok2cm_kernel,
        grid=(ncols // cw, b, nt),
        in_specs=[pl.BlockSpec((None, tt, cw), lambda c, bb, i: (bb, i, col0 // cw + c))],
        out_specs=pl.BlockSpec((cw, tt // FFT_N2, FFT_N2), lambda c, bb, i: (c, bb * nt + i, 0)),
        out_shape=jax.ShapeDtypeStruct((ncols, b * t // FFT_N2, FFT_N2), p.dtype),
        compiler_params=_cparams(("arbitrary", "arbitrary", "arbitrary")),
        name="to_channel_major",
    )(p)


def _cm2tok_kernel(x_ref, o_ref):
    x3 = jnp.swapaxes(jnp.swapaxes(x_ref[...], 0, 1), 1, 2)
    o_ref[...] = x3.reshape(o_ref.shape)


def _to_token_major(y_cm, b):
    c, rows, _ = y_cm.shape
    t = rows * FFT_N2 // b
    tt = 16 * FFT_N2
    nt = t // tt
    return pl.pallas_call(
        _cm2tok_kernel,
        grid=(c // 128, b, nt),
        in_specs=[pl.BlockSpec((128, tt // FFT_N2, FFT_N2), lambda cc, bb, i: (cc, bb * nt + i, 0))],
        out_specs=pl.BlockSpec((None, tt, 128), lambda cc, bb, i: (bb, i, cc)),
        out_shape=jax.ShapeDtypeStruct((b, t, c), y_cm.dtype),
        compiler_params=_cparams(("arbitrary", "arbitrary", "arbitrary")),
        name="to_token_major",
    )(y_cm)


def _hyena_kernel(v_ref, x1_ref, x2_ref, kf1_ref, kf2_ref, cw_ref, cbias_ref, dd_ref,
                  s1_ref, s6_ref, twr_ref, twi_ref, fa_ref, fb_ref, ia_ref, ib_ref,
                  o_ref, vs, x1s, x2s, ar_ref, ai_ref, *, cb, n1, nch):
    i = pl.program_id(0)
    nh = n1 // 2
    rows = 2 * nh
    row = lax.broadcasted_iota(jnp.int32, (rows, FFT_N2), 0)
    lane = lax.broadcasted_iota(jnp.int32, (rows, FFT_N2), 1)
    first = lane == 0
    last = lane == FFT_N2 - 1
    seq_start = first & ((row % nh) == 0)
    seq_end = last & ((row % nh) == nh - 1)
    twr, twi = twr_ref[...], twi_ref[...]

    def short_conv(x, ch):
        prev = pltpu.roll(x, 1, 1)
        prev = jnp.where(first, pltpu.roll(prev, 1, 0), prev)
        prev = jnp.where(seq_start, 0.0, prev)
        nxt = pltpu.roll(x, FFT_N2 - 1, 1)
        nxt = jnp.where(last, pltpu.roll(nxt, rows - 1, 0), nxt)
        nxt = jnp.where(seq_end, 0.0, nxt)
        return cw_ref[ch] * prev + cw_ref[nch + ch] * x + cw_ref[2 * nch + ch] * nxt + cbias_ref[ch]

    def stage1_all(zs):
        acc = [jnp.dot(s1_ref[...], z.astype(BF16), preferred_element_type=F32) for z in zs]
        for c, a in enumerate(acc):
            a_r, a_i = a[:n1].astype(BF16), a[n1:].astype(BF16)
            ar_ref[c] = a_r * twr - a_i * twi
            ai_ref[c] = a_r * twi + a_i * twr

    def spectral(kf_ref):
        xr, xi = _lane_dft(ar_ref, ai_ref, fa_ref, fb_ref)
        kr = kf_ref[:, :n1, :].reshape(cb * n1, FFT_N2)
        ki = kf_ref[:, n1:, :].reshape(cb * n1, FFT_N2)
        xr, xi = xr.astype(BF16), xi.astype(BF16)
        yr = xr * kr - xi * ki
        yi = xr * ki + xi * kr
        bm = (jnp.dot(yr, ia_ref[...], preferred_element_type=F32)
              + jnp.dot(yi, ib_ref[...], preferred_element_type=F32))
        br = bm[:, :FFT_N2].astype(BF16).reshape(cb, n1, FFT_N2)
        bi = bm[:, FFT_N2:].astype(BF16).reshape(cb, n1, FFT_N2)
        ar_ref[...] = br * twr[None] + bi * twi[None]
        ai_ref[...] = bi * twr[None] - br * twi[None]

    def stage6(c):
        bcat = jnp.concatenate([ar_ref[c], ai_ref[c]], axis=0)
        return jnp.dot(s6_ref[...], bcat, preferred_element_type=F32)

    nbr = BRANCH_W
    for c in range(cb):
        ch = i * cb + c
        vs[c] = short_conv(v_ref[c].astype(F32), ch)
        x1s[c] = short_conv(x1_ref[c].astype(F32), nbr + ch)
        x2s[c] = short_conv(x2_ref[c].astype(F32), 2 * nbr + ch)
    stage1_all([vs[c] for c in range(cb)])
    spectral(kf1_ref)
    ys = [stage6(c) for c in range(cb)]
    for c in range(cb):
        vs[c] = x1s[c] * (ys[c] + vs[c] * dd_ref[i * cb + c])
    stage1_all([vs[c] for c in range(cb)])
    spectral(kf2_ref)
    ys = [stage6(c) for c in range(cb)]
    for c in range(cb):
        o_ref[c] = (x2s[c] * (ys[c] + vs[c] * dd_ref[nbr + i * cb + c])).astype(o_ref.dtype)


def _hyena_lat(hy_cm, kf, conv_w, conv_b, hy_d, dft):
    nch, rows, _ = hy_cm.shape
    n1 = rows
    cb = 8
    nblk = BRANCH_W // cb
    const = lambda shape: pl.BlockSpec(shape, lambda i: (0,) * len(shape))
    smem = pl.BlockSpec(memory_space=pltpu.SMEM)
    data = lambda sec: pl.BlockSpec((cb, rows, FFT_N2), lambda i: (sec * nblk + i, 0, 0))
    kfs = lambda o: pl.BlockSpec((cb, 2 * n1, FFT_N2), lambda i: (o * nblk + i, 0, 0))
    return pl.pallas_call(
        functools.partial(_hyena_kernel, cb=cb, n1=n1, nch=nch),
        grid=(nblk,),
        in_specs=[
            data(0), data(1), data(2), kfs(0), kfs(1), smem, smem, smem,
            const((2 * n1, n1)), const((n1, 2 * n1)), const((n1, FFT_N2)), const((n1, FFT_N2)),
            const((FFT_N2, 2 * FFT_N2)), const((FFT_N2, 2 * FFT_N2)),
            const((FFT_N2, 2 * FFT_N2)), const((FFT_N2, 2 * FFT_N2)),
        ],
        out_specs=pl.BlockSpec((cb, rows, FFT_N2), lambda i: (i, 0, 0)),
        out_shape=jax.ShapeDtypeStruct((BRANCH_W, rows, FFT_N2), BF16),
        scratch_shapes=[pltpu.VMEM((cb, rows, FFT_N2), F32) for _ in range(3)]
        + [pltpu.VMEM((cb, n1, FFT_N2), BF16) for _ in range(2)],
        compiler_params=_cparams(("arbitrary",)),
        name="hyena_fftconv",
    )(hy_cm, hy_cm, hy_cm, kf, kf, conv_w.reshape(-1), conv_b, hy_d.reshape(-1),
      dft["s1_data"], dft["s6"], dft["tw_r"], dft["tw_i"], dft["fa"], dft["fb"], dft["ia"], dft["ib"])


def _hyena_ctx_kernel(hy_ref, k_ref, ss_ref, cw_ref, cbias_ref, dd_ref, fwd_a_ref, fwd_b_ref,
                      inv_a_ref, inv_b_ref, o_ref, *, t_len):
    nbr = BRANCH_W
    lane = lax.broadcasted_iota(jnp.int32, (nbr, t_len), 1)

    def short_conv(x, sec):
        sl = slice(sec * nbr, (sec + 1) * nbr)
        prev = jnp.where(lane == 0, 0.0, pltpu.roll(x, 1, 1))
        nxt = jnp.where(lane == t_len - 1, 0.0, pltpu.roll(x, t_len - 1, 1))
        return cw_ref[0, sl, :] * prev + cw_ref[1, sl, :] * x + cw_ref[2, sl, :] * nxt + cbias_ref[sl, :]

    def sec(b, s):
        return short_conv(hy_ref[b, s * nbr:(s + 1) * nbr, :].astype(F32), s)

    klane = lax.broadcasted_iota(jnp.int32, k_ref.shape, 1)
    kk = jnp.where(klane == t_len, 0.0, k_ref[...] * lax.rsqrt(ss_ref[...] + EPS))
    kf = jnp.dot(kk.astype(BF16), fwd_a_ref[...], preferred_element_type=F32)
    n = 2 * t_len

    def conv(z0, z1, order):
        x = (jnp.dot(z0.astype(BF16), fwd_a_ref[:t_len, :], preferred_element_type=F32)
             + jnp.dot(z1.astype(BF16), fwd_b_ref[:t_len, :], preferred_element_type=F32))
        xr, xi = x[:, :n], x[:, n:]
        kr = kf[order * nbr:(order + 1) * nbr, :n]
        ki = kf[order * nbr:(order + 1) * nbr, n:]
        yr = (xr * kr - xi * ki).astype(BF16)
        yi = (xr * ki + xi * kr).astype(BF16)
        y = (jnp.dot(yr, inv_a_ref[...], preferred_element_type=F32)
             + jnp.dot(yi, inv_b_ref[...], preferred_element_type=F32))
        return y[:, :t_len], y[:, t_len:]

    v0, v1 = sec(0, 0), sec(1, 0)
    y0, y1 = conv(v0, v1, 0)
    d1 = dd_ref[:nbr, :]
    d2 = dd_ref[nbr:, :]
    z0 = sec(0, 1) * (y0 + v0 * d1)
    z1 = sec(1, 1) * (y1 + v1 * d1)
    y0, y1 = conv(z0, z1, 1)
    o_ref[0] = (sec(0, 2) * (y0 + z0 * d2)).astype(o_ref.dtype)
    o_ref[1] = (sec(1, 2) * (y1 + z1 * d2)).astype(o_ref.dtype)


def _hyena_ctx(hy_ctx, k_time, ss, conv_w, conv_b, hy_d):
    b, t_len, nch = hy_ctx.shape
    n = 2 * t_len
    ang = -2.0 * np.pi * np.outer(np.arange(n), np.arange(n)) / n
    fr, fi = np.cos(ang), np.sin(ang)
    bf = lambda a: jnp.asarray(a.astype(np.float32)).astype(BF16)
    fwd_a = bf(np.concatenate([fr, fi], axis=1))
    fwd_b = bf(np.concatenate([-fi, fr], axis=1))
    inv_a = bf(np.concatenate([fr[:, :t_len], -fi[:, :t_len]], axis=1) / n)
    inv_b = bf(np.concatenate([fi[:, :t_len], fr[:, :t_len]], axis=1) / n)
    out = pl.pallas_call(
        functools.partial(_hyena_ctx_kernel, t_len=t_len),
        out_shape=jax.ShapeDtypeStruct((b, BRANCH_W, t_len), BF16),
        compiler_params=pltpu.CompilerParams(vmem_limit_bytes=VMEM_LIMIT),
        name="hyena_ctx",
    )(hy_ctx.transpose(0, 2, 1), k_time.T, ss.reshape(-1, 1), conv_w.reshape(3, nch, 1),
      conv_b.reshape(nch, 1), hy_d.reshape(-1, 1), fwd_a, fwd_b, inv_a, inv_b)
    return out.transpose(0, 2, 1)


def _merge_kernel(x_ref, ya_ref, yb_ref, yc_ref, za_ref, zb_ref, zc_ref, g0_ref, g1_ref, g2_ref, mod_ref,
                  gw_ref, gb_ref, wb_ref, wo_ref, fg_ref, o_ref, *, mod_row, d, final):
    ya = jax.nn.gelu(ya_ref[...].astype(F32)).astype(BF16)
    glu = jnp.dot(ya, gw_ref[...], preferred_element_type=F32) + gb_ref[...]
    y_a = glu[:, :BRANCH_W] * (1.0 + jnp.tanh(glu[:, BRANCH_W:]))

    def branch(i, y, z_ref, g_ref):
        zh = z_ref[...]
        u = (zh + zh * jnp.tanh(zh)) * y
        return (1.0 + jnp.tanh(g_ref[...].astype(F32))) * jnp.dot(u, wb_ref[i], preferred_element_type=F32)

    acc = branch(0, y_a.astype(BF16), za_ref, g0_ref)
    acc = acc + branch(1, yb_ref[...], zb_ref, g1_ref)
    acc = acc + branch(2, yc_ref[...], zc_ref, g2_ref)
    out = jnp.dot(acc.astype(BF16), wo_ref[...], preferred_element_type=F32)
    if mod_row is None:
        m = mod_ref[pl.ds(pl.program_id(0), 1), :]
    else:
        m = mod_ref[mod_row:mod_row + 1, :]
    xn = x_ref[...] + m[:, 2 * d:] * out
    if final:
        ms = jnp.mean(xn * xn, axis=-1, keepdims=True)
        xn = xn * lax.rsqrt(ms + EPS) * fg_ref[...]
    o_ref[...] = xn


def _merge(x, ya, yb, yc, p, mod, glu_w, glu_b, w_branch, w_out, final_g, *, mod_row, tm, final):
    b, t, d = x.shape
    tok = lambda w, col: pl.BlockSpec((None, tm, w), lambda bb, i: (bb, i, col // w))
    const = lambda shape: pl.BlockSpec(shape, lambda bb, i: (0,) * len(shape))
    return pl.pallas_call(
        functools.partial(_merge_kernel, mod_row=mod_row, d=d, final=final),
        grid=(b, t // tm),
        in_specs=[
            tok(d, 0), tok(512, 0), tok(512, 0), tok(512, 0),
            tok(512, COL_ZA), tok(512, COL_ZB), tok(512, COL_ZC),
            tok(1024, COL_GT), tok(1024, COL_GT + 1024), tok(1024, COL_GT + 2048),
            const((8, 3 * d)), const((BRANCH_W, 2 * BRANCH_W)), const((1, 2 * BRANCH_W)),
            const((3, BRANCH_W, d)), const((d, d)), const((1, d)),
        ],
        out_specs=tok(d, 0),
        out_shape=jax.ShapeDtypeStruct((b, t, d), F32),
        compiler_params=_cparams(("arbitrary", "arbitrary")),
        name="merge_out",
    )(x, ya, yb, yc, p, p, p, p, p, p, mod, glu_w, glu_b.reshape(1, -1), w_branch, w_out, final_g.reshape(1, d))


def kernel(x, c, ctx, c_ctx, ada_w, ada_b, norm_g, w_in, s5_lam_re, s5_lam_im, s5_log_dt, s5_b_re, s5_b_im,
           s5_c_re, s5_c_im, s5_d, s5_glu_w, s5_glu_b, na_rpb, hy_conv_w, hy_conv_b, hf_w1, hf_b1, hf_freq,
           hf_w2, hf_b2, hf_w3, hy_d, w_branch, w_out, final_g):
    bsz, t_lat, d = x.shape
    t_ctx = ctx.shape[1]
    depth = ada_w.shape[0]
    assert bsz == 2 and d == 1024 and t_lat % (16 * FFT_N2) == 0 and t_ctx % 128 == 0

    cvec = jnp.zeros((8, d), F32).at[:bsz].set(c).at[bsz].set(c_ctx)
    mods = _modulation(cvec, ada_w, ada_b)
    cos_t, sin_t = _rope_tables(t_lat)
    n1 = 2 * t_lat // FFT_N2
    dft = _dft_consts(n1)
    tm_lat = 1024 if t_lat % 1024 == 0 else 512

    x_lat, x_ctx = x, ctx
    for l in range(depth):
        ctx_out = l < depth - 1
        half_cols = jnp.asarray(np.where(np.isin(np.arange(IN_COLS) // BRANCH_W, (1, 5, 9)) | (np.arange(IN_COLS) >= COL_GT),
                                         0.5, 1.0).astype(np.float32))
        w_bf = (w_in[l] * half_cols).astype(BF16)
        p_lat = _inproj(x_lat, mods[l], norm_g[l], w_bf, mod_row=None, tm=tm_lat)
        p_ctx = _inproj(x_ctx, mods[l], norm_g[l], w_bf, mod_row=bsz, tm=t_ctx)

        ops = _s5_operators(s5_lam_re[l], s5_lam_im[l], s5_log_dt[l], s5_b_re[l], s5_b_im[l],
                            s5_c_re[l], s5_c_im[l], s5_d[l])
        ya, ya_c = _s5_mixer(p_lat, p_ctx, ops)

        qr, kt = _rope_qk(p_lat, cos_t, sin_t)
        bias = _na_bias_tables(na_rpb[l])
        yb = _na_mixer(qr, kt, p_lat, p_ctx, bias)

        k_cm, ss = _hyena_filter_cm(t_lat, hf_w1[l], hf_b1[l], hf_freq[l], hf_w2[l], hf_b2[l], hf_w3[l])
        kf = _hyena_filter_spectrum(k_cm, ss, dft)
        hy_cm = _to_channel_major(p_lat, COL_HY, 3 * BRANCH_W)
        yc_cm = _hyena_lat(hy_cm, kf, hy_conv_w[l], hy_conv_b[l], hy_d[l], dft)
        yc = _to_token_major(yc_cm, bsz)

        wb_bf = (0.5 * w_branch[l]).astype(BF16)
        wo_bf = w_out[l].astype(BF16)
        gw_bf = (0.5 * s5_glu_w[l]).astype(BF16)
        glu_b_half = 0.5 * s5_glu_b[l]
        x_lat_new = _merge(x_lat, ya, yb, yc, p_lat, mods[l], gw_bf, glu_b_half, wb_bf, wo_bf,
                           final_g, mod_row=None, tm=512, final=not ctx_out)
        if ctx_out:
            yb_c = _ctx_attention(p_ctx)
            kc_time, ss_c = _hyena_filter_time(t_ctx, hf_w1[l], hf_b1[l], hf_freq[l], hf_w2[l], hf_b2[l], hf_w3[l])
            yc_c = _hyena_ctx(p_ctx[:, :, COL_HY:COL_HY + 1536], kc_time, ss_c, hy_conv_w[l], hy_conv_b[l], hy_d[l])
            x_ctx = _merge(x_ctx, ya_c, yb_c, yc_c, p_ctx, mods[l], gw_bf, glu_b_half, wb_bf, wo_bf,
                           final_g, mod_row=bsz, tm=t_ctx, final=False)
        x_lat = x_lat_new
    return x_lat
```
